```python
import math
import jax, jax.numpy as jnp
from jax import lax
import numpy as np

D_MODEL = 2048
BATCH = 1
SEQ = 8192
DEPTH = 1

D_MIX = D_MODEL
SSM_WIDTH = D_MIX // 2
CONV_WIDTH_CH = D_MIX - SSM_WIDTH
SSM_GROUP_CH = 16
SSM_GROUPS = SSM_WIDTH // SSM_GROUP_CH
SSM_STATE = 64
CONV_HEADS = 16
CONV_HEAD_DIM = CONV_WIDTH_CH // CONV_HEADS
CONV_K = 3
D_IN_PROJ = SSM_WIDTH + 3 * CONV_WIDTH_CH
D_FF = 5632
DT_MIN = 1e-3
DT_MAX = 1e-1
LN_EPS = 1e-5
RMS_EPS = 1e-6
ALPHA = (2.0 * DEPTH) ** 0.25
BETA = (8.0 * DEPTH) ** (-0.25)

kernel_name = "hymba_s5_shortconv_convffn_deepnorm"


def _layernorm(x, g, b):
    xf = x.astype(jnp.float32)
    mu = jnp.mean(xf, axis=-1, keepdims=True)
    xc = xf - mu
    var = jnp.mean(jnp.square(xc), axis=-1, keepdims=True)
    y = xc * lax.rsqrt(var + LN_EPS) * g.astype(jnp.float32) + b.astype(jnp.float32)
    return y.astype(x.dtype)


def _rmsnorm(x, g):
    xf = x.astype(jnp.float32)
    y = xf * lax.rsqrt(jnp.mean(jnp.square(xf), axis=-1, keepdims=True) + RMS_EPS)
    return (y * g.astype(jnp.float32)).astype(x.dtype)


def _causal_dwconv3(x, w):
    L = x.shape[1]
    xp = jnp.pad(x, ((0, 0), (CONV_K - 1, 0), (0, 0)))
    return xp[:, 0:L] * w[0] + xp[:, 1:L + 1] * w[1] + xp[:, 2:L + 2] * w[2]


def _s5_mixer(u, lam_re, lam_im, log_dt, b_re, b_im, c_re, c_im, d_skip, glu_w, glu_b):
    f32 = jnp.float32
    bsz, L, _ = u.shape
    ug = u.astype(f32).reshape(bsz, L, SSM_GROUPS, SSM_GROUP_CH)
    lam = lax.complex(lam_re.astype(f32), lam_im.astype(f32))
    dt = jnp.exp(log_dt.astype(f32))[:, None]
    lam_bar = jnp.exp(lam * dt)
    b = lax.complex(b_re.astype(f32), b_im.astype(f32))
    b_bar = ((lam_bar - 1.0) / lam)[..., None] * b
    bu = jnp.einsum('gph,blgh->blgp', b_bar, ug.astype(jnp.complex64))
    a = jnp.broadcast_to(lam_bar, bu.shape)

    def combine(left, right):
        a_l, b_l = left
        a_r, b_r = right
        return a_r * a_l, a_r * b_l + b_r

    _, states = lax.associative_scan(combine, (a, bu), axis=1)
    c = lax.complex(c_re.astype(f32), c_im.astype(f32))
    y = jnp.einsum('ghp,blgp->blgh', c, states).real \
        + d_skip.astype(f32).reshape(SSM_GROUPS, SSM_GROUP_CH) * ug
    y = y.reshape(bsz, L, SSM_WIDTH)
    g = jax.nn.gelu(y)
    out = g * jax.nn.sigmoid(g @ glu_w.astype(f32) + glu_b.astype(f32))
    return out.astype(u.dtype)


def _short_conv_mixer(gate_b, gate_c, v, conv_w):
    return gate_b * _causal_dwconv3(gate_c * v, conv_w)


def _conv_ffn(x, w_gate, w_up, conv_w, conv_b, w_down):
    a = _causal_dwconv3(x @ w_gate, conv_w) + conv_b
    return (jax.nn.silu(a) * (x @ w_up)) @ w_down


def setup_inputs(seed: int = 0) -> dict:
    key = jax.random.key(seed)
    ks = jax.random.split(key, 26)
    f32 = jnp.float32
    nrm = lambda k, shape, s: jax.random.normal(k, shape, f32) * s
    P, G, H = SSM_STATE, SSM_GROUPS, SSM_GROUP_CH

    x = jax.random.normal(ks[0], (BATCH, SEQ, D_MODEL), f32)
    w_in = nrm(ks[1], (DEPTH, D_MODEL, D_IN_PROJ), D_MODEL ** -0.5)
    ssm_lambda_re = -0.5 + nrm(ks[2], (DEPTH, G, P), 0.01)
    ssm_lambda_im = math.pi * jnp.arange(P, dtype=f32)[None, None, :] + nrm(ks[3], (DEPTH, G, P), 0.01)
    ssm_log_dt = jax.random.uniform(ks[4], (DEPTH, G), f32, minval=math.log(DT_MIN), maxval=math.log(DT_MAX))
    ssm_b_re = nrm(ks[5], (DEPTH, G, P, H), (2.0 * H) ** -0.5)
    ssm_b_im = nrm(ks[6], (DEPTH, G, P, H), (2.0 * H) ** -0.5)
    ssm_c_re = nrm(ks[7], (DEPTH, G, H, P), (2.0 * P) ** -0.5)
    ssm_c_im = nrm(ks[8], (DEPTH, G, H, P), (2.0 * P) ** -0.5)
    ssm_d = 1.0 + nrm(ks[9], (DEPTH, SSM_WIDTH), 0.1)
    ssm_glu_w = nrm(ks[10], (DEPTH, SSM_WIDTH, SSM_WIDTH), SSM_WIDTH ** -0.5)
    ssm_glu_b = nrm(ks[11], (DEPTH, SSM_WIDTH), 0.01)
    sconv_w = nrm(ks[12], (DEPTH, CONV_K, CONV_WIDTH_CH), CONV_K ** -0.5)
    norm_ssm_g = 1.0 + nrm(ks[13], (DEPTH, SSM_WIDTH), 0.02)
    norm_conv_g = 1.0 + nrm(ks[14], (DEPTH, CONV_WIDTH_CH), 0.02)
    w_out = nrm(ks[15], (DEPTH, D_MIX, D_MODEL), BETA * D_MIX ** -0.5)
    ln1_g = 1.0 + nrm(ks[16], (DEPTH, D_MODEL), 0.02)
    ln1_b = nrm(ks[17], (DEPTH, D_MODEL), 0.01)
    ffn_w_gate = nrm(ks[18], (DEPTH, D_MODEL, D_FF), D_MODEL ** -0.5)
    ffn_w_up = nrm(ks[19], (DEPTH, D_MODEL, D_FF), D_MODEL ** -0.5)
    ffn_conv_w = nrm(ks[20], (DEPTH, CONV_K, D_FF), CONV_K ** -0.5)
    ffn_conv_b = nrm(ks[21], (DEPTH, D_FF), 0.01)
    ffn_w_down = nrm(ks[22], (DEPTH, D_FF, D_MODEL), BETA * D_FF ** -0.5)
    ln2_g = 1.0 + nrm(ks[23], (DEPTH, D_MODEL), 0.02)
    ln2_b = nrm(ks[24], (DEPTH, D_MODEL), 0.01)
    return {
        "x": x, "w_in": w_in,
        "ssm_lambda_re": ssm_lambda_re, "ssm_lambda_im": ssm_lambda_im, "ssm_log_dt": ssm_log_dt,
        "ssm_b_re": ssm_b_re, "ssm_b_im": ssm_b_im, "ssm_c_re": ssm_c_re, "ssm_c_im": ssm_c_im,
        "ssm_d": ssm_d, "ssm_glu_w": ssm_glu_w, "ssm_glu_b": ssm_glu_b,
        "sconv_w": sconv_w, "norm_ssm_g": norm_ssm_g, "norm_conv_g": norm_conv_g,
        "w_out": w_out, "ln1_g": ln1_g, "ln1_b": ln1_b,
        "ffn_w_gate": ffn_w_gate, "ffn_w_up": ffn_w_up, "ffn_conv_w": ffn_conv_w,
        "ffn_conv_b": ffn_conv_b, "ffn_w_down": ffn_w_down, "ln2_g": ln2_g, "ln2_b": ln2_b,
    }


def reference(x, w_in, ssm_lambda_re, ssm_lambda_im, ssm_log_dt, ssm_b_re, ssm_b_im,
              ssm_c_re, ssm_c_im, ssm_d, ssm_glu_w, ssm_glu_b, sconv_w, norm_ssm_g,
              norm_conv_g, w_out, ln1_g, ln1_b, ffn_w_gate, ffn_w_up, ffn_conv_w,
              ffn_conv_b, ffn_w_down, ln2_g, ln2_b):
    h = x
    for l in range(DEPTH):
        proj = h @ w_in[l]
        u = proj[..., :SSM_WIDTH]
        gate_b, gate_c, v = jnp.split(proj[..., SSM_WIDTH:], 3, axis=-1)
        y_ssm = _s5_mixer(u, ssm_lambda_re[l], ssm_lambda_im[l], ssm_log_dt[l],
                          ssm_b_re[l], ssm_b_im[l], ssm_c_re[l], ssm_c_im[l],
                          ssm_d[l], ssm_glu_w[l], ssm_glu_b[l])
        y_conv = _short_conv_mixer(gate_b, gate_c, v, sconv_w[l])
        y_mix = jnp.concatenate([_rmsnorm(y_ssm, norm_ssm_g[l]),
                                 _rmsnorm(y_conv, norm_conv_g[l])], axis=-1)
        h = _layernorm(ALPHA * h + y_mix @ w_out[l], ln1_g[l], ln1_b[l])
        f = _conv_ffn(h, ffn_w_gate[l], ffn_w_up[l], ffn_conv_w[l], ffn_conv_b[l], ffn_w_down[l])
        h = _layernorm(ALPHA * h + f, ln2_g[l], ln2_b[l])
    return h
```

```python
import functools
import math

import jax
import jax.numpy as jnp
from jax import lax
from jax.experimental import pallas as pl
from jax.experimental.pallas import tpu as pltpu

SEQ = 8192
D_MODEL = 2048
SSM_WIDTH = 1024
CONV_WIDTH = 1024
GROUP_CH = 16
GROUPS = SSM_WIDTH // GROUP_CH
STATE = 64
D_FF = 5632
DEPTH = 1
LN_EPS = 1e-5
RMS_EPS = 1e-6
ALPHA = (2.0 * DEPTH) ** 0.25

SUBLANES = 8
SLAB_GROUPS = 8
N_SLABS = GROUPS // SLAB_GROUPS
SLAB_CH = SLAB_GROUPS * GROUP_CH
SLAB_STATE = SLAB_GROUPS * STATE
SLAB_LANES = 2 * SLAB_STATE

TM_PROJ = 512
TM_SSM = 256
TM_OUT = 512
TM_FFN = 512
TF_FFN = 512
VMEM_LIMIT = 56 * 1024 * 1024

_F32 = jnp.float32
_BF16 = jnp.bfloat16


def _resident(shape):
    return pl.BlockSpec(shape, lambda *_: (0,) * len(shape), pipeline_mode=pl.Buffered(1))


def _layernorm(r, g, b):
    mu = jnp.mean(r, axis=-1, keepdims=True)
    rc = r - mu
    var = jnp.mean(rc * rc, axis=-1, keepdims=True)
    return rc * lax.rsqrt(var + LN_EPS) * g + b


def _rmsnorm(y, g):
    return y * lax.rsqrt(jnp.mean(y * y, axis=-1, keepdims=True) + RMS_EPS) * g


def _causal_taps(cur, halo_ref, stage_ref, first_tile):
    tm = cur.shape[0]

    @pl.when(first_tile)
    def _():
        stage_ref[0:SUBLANES, :] = jnp.zeros((SUBLANES, cur.shape[1]), _F32)

    @pl.when(jnp.logical_not(first_tile))
    def _():
        stage_ref[0:SUBLANES, :] = halo_ref[...]

    stage_ref[SUBLANES:SUBLANES + tm, :] = cur
    halo_ref[...] = cur[tm - SUBLANES:tm, :]
    return stage_ref[SUBLANES - 1:SUBLANES - 1 + tm, :], stage_ref[SUBLANES - 2:SUBLANES - 2 + tm, :]


def _proj_kernel(x_ref, wu_ref, wb_ref, wc_ref, wv_ref, cw_ref, g_ref, u_ref, yc_ref, halo_ref, stage_ref):
    xb = x_ref[...].astype(_BF16)
    u_ref[...] = jnp.dot(xb, wu_ref[...], preferred_element_type=_F32)
    gate_c = jnp.dot(xb, wc_ref[...], preferred_element_type=_F32)
    v = jnp.dot(xb, wv_ref[...], preferred_element_type=_F32)
    cv = gate_c * v
    cv1, cv2 = _causal_taps(cv, halo_ref, stage_ref, pl.program_id(0) == 0)
    conv = cw_ref[0:1, :] * cv2 + cw_ref[1:2, :] * cv1 + cw_ref[2:3, :] * cv
    gate_b = jnp.dot(xb, wb_ref[...], preferred_element_type=_F32)
    yc_ref[...] = _rmsnorm(gate_b * conv, g_ref[...]).astype(_BF16)


def _proj_call(x, w_in_bf, sconv_w, norm_conv_g):
    tm = TM_PROJ
    wspec = lambda j: pl.BlockSpec((D_MODEL, 1024), lambda i, j=j: (0, j), pipeline_mode=pl.Buffered(1))
    return pl.pallas_call(
        _proj_kernel,
        grid=(SEQ // tm,),
        in_specs=[
            pl.BlockSpec((tm, D_MODEL), lambda i: (i, 0)),
            wspec(0), wspec(1), wspec(2), wspec(3),
            _resident((3, CONV_WIDTH)),
            _resident((1, CONV_WIDTH)),
        ],
        out_specs=[
            pl.BlockSpec((tm, SSM_WIDTH), lambda i: (i, 0)),
            pl.BlockSpec((tm, CONV_WIDTH), lambda i: (i, 0)),
        ],
        out_shape=[
            jax.ShapeDtypeStruct((SEQ, SSM_WIDTH), _F32),
            jax.ShapeDtypeStruct((SEQ, CONV_WIDTH), _BF16),
        ],
        scratch_shapes=[
            pltpu.VMEM((SUBLANES, CONV_WIDTH), _F32),
            pltpu.VMEM((SUBLANES + tm, CONV_WIDTH), _F32),
        ],
        compiler_params=pltpu.CompilerParams(dimension_semantics=("arbitrary",), vmem_limit_bytes=VMEM_LIMIT),
        name="proj_shortconv",
    )(x, w_in_bf, w_in_bf, w_in_bf, w_in_bf, sconv_w, norm_conv_g)


def _ssm_kernel(u_ref, bmat_ref, lam_ref, cmat_ref, d_ref, gw_ref, gb_ref, g_ref, ys_ref,
                bu_ref, state_ref, y_ref):
    tm = u_ref.shape[0]

    @pl.when(pl.program_id(0) == 0)
    def _():
        state_ref[...] = jnp.zeros(state_ref.shape, _F32)

    u = u_ref[...]
    ub = u.astype(_BF16)
    for k in range(N_SLABS):
        bu_ref[:, k * SLAB_LANES:(k + 1) * SLAB_LANES] = jnp.dot(
            ub[:, k * SLAB_CH:(k + 1) * SLAB_CH], bmat_ref[k], preferred_element_type=_F32)

    for k in range(N_SLABS):
        re = slice(k * SLAB_LANES, k * SLAB_LANES + SLAB_STATE)
        im = slice(k * SLAB_LANES + SLAB_STATE, (k + 1) * SLAB_LANES)
        lr = lam_ref[k, 0:1, :]
        li = lam_ref[k, 1:2, :]

        def step(t, carry, re=re, im=im, lr=lr, li=li):
            xr, xi = carry
            nxr = lr * xr - li * xi + bu_ref[pl.ds(t, 1), re]
            nxi = lr * xi + li * xr + bu_ref[pl.ds(t, 1), im]
            bu_ref[pl.ds(t, 1), re] = nxr
            bu_ref[pl.ds(t, 1), im] = nxi
            return nxr, nxi

        xr, xi = lax.fori_loop(0, tm, step,
                               (state_ref[k:k + 1, 0:SLAB_STATE], state_ref[k:k + 1, SLAB_STATE:SLAB_LANES]),
                               unroll=8)
        state_ref[k:k + 1, 0:SLAB_STATE] = xr
        state_ref[k:k + 1, SLAB_STATE:SLAB_LANES] = xi

    for k in range(N_SLABS):
        xs = bu_ref[:, k * SLAB_LANES:(k + 1) * SLAB_LANES].astype(_BF16)
        y_ref[:, k * SLAB_CH:(k + 1) * SLAB_CH] = jnp.dot(xs, cmat_ref[k], preferred_element_type=_F32)
    y = y_ref[...] + d_ref[...] * u
    g = jax.nn.gelu(y)
    z = jnp.dot(g.astype(_BF16), gw_ref[...], preferred_element_type=_F32) + gb_ref[...]
    ys_ref[...] = _rmsnorm(g * jax.nn.sigmoid(z), g_ref[...]).astype(_BF16)


def _ssm_call(u, bmat, lam, cmat, d_skip, glu_w_bf, glu_b, norm_g):
    tm = TM_SSM
    return pl.pallas_call(
        _ssm_kernel,
        grid=(SEQ // tm,),
        in_specs=[
            pl.BlockSpec((tm, SSM_WIDTH), lambda i: (i, 0)),
            _resident((N_SLABS, SLAB_CH, SLAB_LANES)),
            _resident((N_SLABS, 2, SLAB_STATE)),
            _resident((N_SLABS, SLAB_LANES, SLAB_CH)),
            _resident((1, SSM_WIDTH)),
            _resident((SSM_WIDTH, SSM_WIDTH)),
            _resident((1, SSM_WIDTH)),
            _resident((1, SSM_WIDTH)),
        ],
        out_specs=pl.BlockSpec((tm, SSM_WIDTH), lambda i: (i, 0)),
        out_shape=jax.ShapeDtypeStruct((SEQ, SSM_WIDTH), _BF16),
        scratch_shapes=[
            pltpu.VMEM((tm, N_SLABS * SLAB_LANES), _F32),
            pltpu.VMEM((N_SLABS, SLAB_LANES), _F32),
            pltpu.VMEM((tm, SSM_WIDTH), _F32),
        ],
        compiler_params=pltpu.CompilerParams(dimension_semantics=("arbitrary",), vmem_limit_bytes=VMEM_LIMIT),
        name="s5_mixer",
    )(u, bmat, lam, cmat, d_skip, glu_w_bf, glu_b, norm_g)


def _s5_operands(lam_re, lam_im, log_dt, b_re, b_im, c_re, c_im):
    dt = jnp.exp(log_dt)[:, None]
    mag = jnp.exp(lam_re * dt)
    bar_re = mag * jnp.cos(lam_im * dt)
    bar_im = mag * jnp.sin(lam_im * dt)
    inv = 1.0 / (lam_re * lam_re + lam_im * lam_im)
    coef_re = ((bar_re - 1.0) * lam_re + bar_im * lam_im) * inv
    coef_im = (bar_im * lam_re - (bar_re - 1.0) * lam_im) * inv
    bbar_re = coef_re[..., None] * b_re - coef_im[..., None] * b_im
    bbar_im = coef_re[..., None] * b_im + coef_im[..., None] * b_re
    eye = jnp.eye(SLAB_GROUPS, dtype=_F32)

    def b_block(part):
        part = part.reshape(N_SLABS, SLAB_GROUPS, STATE, GROUP_CH)
        blk = jnp.einsum('kgph,gj->kghjp', part, eye)
        return blk.reshape(N_SLABS, SLAB_CH, SLAB_STATE)

    def c_block(part):
        part = part.reshape(N_SLABS, SLAB_GROUPS, GROUP_CH, STATE)
        blk = jnp.einsum('kghp,gj->kgpjh', part, eye)
        return blk.reshape(N_SLABS, SLAB_STATE, SLAB_CH)

    bmat = jnp.concatenate([b_block(bbar_re), b_block(bbar_im)], axis=2).astype(_BF16)
    cmat = jnp.concatenate([c_block(c_re), c_block(-c_im)], axis=1).astype(_BF16)
    lam_rows = jnp.stack([bar_re.reshape(N_SLABS, SLAB_STATE),
                          bar_im.reshape(N_SLABS, SLAB_STATE)], axis=1)
    return bmat, lam_rows, cmat


def _outproj_kernel(ys_ref, yc_ref, wt_ref, wb_ref, x_ref, g_ref, b_ref, h_ref):
    mix = jnp.dot(ys_ref[...], wt_ref[...], preferred_element_type=_F32)
    mix = mix + jnp.dot(yc_ref[...], wb_ref[...], preferred_element_type=_F32)
    h_ref[...] = _layernorm(ALPHA * x_ref[...] + mix, g_ref[...], b_ref[...])


def _outproj_call(ys, yc, w_out_bf, x, ln_g, ln_b):
    tm = TM_OUT
    half = lambda j: pl.BlockSpec((SSM_WIDTH, D_MODEL), lambda i, j=j: (j, 0), pipeline_mode=pl.Buffered(1))
    return pl.pallas_call(
        _outproj_kernel,
        grid=(SEQ // tm,),
        in_specs=[
            pl.BlockSpec((tm, SSM_WIDTH), lambda i: (i, 0)),
            pl.BlockSpec((tm, CONV_WIDTH), lambda i: (i, 0)),
            half(0), half(1),
            pl.BlockSpec((tm, D_MODEL), lambda i: (i, 0)),
            _resident((1, D_MODEL)),
            _resident((1, D_MODEL)),
        ],
        out_specs=pl.BlockSpec((tm, D_MODEL), lambda i: (i, 0)),
        out_shape=jax.ShapeDtypeStruct((SEQ, D_MODEL), _F32),
        compiler_params=pltpu.CompilerParams(dimension_semantics=("arbitrary",), vmem_limit_bytes=VMEM_LIMIT),
        name="outproj_ln1",
    )(ys, yc, w_out_bf, w_out_bf, x, ln_g, ln_b)


def _ffn_kernel(h_ref, wg_ref, wu_ref, wd_ref, cw_ref, cb_ref, g_ref, b_ref, o_ref,
                hb_ref, acc_ref, halo_ref, stage_ref):
    i = pl.program_id(0)
    k = pl.program_id(1)

    @pl.when(k == 0)
    def _():
        hb_ref[...] = h_ref[...].astype(_BF16)
        acc_ref[...] = jnp.zeros(acc_ref.shape, _F32)

    hb = hb_ref[...]
    gate = jnp.dot(hb, wg_ref[...], preferred_element_type=_F32)
    up = jnp.dot(hb, wu_ref[...], preferred_element_type=_F32)
    g1, g2 = _causal_taps(gate, halo_ref.at[k], stage_ref, i == 0)
    a = cw_ref[0:1, :] * g2 + cw_ref[1:2, :] * g1 + cw_ref[2:3, :] * gate + cb_ref[...]
    act = (jax.nn.silu(a) * up).astype(_BF16)
    acc_ref[...] += jnp.dot(act, wd_ref[...], preferred_element_type=_F32)

    @pl.when(k == pl.num_programs(1) - 1)
    def _():
        o_ref[...] = _layernorm(ALPHA * h_ref[...] + acc_ref[...], g_ref[...], b_ref[...])


def _ffn_call(h, wg_bf, wu_bf, wd_bf, conv_w, conv_b, ln_g, ln_b):
    tm, tf = TM_FFN, TF_FFN
    nk = D_FF // tf
    return pl.pallas_call(
        _ffn_kernel,
        grid=(SEQ // tm, nk),
        in_specs=[
            pl.BlockSpec((tm, D_MODEL), lambda i, k: (i, 0)),
            pl.BlockSpec((D_MODEL, tf), lambda i, k: (0, k)),
            pl.BlockSpec((D_MODEL, tf), lambda i, k: (0, k)),
            pl.BlockSpec((tf, D_MODEL), lambda i, k: (k, 0)),
            pl.BlockSpec((3, tf), lambda i, k: (0, k)),
            pl.BlockSpec((1, tf), lambda i, k: (0, k)),
            _resident((1, D_MODEL)),
            _resident((1, D_MODEL)),
        ],
        out_specs=pl.BlockSpec((tm, D_MODEL), lambda i, k: (i, 0)),
        out_shape=jax.ShapeDtypeStruct((SEQ, D_MODEL), _F32),
        scratch_shapes=[
            pltpu.VMEM((tm, D_MODEL), _BF16),
            pltpu.VMEM((tm, D_MODEL), _F32),
            pltpu.VMEM((nk, SUBLANES, tf), _F32),
            pltpu.VMEM((SUBLANES + tm, tf), _F32),
        ],
        compiler_params=pltpu.CompilerParams(dimension_semantics=("arbitrary", "arbitrary"),
                                             vmem_limit_bytes=VMEM_LIMIT),
        name="convffn_ln2",
    )(h, wg_bf, wu_bf, wd_bf, conv_w, conv_b, ln_g, ln_b)


def kernel(x, w_in, ssm_lambda_re, ssm_lambda_im, ssm_log_dt, ssm_b_re, ssm_b_im, ssm_c_re, ssm_c_im,
           ssm_d, ssm_glu_w, ssm_glu_b, sconv_w, norm_ssm_g, norm_conv_g, w_out, ln1_g, ln1_b,
           ffn_w_gate, ffn_w_up, ffn_conv_w, ffn_conv_b, ffn_w_down, ln2_g, ln2_b):
    assert x.shape == (1, SEQ, D_MODEL) and w_in.shape[0] == DEPTH
    h = x[0]
    for l in range(DEPTH):
        row = lambda p: p[l].reshape(1, -1)
        u, yc = _proj_call(h, w_in[l].astype(_BF16), sconv_w[l], row(norm_conv_g))
        bmat, lam_rows, cmat = _s5_operands(ssm_lambda_re[l], ssm_lambda_im[l], ssm_log_dt[l],
                                            ssm_b_re[l], ssm_b_im[l], ssm_c_re[l], ssm_c_im[l])
        ys = _ssm_call(u, bmat, lam_rows, cmat, row(ssm_d), ssm_glu_w[l].astype(_BF16),
                       row(ssm_glu_b), row(norm_ssm_g))
        h = _outproj_call(ys, yc, w_out[l].astype(_BF16), h, row(ln1_g), row(ln1_b))
        h = _ffn_call(h, ffn_w_gate[l].astype(_BF16), ffn_w_up[l].astype(_BF16), ffn_w_down[l].astype(_BF16),
                      ffn_conv_w[l], row(ffn_conv_b), row(ln2_g), row(ln2_b))
    return h[None]
```

```python
import functools
import math

import jax
import jax.numpy as jnp
from jax import lax
from jax.experimental import pallas as pl
from jax.experimental.pallas import tpu as pltpu

SEQ = 8192
D_MODEL = 2048
SSM_WIDTH = 1024
CONV_WIDTH = 1024
GROUP_CH = 16
GROUPS = SSM_WIDTH // GROUP_CH
STATE = 64
D_FF = 5632
DEPTH = 1
LN_EPS = 1e-5
RMS_EPS = 1e-6
ALPHA = (2.0 * DEPTH) ** 0.25

SUBLANES = 8
SLAB_GROUPS = 8
N_SLABS = GROUPS // SLAB_GROUPS
SLAB_CH = SLAB_GROUPS * GROUP_CH
SLAB_STATE = SLAB_GROUPS * STATE
SLAB_LANES = 2 * SLAB_STATE

TM_PROJ = 512
TM_SSM = 256
TM_OUT = 512
TM_FFN = 512
TF_FFN = 512
FFN_ROW_SPLIT = 2
VMEM_LIMIT = 56 * 1024 * 1024

_F32 = jnp.float32
_BF16 = jnp.bfloat16


def _resident(shape):
    return pl.BlockSpec(shape, lambda *_: (0,) * len(shape), pipeline_mode=pl.Buffered(1))


def _layernorm(r, g, b):
    mu = jnp.mean(r, axis=-1, keepdims=True)
    rc = r - mu
    var = jnp.mean(rc * rc, axis=-1, keepdims=True)
    return rc * lax.rsqrt(var + LN_EPS) * g + b


def _rmsnorm(y, g):
    return y * lax.rsqrt(jnp.mean(y * y, axis=-1, keepdims=True) + RMS_EPS) * g


def _causal_taps(cur, halo_ref, stage_ref, first_tile):
    tm = cur.shape[0]

    @pl.when(first_tile)
    def _():
        stage_ref[0:SUBLANES, :] = jnp.zeros((SUBLANES, cur.shape[1]), _F32)

    @pl.when(jnp.logical_not(first_tile))
    def _():
        stage_ref[0:SUBLANES, :] = halo_ref[...]

    stage_ref[SUBLANES:SUBLANES + tm, :] = cur
    halo_ref[...] = cur[tm - SUBLANES:tm, :]
    return stage_ref[SUBLANES - 1:SUBLANES - 1 + tm, :], stage_ref[SUBLANES - 2:SUBLANES - 2 + tm, :]


def _proj_kernel(x_ref, wu_ref, wb_ref, wc_ref, wv_ref, cw_ref, g_ref, u_ref, yc_ref, halo_ref, stage_ref):
    xb = x_ref[...].astype(_BF16)
    u_ref[...] = jnp.dot(xb, wu_ref[...], preferred_element_type=_F32)
    gate_c = jnp.dot(xb, wc_ref[...], preferred_element_type=_F32)
    v = jnp.dot(xb, wv_ref[...], preferred_element_type=_F32)
    cv = gate_c * v
    cv1, cv2 = _causal_taps(cv, halo_ref, stage_ref, pl.program_id(0) == 0)
    conv = cw_ref[0:1, :] * cv2 + cw_ref[1:2, :] * cv1 + cw_ref[2:3, :] * cv
    gate_b = jnp.dot(xb, wb_ref[...], preferred_element_type=_F32)
    yc_ref[...] = _rmsnorm(gate_b * conv, g_ref[...]).astype(_BF16)


def _proj_call(x, w_in_bf, sconv_w, norm_conv_g):
    tm = TM_PROJ
    wspec = lambda j: pl.BlockSpec((D_MODEL, 1024), lambda i, j=j: (0, j), pipeline_mode=pl.Buffered(1))
    return pl.pallas_call(
        _proj_kernel,
        grid=(SEQ // tm,),
        in_specs=[
            pl.BlockSpec((tm, D_MODEL), lambda i: (i, 0)),
            wspec(0), wspec(1), wspec(2), wspec(3),
            _resident((3, CONV_WIDTH)),
            _resident((1, CONV_WIDTH)),
        ],
        out_specs=[
            pl.BlockSpec((tm, SSM_WIDTH), lambda i: (i, 0)),
            pl.BlockSpec((tm, CONV_WIDTH), lambda i: (i, 0)),
        ],
        out_shape=[
            jax.ShapeDtypeStruct((SEQ, SSM_WIDTH), _F32),
            jax.ShapeDtypeStruct((SEQ, CONV_WIDTH), _BF16),
        ],
        scratch_shapes=[
            pltpu.VMEM((SUBLANES, CONV_WIDTH), _F32),
            pltpu.VMEM((SUBLANES + tm, CONV_WIDTH), _F32),
        ],
        compiler_params=pltpu.CompilerParams(dimension_semantics=("arbitrary",), vmem_limit_bytes=VMEM_LIMIT),
        name="proj_shortconv",
    )(x, w_in_bf, w_in_bf, w_in_bf, w_in_bf, sconv_w, norm_conv_g)


def _ssm_kernel(u_ref, bmat_ref, lam_ref, cmat_ref, d_ref, gw_ref, gb_ref, g_ref, ys_ref,
                bu_ref, state_ref, y_ref):
    tm = u_ref.shape[0]

    @pl.when(pl.program_id(0) == 0)
    def _():
        state_ref[...] = jnp.zeros(state_ref.shape, _F32)

    u = u_ref[...]
    ub = u.astype(_BF16)
    for k in range(N_SLABS):
        bu_ref[:, k * SLAB_LANES:(k + 1) * SLAB_LANES] = jnp.dot(
            ub[:, k * SLAB_CH:(k + 1) * SLAB_CH], bmat_ref[k], preferred_element_type=_F32)

    for k in range(N_SLABS):
        re = slice(k * SLAB_LANES, k * SLAB_LANES + SLAB_STATE)
        im = slice(k * SLAB_LANES + SLAB_STATE, (k + 1) * SLAB_LANES)
        lr = lam_ref[k, 0:1, :]
        li = lam_ref[k, 1:2, :]

        def step(t, carry, re=re, im=im, lr=lr, li=li):
            xr, xi = carry
            nxr = lr * xr - li * xi + bu_ref[pl.ds(t, 1), re]
            nxi = lr * xi + li * xr + bu_ref[pl.ds(t, 1), im]
            bu_ref[pl.ds(t, 1), re] = nxr
            bu_ref[pl.ds(t, 1), im] = nxi
            return nxr, nxi

        xr, xi = lax.fori_loop(0, tm, step,
                               (state_ref[k:k + 1, 0:SLAB_STATE], state_ref[k:k + 1, SLAB_STATE:SLAB_LANES]),
                               unroll=8)
        state_ref[k:k + 1, 0:SLAB_STATE] = xr
        state_ref[k:k + 1, SLAB_STATE:SLAB_LANES] = xi

    for k in range(N_SLABS):
        xs = bu_ref[:, k * SLAB_LANES:(k + 1) * SLAB_LANES].astype(_BF16)
        y_ref[:, k * SLAB_CH:(k + 1) * SLAB_CH] = jnp.dot(xs, cmat_ref[k], preferred_element_type=_F32)
    y = y_ref[...] + d_ref[...] * u
    g = jax.nn.gelu(y)
    z = jnp.dot(g.astype(_BF16), gw_ref[...], preferred_element_type=_F32) + gb_ref[...]
    ys_ref[...] = _rmsnorm(g * jax.nn.sigmoid(z), g_ref[...]).astype(_BF16)


def _ssm_call(u, bmat, lam, cmat, d_skip, glu_w_bf, glu_b, norm_g):
    tm = TM_SSM
    return pl.pallas_call(
        _ssm_kernel,
        grid=(SEQ // tm,),
        in_specs=[
            pl.BlockSpec((tm, SSM_WIDTH), lambda i: (i, 0)),
            _resident((N_SLABS, SLAB_CH, SLAB_LANES)),
            _resident((N_SLABS, 2, SLAB_STATE)),
            _resident((N_SLABS, SLAB_LANES, SLAB_CH)),
            _resident((1, SSM_WIDTH)),
            _resident((SSM_WIDTH, SSM_WIDTH)),
            _resident((1, SSM_WIDTH)),
            _resident((1, SSM_WIDTH)),
        ],
        out_specs=pl.BlockSpec((tm, SSM_WIDTH), lambda i: (i, 0)),
        out_shape=jax.ShapeDtypeStruct((SEQ, SSM_WIDTH), _BF16),
        scratch_shapes=[
            pltpu.VMEM((tm, N_SLABS * SLAB_LANES), _F32),
            pltpu.VMEM((N_SLABS, SLAB_LANES), _F32),
            pltpu.VMEM((tm, SSM_WIDTH), _F32),
        ],
        compiler_params=pltpu.CompilerParams(dimension_semantics=("arbitrary",), vmem_limit_bytes=VMEM_LIMIT),
        name="s5_mixer",
    )(u, bmat, lam, cmat, d_skip, glu_w_bf, glu_b, norm_g)


def _s5_operands(lam_re, lam_im, log_dt, b_re, b_im, c_re, c_im):
    dt = jnp.exp(log_dt)[:, None]
    mag = jnp.exp(lam_re * dt)
    bar_re = mag * jnp.cos(lam_im * dt)
    bar_im = mag * jnp.sin(lam_im * dt)
    inv = 1.0 / (lam_re * lam_re + lam_im * lam_im)
    coef_re = ((bar_re - 1.0) * lam_re + bar_im * lam_im) * inv
    coef_im = (bar_im * lam_re - (bar_re - 1.0) * lam_im) * inv
    bbar_re = coef_re[..., None] * b_re - coef_im[..., None] * b_im
    bbar_im = coef_re[..., None] * b_im + coef_im[..., None] * b_re
    eye = jnp.eye(SLAB_GROUPS, dtype=_F32)

    def b_block(part):
        part = part.reshape(N_SLABS, SLAB_GROUPS, STATE, GROUP_CH)
        blk = jnp.einsum('kgph,gj->kghjp', part, eye)
        return blk.reshape(N_SLABS, SLAB_CH, SLAB_STATE)

    def c_block(part):
        part = part.reshape(N_SLABS, SLAB_GROUPS, GROUP_CH, STATE)
        blk = jnp.einsum('kghp,gj->kgpjh', part, eye)
        return blk.reshape(N_SLABS, SLAB_STATE, SLAB_CH)

    bmat = jnp.concatenate([b_block(bbar_re), b_block(bbar_im)], axis=2).astype(_BF16)
    cmat = jnp.concatenate([c_block(c_re), c_block(-c_im)], axis=1).astype(_BF16)
    lam_rows = jnp.stack([bar_re.reshape(N_SLABS, SLAB_STATE),
                          bar_im.reshape(N_SLABS, SLAB_STATE)], axis=1)
    return bmat, lam_rows, cmat


def _outproj_kernel(ys_ref, yc_ref, wt_ref, wb_ref, x_ref, g_ref, b_ref, h_ref):
    mix = jnp.dot(ys_ref[...], wt_ref[...], preferred_element_type=_F32)
    mix = mix + jnp.dot(yc_ref[...], wb_ref[...], preferred_element_type=_F32)
    h_ref[...] = _layernorm(ALPHA * x_ref[...] + mix, g_ref[...], b_ref[...])


def _outproj_call(ys, yc, w_out_bf, x, ln_g, ln_b):
    tm = TM_OUT
    half = lambda j: pl.BlockSpec((SSM_WIDTH, D_MODEL), lambda i, j=j: (j, 0), pipeline_mode=pl.Buffered(1))
    return pl.pallas_call(
        _outproj_kernel,
        grid=(SEQ // tm,),
        in_specs=[
            pl.BlockSpec((tm, SSM_WIDTH), lambda i: (i, 0)),
            pl.BlockSpec((tm, CONV_WIDTH), lambda i: (i, 0)),
            half(0), half(1),
            pl.BlockSpec((tm, D_MODEL), lambda i: (i, 0)),
            _resident((1, D_MODEL)),
            _resident((1, D_MODEL)),
        ],
        out_specs=pl.BlockSpec((tm, D_MODEL), lambda i: (i, 0)),
        out_shape=jax.ShapeDtypeStruct((SEQ, D_MODEL), _F32),
        compiler_params=pltpu.CompilerParams(dimension_semantics=("arbitrary",), vmem_limit_bytes=VMEM_LIMIT),
        name="outproj_ln1",
    )(ys, yc, w_out_bf, w_out_bf, x, ln_g, ln_b)


def _ffn_kernel(h_ref, wg_ref, wu_ref, wd_ref, cw_ref, cb_ref, g_ref, b_ref, o_ref,
                hb_ref, acc_ref, halo_ref, stage_ref):
    i = pl.program_id(0)
    k = pl.program_id(1)

    @pl.when(k == 0)
    def _():
        hb_ref[...] = h_ref[...].astype(_BF16)
        acc_ref[...] = jnp.zeros(acc_ref.shape, _F32)

    tm = hb_ref.shape[0]
    hm = tm // FFN_ROW_SPLIT

    @pl.when(i == 0)
    def _():
        stage_ref[0:SUBLANES, :] = jnp.zeros((SUBLANES, stage_ref.shape[1]), _F32)

    @pl.when(i > 0)
    def _():
        stage_ref[0:SUBLANES, :] = halo_ref[k]

    for r in range(FFN_ROW_SPLIT):
        rows = slice(r * hm, (r + 1) * hm)
        hb = hb_ref[rows, :]
        gate = jnp.dot(hb, wg_ref[...], preferred_element_type=_F32)
        up = jnp.dot(hb, wu_ref[...], preferred_element_type=_F32)
        stage_ref[SUBLANES + r * hm:SUBLANES + (r + 1) * hm, :] = gate
        g1 = stage_ref[SUBLANES - 1 + r * hm:SUBLANES - 1 + (r + 1) * hm, :]
        g2 = stage_ref[SUBLANES - 2 + r * hm:SUBLANES - 2 + (r + 1) * hm, :]
        a = cw_ref[0:1, :] * g2 + cw_ref[1:2, :] * g1 + cw_ref[2:3, :] * gate + cb_ref[...]
        act = (jax.nn.silu(a) * up).astype(_BF16)
        acc_ref[rows, :] += jnp.dot(act, wd_ref[...], preferred_element_type=_F32)
    halo_ref[k] = stage_ref[tm:tm + SUBLANES, :]

    @pl.when(k == pl.num_programs(1) - 1)
    def _():
        o_ref[...] = _layernorm(ALPHA * h_ref[...] + acc_ref[...], g_ref[...], b_ref[...])


def _ffn_call(h, wg_bf, wu_bf, wd_bf, conv_w, conv_b, ln_g, ln_b):
    tm, tf = TM_FFN, TF_FFN
    nk = D_FF // tf
    return pl.pallas_call(
        _ffn_kernel,
        grid=(SEQ // tm, nk),
        in_specs=[
            pl.BlockSpec((tm, D_MODEL), lambda i, k: (i, 0)),
            pl.BlockSpec((D_MODEL, tf), lambda i, k: (0, k)),
            pl.BlockSpec((D_MODEL, tf), lambda i, k: (0, k)),
            pl.BlockSpec((tf, D_MODEL), lambda i, k: (k, 0)),
            pl.BlockSpec((3, tf), lambda i, k: (0, k)),
            pl.BlockSpec((1, tf), lambda i, k: (0, k)),
            _resident((1, D_MODEL)),
            _resident((1, D_MODEL)),
        ],
        out_specs=pl.BlockSpec((tm, D_MODEL), lambda i, k: (i, 0)),
        out_shape=jax.ShapeDtypeStruct((SEQ, D_MODEL), _F32),
        scratch_shapes=[
            pltpu.VMEM((tm, D_MODEL), _BF16),
            pltpu.VMEM((tm, D_MODEL), _F32),
            pltpu.VMEM((nk, SUBLANES, tf), _F32),
            pltpu.VMEM((SUBLANES + tm, tf), _F32),
        ],
        compiler_params=pltpu.CompilerParams(dimension_semantics=("arbitrary", "arbitrary"),
                                             vmem_limit_bytes=VMEM_LIMIT),
        name="convffn_ln2",
    )(h, wg_bf, wu_bf, wd_bf, conv_w, conv_b, ln_g, ln_b)


def kernel(x, w_in, ssm_lambda_re, ssm_lambda_im, ssm_log_dt, ssm_b_re, ssm_b_im, ssm_c_re, ssm_c_im,
           ssm_d, ssm_glu_w, ssm_glu_b, sconv_w, norm_ssm_g, norm_conv_g, w_out, ln1_g, ln1_b,
           ffn_w_gate, ffn_w_up, ffn_conv_w, ffn_conv_b, ffn_w_down, ln2_g, ln2_b):
    assert x.shape == (1, SEQ, D_MODEL) and w_in.shape[0] == DEPTH
    h = x[0]
    for l in range(DEPTH):
        row = lambda p: p[l].reshape(1, -1)
        u, yc = _proj_call(h, w_in[l].astype(_BF16), sconv_w[l], row(norm_conv_g))
        bmat, lam_rows, cmat = _s5_operands(ssm_lambda_re[l], ssm_lambda_im[l], ssm_log_dt[l],
                                            ssm_b_re[l], ssm_b_im[l], ssm_c_re[l], ssm_c_im[l])
        ys = _ssm_call(u, bmat, lam_rows, cmat, row(ssm_d), ssm_glu_w[l].astype(_BF16),
                       row(ssm_glu_b), row(norm_ssm_g))
        h = _outproj_call(ys, yc, w_out[l].astype(_BF16), h, row(ln1_g), row(ln1_b))
        h = _ffn_call(h, ffn_w_gate[l].astype(_BF16), ffn_w_up[l].astype(_BF16), ffn_w_down[l].astype(_BF16),
                      ffn_conv_w[l], row(ffn_conv_b), row(ln2_g), row(ln2_b))
    return h[None]
```

```python
import functools
import math

import jax
import jax.numpy as jnp
from jax import lax
from jax.experimental import pallas as pl
from jax.experimental.pallas import tpu as pltpu

SEQ = 8192
D_MODEL = 2048
SSM_WIDTH = 1024
CONV_WIDTH = 1024
GROUP_CH = 16
GROUPS = SSM_WIDTH // GROUP_CH
STATE = 64
D_FF = 5632
DEPTH = 1
LN_EPS = 1e-5
RMS_EPS = 1e-6
ALPHA = (2.0 * DEPTH) ** 0.25

SUBLANES = 8
SLAB_GROUPS = 8
N_SLABS = GROUPS // SLAB_GROUPS
SLAB_CH = SLAB_GROUPS * GROUP_CH
SLAB_STATE = SLAB_GROUPS * STATE
SLAB_LANES = 2 * SLAB_STATE

TM_PROJ = 512
TM_SSM = 256
TM_OUT = 512
TM_FFN = 1024
TF_FFN = 512
FFN_ROW_SPLIT = 4
VMEM_LIMIT = 56 * 1024 * 1024

_F32 = jnp.float32
_BF16 = jnp.bfloat16


def _resident(shape):
    return pl.BlockSpec(shape, lambda *_: (0,) * len(shape), pipeline_mode=pl.Buffered(1))


def _layernorm(r, g, b):
    mu = jnp.mean(r, axis=-1, keepdims=True)
    rc = r - mu
    var = jnp.mean(rc * rc, axis=-1, keepdims=True)
    return rc * lax.rsqrt(var + LN_EPS) * g + b


def _rmsnorm(y, g):
    return y * lax.rsqrt(jnp.mean(y * y, axis=-1, keepdims=True) + RMS_EPS) * g


def _causal_taps(cur, halo_ref, stage_ref, first_tile):
    tm = cur.shape[0]

    @pl.when(first_tile)
    def _():
        stage_ref[0:SUBLANES, :] = jnp.zeros((SUBLANES, cur.shape[1]), _F32)

    @pl.when(jnp.logical_not(first_tile))
    def _():
        stage_ref[0:SUBLANES, :] = halo_ref[...]

    stage_ref[SUBLANES:SUBLANES + tm, :] = cur
    halo_ref[...] = cur[tm - SUBLANES:tm, :]
    return stage_ref[SUBLANES - 1:SUBLANES - 1 + tm, :], stage_ref[SUBLANES - 2:SUBLANES - 2 + tm, :]


def _proj_kernel(x_ref, wu_ref, wb_ref, wc_ref, wv_ref, cw_ref, g_ref, u_ref, yc_ref, halo_ref, stage_ref):
    xb = x_ref[...].astype(_BF16)
    u_ref[...] = jnp.dot(xb, wu_ref[...], preferred_element_type=_F32)
    gate_c = jnp.dot(xb, wc_ref[...], preferred_element_type=_F32)
    v = jnp.dot(xb, wv_ref[...], preferred_element_type=_F32)
    cv = gate_c * v
    cv1, cv2 = _causal_taps(cv, halo_ref, stage_ref, pl.program_id(0) == 0)
    conv = cw_ref[0:1, :] * cv2 + cw_ref[1:2, :] * cv1 + cw_ref[2:3, :] * cv
    gate_b = jnp.dot(xb, wb_ref[...], preferred_element_type=_F32)
    yc_ref[...] = _rmsnorm(gate_b * conv, g_ref[...]).astype(_BF16)


def _proj_call(x, w_in_bf, sconv_w, norm_conv_g):
    tm = TM_PROJ
    wspec = lambda j: pl.BlockSpec((D_MODEL, 1024), lambda i, j=j: (0, j), pipeline_mode=pl.Buffered(1))
    return pl.pallas_call(
        _proj_kernel,
        grid=(SEQ // tm,),
        in_specs=[
            pl.BlockSpec((tm, D_MODEL), lambda i: (i, 0)),
            wspec(0), wspec(1), wspec(2), wspec(3),
            _resident((3, CONV_WIDTH)),
            _resident((1, CONV_WIDTH)),
        ],
        out_specs=[
            pl.BlockSpec((tm, SSM_WIDTH), lambda i: (i, 0)),
            pl.BlockSpec((tm, CONV_WIDTH), lambda i: (i, 0)),
        ],
        out_shape=[
            jax.ShapeDtypeStruct((SEQ, SSM_WIDTH), _F32),
            jax.ShapeDtypeStruct((SEQ, CONV_WIDTH), _BF16),
        ],
        scratch_shapes=[
            pltpu.VMEM((SUBLANES, CONV_WIDTH), _F32),
            pltpu.VMEM((SUBLANES + tm, CONV_WIDTH), _F32),
        ],
        compiler_params=pltpu.CompilerParams(dimension_semantics=("arbitrary",), vmem_limit_bytes=VMEM_LIMIT),
        name="proj_shortconv",
    )(x, w_in_bf, w_in_bf, w_in_bf, w_in_bf, sconv_w, norm_conv_g)


def _ssm_kernel(u_ref, bmat_ref, lam_ref, cmat_ref, d_ref, gw_ref, gb_ref, g_ref, ys_ref,
                bu_ref, state_ref, y_ref):
    tm = u_ref.shape[0]

    @pl.when(pl.program_id(0) == 0)
    def _():
        state_ref[...] = jnp.zeros(state_ref.shape, _F32)

    u = u_ref[...]
    ub = u.astype(_BF16)
    for k in range(N_SLABS):
        bu_ref[:, k * SLAB_LANES:(k + 1) * SLAB_LANES] = jnp.dot(
            ub[:, k * SLAB_CH:(k + 1) * SLAB_CH], bmat_ref[k], preferred_element_type=_F32)

    for k in range(N_SLABS):
        re = slice(k * SLAB_LANES, k * SLAB_LANES + SLAB_STATE)
        im = slice(k * SLAB_LANES + SLAB_STATE, (k + 1) * SLAB_LANES)
        lr = lam_ref[k, 0:1, :]
        li = lam_ref[k, 1:2, :]

        def step(t, carry, re=re, im=im, lr=lr, li=li):
            xr, xi = carry
            nxr = lr * xr - li * xi + bu_ref[pl.ds(t, 1), re]
            nxi = lr * xi + li * xr + bu_ref[pl.ds(t, 1), im]
            bu_ref[pl.ds(t, 1), re] = nxr
            bu_ref[pl.ds(t, 1), im] = nxi
            return nxr, nxi

        xr, xi = lax.fori_loop(0, tm, step,
                               (state_ref[k:k + 1, 0:SLAB_STATE], state_ref[k:k + 1, SLAB_STATE:SLAB_LANES]),
                               unroll=8)
        state_ref[k:k + 1, 0:SLAB_STATE] = xr
        state_ref[k:k + 1, SLAB_STATE:SLAB_LANES] = xi

    for k in range(N_SLABS):
        xs = bu_ref[:, k * SLAB_LANES:(k + 1) * SLAB_LANES].astype(_BF16)
        y_ref[:, k * SLAB_CH:(k + 1) * SLAB_CH] = jnp.dot(xs, cmat_ref[k], preferred_element_type=_F32)
    y = y_ref[...] + d_ref[...] * u
    g = jax.nn.gelu(y)
    z = jnp.dot(g.astype(_BF16), gw_ref[...], preferred_element_type=_F32) + gb_ref[...]
    ys_ref[...] = _rmsnorm(g * jax.nn.sigmoid(z), g_ref[...]).astype(_BF16)


def _ssm_call(u, bmat, lam, cmat, d_skip, glu_w_bf, glu_b, norm_g):
    tm = TM_SSM
    return pl.pallas_call(
        _ssm_kernel,
        grid=(SEQ // tm,),
        in_specs=[
            pl.BlockSpec((tm, SSM_WIDTH), lambda i: (i, 0)),
            _resident((N_SLABS, SLAB_CH, SLAB_LANES)),
            _resident((N_SLABS, 2, SLAB_STATE)),
            _resident((N_SLABS, SLAB_LANES, SLAB_CH)),
            _resident((1, SSM_WIDTH)),
            _resident((SSM_WIDTH, SSM_WIDTH)),
            _resident((1, SSM_WIDTH)),
            _resident((1, SSM_WIDTH)),
        ],
        out_specs=pl.BlockSpec((tm, SSM_WIDTH), lambda i: (i, 0)),
        out_shape=jax.ShapeDtypeStruct((SEQ, SSM_WIDTH), _BF16),
        scratch_shapes=[
            pltpu.VMEM((tm, N_SLABS * SLAB_LANES), _F32),
            pltpu.VMEM((N_SLABS, SLAB_LANES), _F32),
            pltpu.VMEM((tm, SSM_WIDTH), _F32),
        ],
        compiler_params=pltpu.CompilerParams(dimension_semantics=("arbitrary",), vmem_limit_bytes=VMEM_LIMIT),
        name="s5_mixer",
    )(u, bmat, lam, cmat, d_skip, glu_w_bf, glu_b, norm_g)


def _s5_operands(lam_re, lam_im, log_dt, b_re, b_im, c_re, c_im):
    dt = jnp.exp(log_dt)[:, None]
    mag = jnp.exp(lam_re * dt)
    bar_re = mag * jnp.cos(lam_im * dt)
    bar_im = mag * jnp.sin(lam_im * dt)
    inv = 1.0 / (lam_re * lam_re + lam_im * lam_im)
    coef_re = ((bar_re - 1.0) * lam_re + bar_im * lam_im) * inv
    coef_im = (bar_im * lam_re - (bar_re - 1.0) * lam_im) * inv
    bbar_re = coef_re[..., None] * b_re - coef_im[..., None] * b_im
    bbar_im = coef_re[..., None] * b_im + coef_im[..., None] * b_re
    eye = jnp.eye(SLAB_GROUPS, dtype=_F32)

    def b_block(part):
        part = part.reshape(N_SLABS, SLAB_GROUPS, STATE, GROUP_CH)
        blk = jnp.einsum('kgph,gj->kghjp', part, eye)
        return blk.reshape(N_SLABS, SLAB_CH, SLAB_STATE)

    def c_block(part):
        part = part.reshape(N_SLABS, SLAB_GROUPS, GROUP_CH, STATE)
        blk = jnp.einsum('kghp,gj->kgpjh', part, eye)
        return blk.reshape(N_SLABS, SLAB_STATE, SLAB_CH)

    bmat = jnp.concatenate([b_block(bbar_re), b_block(bbar_im)], axis=2).astype(_BF16)
    cmat = jnp.concatenate([c_block(c_re), c_block(-c_im)], axis=1).astype(_BF16)
    lam_rows = jnp.stack([bar_re.reshape(N_SLABS, SLAB_STATE),
                          bar_im.reshape(N_SLABS, SLAB_STATE)], axis=1)
    return bmat, lam_rows, cmat


def _outproj_kernel(ys_ref, yc_ref, wt_ref, wb_ref, x_ref, g_ref, b_ref, h_ref):
    mix = jnp.dot(ys_ref[...], wt_ref[...], preferred_element_type=_F32)
    mix = mix + jnp.dot(yc_ref[...], wb_ref[...], preferred_element_type=_F32)
    h_ref[...] = _layernorm(ALPHA * x_ref[...] + mix, g_ref[...], b_ref[...])


def _outproj_call(ys, yc, w_out_bf, x, ln_g, ln_b):
    tm = TM_OUT
    half = lambda j: pl.BlockSpec((SSM_WIDTH, D_MODEL), lambda i, j=j: (j, 0), pipeline_mode=pl.Buffered(1))
    return pl.pallas_call(
        _outproj_kernel,
        grid=(SEQ // tm,),
        in_specs=[
            pl.BlockSpec((tm, SSM_WIDTH), lambda i: (i, 0)),
            pl.BlockSpec((tm, CONV_WIDTH), lambda i: (i, 0)),
            half(0), half(1),
            pl.BlockSpec((tm, D_MODEL), lambda i: (i, 0)),
            _resident((1, D_MODEL)),
            _resident((1, D_MODEL)),
        ],
        out_specs=pl.BlockSpec((tm, D_MODEL), lambda i: (i, 0)),
        out_shape=jax.ShapeDtypeStruct((SEQ, D_MODEL), _F32),
        compiler_params=pltpu.CompilerParams(dimension_semantics=("arbitrary",), vmem_limit_bytes=VMEM_LIMIT),
        name="outproj_ln1",
    )(ys, yc, w_out_bf, w_out_bf, x, ln_g, ln_b)


def _ffn_kernel(h_ref, wg_ref, wu_ref, wd_ref, cw_ref, cb_ref, g_ref, b_ref, o_ref,
                hb_ref, halo_ref, stage_ref):
    i = pl.program_id(0)
    k = pl.program_id(1)

    @pl.when(k == 0)
    def _():
        hb_ref[...] = h_ref[...].astype(_BF16)
        o_ref[...] = jnp.zeros(o_ref.shape, _F32)

    tm = hb_ref.shape[0]
    hm = tm // FFN_ROW_SPLIT

    @pl.when(i == 0)
    def _():
        stage_ref[0:SUBLANES, :] = jnp.zeros((SUBLANES, stage_ref.shape[1]), _F32)

    @pl.when(i > 0)
    def _():
        stage_ref[0:SUBLANES, :] = halo_ref[k]

    for r in range(FFN_ROW_SPLIT):
        rows = slice(r * hm, (r + 1) * hm)
        hb = hb_ref[rows, :]
        gate = jnp.dot(hb, wg_ref[...], preferred_element_type=_F32)
        up = jnp.dot(hb, wu_ref[...], preferred_element_type=_F32)
        stage_ref[SUBLANES + r * hm:SUBLANES + (r + 1) * hm, :] = gate
        g1 = stage_ref[SUBLANES - 1 + r * hm:SUBLANES - 1 + (r + 1) * hm, :]
        g2 = stage_ref[SUBLANES - 2 + r * hm:SUBLANES - 2 + (r + 1) * hm, :]
        a = cw_ref[0:1, :] * g2 + cw_ref[1:2, :] * g1 + cw_ref[2:3, :] * gate + cb_ref[...]
        act = (jax.nn.silu(a) * up).astype(_BF16)
        o_ref[rows, :] += jnp.dot(act, wd_ref[...], preferred_element_type=_F32)
    halo_ref[k] = stage_ref[tm:tm + SUBLANES, :]

    @pl.when(k == pl.num_programs(1) - 1)
    def _():
        o_ref[...] = _layernorm(ALPHA * h_ref[...] + o_ref[...], g_ref[...], b_ref[...])


def _ffn_call(h, wg_bf, wu_bf, wd_bf, conv_w, conv_b, ln_g, ln_b):
    tm, tf = TM_FFN, TF_FFN
    nk = D_FF // tf
    return pl.pallas_call(
        _ffn_kernel,
        grid=(SEQ // tm, nk),
        in_specs=[
            pl.BlockSpec((tm, D_MODEL), lambda i, k: (i, 0), pipeline_mode=pl.Buffered(1)),
            pl.BlockSpec((D_MODEL, tf), lambda i, k: (0, k)),
            pl.BlockSpec((D_MODEL, tf), lambda i, k: (0, k)),
            pl.BlockSpec((tf, D_MODEL), lambda i, k: (k, 0)),
            pl.BlockSpec((3, tf), lambda i, k: (0, k)),
            pl.BlockSpec((1, tf), lambda i, k: (0, k)),
            _resident((1, D_MODEL)),
            _resident((1, D_MODEL)),
        ],
        out_specs=pl.BlockSpec((tm, D_MODEL), lambda i, k: (i, 0)),
        out_shape=jax.ShapeDtypeStruct((SEQ, D_MODEL), _F32),
        scratch_shapes=[
            pltpu.VMEM((tm, D_MODEL), _BF16),
            pltpu.VMEM((nk, SUBLANES, tf), _F32),
            pltpu.VMEM((SUBLANES + tm, tf), _F32),
        ],
        compiler_params=pltpu.CompilerParams(dimension_semantics=("arbitrary", "arbitrary"),
                                             vmem_limit_bytes=VMEM_LIMIT),
        name="convffn_ln2",
    )(h, wg_bf, wu_bf, wd_bf, conv_w, conv_b, ln_g, ln_b)


def kernel(x, w_in, ssm_lambda_re, ssm_lambda_im, ssm_log_dt, ssm_b_re, ssm_b_im, ssm_c_re, ssm_c_im,
           ssm_d, ssm_glu_w, ssm_glu_b, sconv_w, norm_ssm_g, norm_conv_g, w_out, ln1_g, ln1_b,
           ffn_w_gate, ffn_w_up, ffn_conv_w, ffn_conv_b, ffn_w_down, ln2_g, ln2_b):
    assert x.shape == (1, SEQ, D_MODEL) and w_in.shape[0] == DEPTH
    h = x[0]
    for l in range(DEPTH):
        row = lambda p: p[l].reshape(1, -1)
        u, yc = _proj_call(h, w_in[l].astype(_BF16), sconv_w[l], row(norm_conv_g))
        bmat, lam_rows, cmat = _s5_operands(ssm_lambda_re[l], ssm_lambda_im[l], ssm_log_dt[l],
                                            ssm_b_re[l], ssm_b_im[l], ssm_c_re[l], ssm_c_im[l])
        ys = _ssm_call(u, bmat, lam_rows, cmat, row(ssm_d), ssm_glu_w[l].astype(_BF16),
                       row(ssm_glu_b), row(norm_ssm_g))
        h = _outproj_call(ys, yc, w_out[l].astype(_BF16), h, row(ln1_g), row(ln1_b))
        h = _ffn_call(h, ffn_w_gate[l].astype(_BF16), ffn_w_up[l].astype(_BF16), ffn_w_down[l].astype(_BF16),
                      ffn_conv_w[l], row(ffn_conv_b), row(ln2_g), row(ln2_b))
    return h[None]
```

```python
import jax
import jax.numpy as jnp
from jax import lax
from jax.experimental import pallas as pl
from jax.experimental.pallas import tpu as pltpu

SEQ = 8192
D_MODEL = 2048
SSM_WIDTH = 1024
CONV_WIDTH = 1024
GROUP_CH = 16
GROUPS = SSM_WIDTH // GROUP_CH
STATE = 64
D_FF = 5632
DEPTH = 1
LN_EPS = 1e-5
RMS_EPS = 1e-6
ALPHA = (2.0 * DEPTH) ** 0.25

SUBLANES = 8
SLAB_GROUPS = 8
N_SLABS = GROUPS // SLAB_GROUPS
SLAB_CH = SLAB_GROUPS * GROUP_CH
SLAB_STATE = SLAB_GROUPS * STATE
SLAB_LANES = 2 * SLAB_STATE

TM_PROJ = 512
TM_SSM = 512
SSM_CHUNKS = SUBLANES
SSM_STEPS = TM_SSM // SSM_CHUNKS
TM_OUT = 512
TM_FFN = 512
TF_FFN = 512
FFN_ROW_SPLIT = 2
VMEM_LIMIT = 56 * 1024 * 1024

_F32 = jnp.float32
_BF16 = jnp.bfloat16


def _resident(shape):
    return pl.BlockSpec(shape, lambda *_: (0,) * len(shape), pipeline_mode=pl.Buffered(1))


def _layernorm(r, g, b):
    mu = jnp.mean(r, axis=-1, keepdims=True)
    rc = r - mu
    var = jnp.mean(rc * rc, axis=-1, keepdims=True)
    return rc * lax.rsqrt(var + LN_EPS) * g + b


def _rmsnorm(y, g):
    return y * lax.rsqrt(jnp.mean(y * y, axis=-1, keepdims=True) + RMS_EPS) * g


def _causal_taps(cur, halo_ref, stage_ref, first_tile):
    tm = cur.shape[0]

    @pl.when(first_tile)
    def _():
        stage_ref[0:SUBLANES, :] = jnp.zeros((SUBLANES, cur.shape[1]), _F32)

    @pl.when(jnp.logical_not(first_tile))
    def _():
        stage_ref[0:SUBLANES, :] = halo_ref[...]

    stage_ref[SUBLANES:SUBLANES + tm, :] = cur
    halo_ref[...] = cur[tm - SUBLANES:tm, :]
    return stage_ref[SUBLANES - 1:SUBLANES - 1 + tm, :], stage_ref[SUBLANES - 2:SUBLANES - 2 + tm, :]


def _proj_kernel(x_ref, wu_ref, wb_ref, wc_ref, wv_ref, cw_ref, g_ref, u_ref, yc_ref, halo_ref, stage_ref):
    xb = x_ref[...].astype(_BF16)
    u_ref[...] = jnp.dot(xb, wu_ref[...], preferred_element_type=_F32).astype(_BF16)
    gate_c = jnp.dot(xb, wc_ref[...], preferred_element_type=_F32)
    v = jnp.dot(xb, wv_ref[...], preferred_element_type=_F32)
    cv = gate_c * v
    cv1, cv2 = _causal_taps(cv, halo_ref, stage_ref, pl.program_id(0) == 0)
    conv = cw_ref[0:1, :] * cv2 + cw_ref[1:2, :] * cv1 + cw_ref[2:3, :] * cv
    gate_b = jnp.dot(xb, wb_ref[...], preferred_element_type=_F32)
    yc_ref[...] = _rmsnorm(gate_b * conv, g_ref[...]).astype(_BF16)


def _proj_call(x, w_in_bf, sconv_w, norm_conv_g):
    tm = TM_PROJ
    wspec = lambda j: pl.BlockSpec((D_MODEL, 1024), lambda i, j=j: (0, j), pipeline_mode=pl.Buffered(1))
    return pl.pallas_call(
        _proj_kernel,
        grid=(SEQ // tm,),
        in_specs=[
            pl.BlockSpec((tm, D_MODEL), lambda i: (i, 0)),
            wspec(0), wspec(1), wspec(2), wspec(3),
            _resident((3, CONV_WIDTH)),
            _resident((1, CONV_WIDTH)),
        ],
        out_specs=[
            pl.BlockSpec((tm, SSM_WIDTH), lambda i: (i, 0)),
            pl.BlockSpec((tm, CONV_WIDTH), lambda i: (i, 0)),
        ],
        out_shape=[
            jax.ShapeDtypeStruct((SEQ, SSM_WIDTH), _BF16),
            jax.ShapeDtypeStruct((SEQ, CONV_WIDTH), _BF16),
        ],
        scratch_shapes=[
            pltpu.VMEM((SUBLANES, CONV_WIDTH), _F32),
            pltpu.VMEM((SUBLANES + tm, CONV_WIDTH), _F32),
        ],
        compiler_params=pltpu.CompilerParams(dimension_semantics=("arbitrary",), vmem_limit_bytes=VMEM_LIMIT),
        name="proj_shortconv",
    )(x, w_in_bf, w_in_bf, w_in_bf, w_in_bf, sconv_w, norm_conv_g)


def _cmul_add(ar, ai, br, bi, cr, ci):
    return ar * br - ai * bi + cr, ar * bi + ai * br + ci


def _ssm_kernel(u_ref, perm_ref, permt_ref, bmat_ref, pow_ref, cmat_ref, d_ref, gw_ref, gb_ref, g_ref, ys_ref,
                x0_ref, xb_ref, state_ref, y_ref):
    nc, ns = SSM_CHUNKS, SSM_STEPS

    @pl.when(pl.program_id(0) == 0)
    def _():
        state_ref[...] = jnp.zeros(state_ref.shape, _F32)

    up = jnp.dot(perm_ref[...], u_ref[...], preferred_element_type=_F32)
    upb = up.astype(_BF16)
    re = slice(0, SLAB_STATE)
    im = slice(SLAB_STATE, SLAB_LANES)
    for k in range(N_SLABS):
        x0_ref[k] = jnp.dot(upb[:, k * SLAB_CH:(k + 1) * SLAB_CH], bmat_ref[k], preferred_element_type=_F32)
        lr = jnp.broadcast_to(pow_ref[k, 0, 0:1, :], (nc, SLAB_STATE))
        li = jnp.broadcast_to(pow_ref[k, 1, 0:1, :], (nc, SLAB_STATE))
        xr = jnp.zeros((nc, SLAB_STATE), _F32)
        xi = jnp.zeros((nc, SLAB_STATE), _F32)
        for s in range(ns):
            rows = slice(s * nc, (s + 1) * nc)
            xr, xi = _cmul_add(lr, li, xr, xi, x0_ref[k, rows, re], x0_ref[k, rows, im])
            x0_ref[k, rows, re] = xr
            x0_ref[k, rows, im] = xi

        tr = pow_ref[k, 0, ns - 1:ns, :]
        ti = pow_ref[k, 1, ns - 1:ns, :]
        cr = state_ref[k, 0:1, :]
        ci = state_ref[k, 1:2, :]
        starts_r, starts_i = [], []
        for c in range(nc):
            starts_r.append(cr)
            starts_i.append(ci)
            cr, ci = _cmul_add(tr, ti, cr, ci, xr[c:c + 1, :], xi[c:c + 1, :])
        state_ref[k, 0:1, :] = cr
        state_ref[k, 1:2, :] = ci
        sr = jnp.concatenate(starts_r, axis=0)
        si = jnp.concatenate(starts_i, axis=0)

        for s in range(0, ns, 2):
            parts_r, parts_i = [], []
            for q in (s, s + 1):
                rows = slice(q * nc, (q + 1) * nc)
                fr, fi = _cmul_add(pow_ref[k, 0, q:q + 1, :], pow_ref[k, 1, q:q + 1, :], sr, si,
                                   x0_ref[k, rows, re], x0_ref[k, rows, im])
                parts_r.append(fr)
                parts_i.append(fi)
            rows2 = slice(s * nc, (s + 2) * nc)
            xb_ref[rows2, k * SLAB_LANES:k * SLAB_LANES + SLAB_STATE] = jnp.concatenate(parts_r, axis=0).astype(_BF16)
            xb_ref[rows2, k * SLAB_LANES + SLAB_STATE:(k + 1) * SLAB_LANES] = (
                jnp.concatenate(parts_i, axis=0).astype(_BF16))
        y_ref[:, k * SLAB_CH:(k + 1) * SLAB_CH] = jnp.dot(
            xb_ref[:, k * SLAB_LANES:(k + 1) * SLAB_LANES], cmat_ref[k], preferred_element_type=_F32)

    y = y_ref[...] + d_ref[...] * up
    g = jax.nn.gelu(y)
    z = jnp.dot(g.astype(_BF16), gw_ref[...], preferred_element_type=_F32) + gb_ref[...]
    out = _rmsnorm(g * jax.nn.sigmoid(z), g_ref[...]).astype(_BF16)
    ys_ref[...] = jnp.dot(permt_ref[...], out, preferred_element_type=_F32).astype(_BF16)


def _ssm_call(u, perm, permt, bmat, pows, cmat, d_skip, glu_w_bf, glu_b, norm_g):
    tm = TM_SSM
    return pl.pallas_call(
        _ssm_kernel,
        grid=(SEQ // tm,),
        in_specs=[
            pl.BlockSpec((tm, SSM_WIDTH), lambda i: (i, 0)),
            _resident((tm, tm)),
            _resident((tm, tm)),
            _resident((N_SLABS, SLAB_CH, SLAB_LANES)),
            _resident((N_SLABS, 2, SSM_STEPS, SLAB_STATE)),
            _resident((N_SLABS, SLAB_LANES, SLAB_CH)),
            _resident((1, SSM_WIDTH)),
            _resident((SSM_WIDTH, SSM_WIDTH)),
            _resident((1, SSM_WIDTH)),
            _resident((1, SSM_WIDTH)),
        ],
        out_specs=pl.BlockSpec((tm, SSM_WIDTH), lambda i: (i, 0)),
        out_shape=jax.ShapeDtypeStruct((SEQ, SSM_WIDTH), _BF16),
        scratch_shapes=[
            pltpu.VMEM((N_SLABS, tm, SLAB_LANES), _F32),
            pltpu.VMEM((tm, N_SLABS * SLAB_LANES), _BF16),
            pltpu.VMEM((N_SLABS, 2, SLAB_STATE), _F32),
            pltpu.VMEM((tm, SSM_WIDTH), _F32),
        ],
        compiler_params=pltpu.CompilerParams(dimension_semantics=("arbitrary",), vmem_limit_bytes=VMEM_LIMIT),
        name="s5_mixer",
    )(u, perm, permt, bmat, pows, cmat, d_skip, glu_w_bf, glu_b, norm_g)


def _s5_operands(lam_re, lam_im, log_dt, b_re, b_im, c_re, c_im):
    dt = jnp.exp(log_dt)[:, None]
    mag = jnp.exp(lam_re * dt)
    bar_re = mag * jnp.cos(lam_im * dt)
    bar_im = mag * jnp.sin(lam_im * dt)
    inv = 1.0 / (lam_re * lam_re + lam_im * lam_im)
    coef_re = ((bar_re - 1.0) * lam_re + bar_im * lam_im) * inv
    coef_im = (bar_im * lam_re - (bar_re - 1.0) * lam_im) * inv
    bbar_re = coef_re[..., None] * b_re - coef_im[..., None] * b_im
    bbar_im = coef_re[..., None] * b_im + coef_im[..., None] * b_re
    eye = jnp.eye(SLAB_GROUPS, dtype=_F32)

    def b_block(part):
        part = part.reshape(N_SLABS, SLAB_GROUPS, STATE, GROUP_CH)
        blk = jnp.einsum('kgph,gj->kghjp', part, eye)
        return blk.reshape(N_SLABS, SLAB_CH, SLAB_STATE)

    def c_block(part):
        part = part.reshape(N_SLABS, SLAB_GROUPS, GROUP_CH, STATE)
        blk = jnp.einsum('kghp,gj->kgpjh', part, eye)
        return blk.reshape(N_SLABS, SLAB_STATE, SLAB_CH)

    bmat = jnp.concatenate([b_block(bbar_re), b_block(bbar_im)], axis=2).astype(_BF16)
    cmat = jnp.concatenate([c_block(c_re), c_block(-c_im)], axis=1).astype(_BF16)
    steps = jnp.arange(1, SSM_STEPS + 1, dtype=_F32)[:, None, None]
    pmag = jnp.exp(steps * (lam_re * dt))
    pre = (pmag * jnp.cos(steps * (lam_im * dt))).reshape(SSM_STEPS, N_SLABS, SLAB_STATE)
    pim = (pmag * jnp.sin(steps * (lam_im * dt))).reshape(SSM_STEPS, N_SLABS, SLAB_STATE)
    pows = jnp.stack([pre.transpose(1, 0, 2), pim.transpose(1, 0, 2)], axis=1)
    return bmat, pows, cmat


def _regroup_matrices():
    q = jnp.arange(TM_SSM)
    src = (q % SSM_CHUNKS) * SSM_STEPS + q // SSM_CHUNKS
    perm = (src[:, None] == jnp.arange(TM_SSM)[None, :]).astype(_BF16)
    return perm, perm.T


def _outproj_kernel(ys_ref, yc_ref, wt_ref, wb_ref, x_ref, g_ref, b_ref, h_ref):
    mix = jnp.dot(ys_ref[...], wt_ref[...], preferred_element_type=_F32)
    mix = mix + jnp.dot(yc_ref[...], wb_ref[...], preferred_element_type=_F32)
    h_ref[...] = _layernorm(ALPHA * x_ref[...] + mix, g_ref[...], b_ref[...])


def _outproj_call(ys, yc, w_out_bf, x, ln_g, ln_b):
    tm = TM_OUT
    half = lambda j: pl.BlockSpec((SSM_WIDTH, D_MODEL), lambda i, j=j: (j, 0), pipeline_mode=pl.Buffered(1))
    return pl.pallas_call(
        _outproj_kernel,
        grid=(SEQ // tm,),
        in_specs=[
            pl.BlockSpec((tm, SSM_WIDTH), lambda i: (i, 0)),
            pl.BlockSpec((tm, CONV_WIDTH), lambda i: (i, 0)),
            half(0), half(1),
            pl.BlockSpec((tm, D_MODEL), lambda i: (i, 0)),
            _resident((1, D_MODEL)),
            _resident((1, D_MODEL)),
        ],
        out_specs=pl.BlockSpec((tm, D_MODEL), lambda i: (i, 0)),
        out_shape=jax.ShapeDtypeStruct((SEQ, D_MODEL), _F32),
        compiler_params=pltpu.CompilerParams(dimension_semantics=("arbitrary",), vmem_limit_bytes=VMEM_LIMIT),
        name="outproj_ln1",
    )(ys, yc, w_out_bf, w_out_bf, x, ln_g, ln_b)


def _ffn_kernel(h_ref, wg_ref, wu_ref, wd_ref, cw_ref, cb_ref, g_ref, b_ref, o_ref,
                hb_ref, halo_ref, stage_ref):
    i = pl.program_id(0)
    k = pl.program_id(1)

    @pl.when(k == 0)
    def _():
        hb_ref[...] = h_ref[...].astype(_BF16)
        o_ref[...] = jnp.zeros(o_ref.shape, _F32)

    tm = hb_ref.shape[0]
    hm = tm // FFN_ROW_SPLIT

    @pl.when(i == 0)
    def _():
        stage_ref[0:SUBLANES, :] = jnp.zeros((SUBLANES, stage_ref.shape[1]), _F32)

    @pl.when(i > 0)
    def _():
        stage_ref[0:SUBLANES, :] = halo_ref[k]

    for r in range(FFN_ROW_SPLIT):
        rows = slice(r * hm, (r + 1) * hm)
        hb = hb_ref[rows, :]
        gate = jnp.dot(hb, wg_ref[...], preferred_element_type=_F32)
        up = jnp.dot(hb, wu_ref[...], preferred_element_type=_F32)
        stage_ref[SUBLANES + r * hm:SUBLANES + (r + 1) * hm, :] = gate
        g1 = stage_ref[SUBLANES - 1 + r * hm:SUBLANES - 1 + (r + 1) * hm, :]
        g2 = stage_ref[SUBLANES - 2 + r * hm:SUBLANES - 2 + (r + 1) * hm, :]
        a = cw_ref[0:1, :] * g2 + cw_ref[1:2, :] * g1 + cw_ref[2:3, :] * gate + cb_ref[...]
        act = (jax.nn.silu(a) * up).astype(_BF16)
        o_ref[rows, :] += jnp.dot(act, wd_ref[...], preferred_element_type=_F32)
    halo_ref[k] = stage_ref[tm:tm + SUBLANES, :]

    @pl.when(k == pl.num_programs(1) - 1)
    def _():
        o_ref[...] = _layernorm(ALPHA * h_ref[...] + o_ref[...], g_ref[...], b_ref[...])


def _ffn_call(h, wg_bf, wu_bf, wd_bf, conv_w, conv_b, ln_g, ln_b):
    tm, tf = TM_FFN, TF_FFN
    nk = D_FF // tf
    return pl.pallas_call(
        _ffn_kernel,
        grid=(SEQ // tm, nk),
        in_specs=[
            pl.BlockSpec((tm, D_MODEL), lambda i, k: (i, 0)),
            pl.BlockSpec((D_MODEL, tf), lambda i, k: (0, k)),
            pl.BlockSpec((D_MODEL, tf), lambda i, k: (0, k)),
            pl.BlockSpec((tf, D_MODEL), lambda i, k: (k, 0)),
            pl.BlockSpec((3, tf), lambda i, k: (0, k)),
            pl.BlockSpec((1, tf), lambda i, k: (0, k)),
            _resident((1, D_MODEL)),
            _resident((1, D_MODEL)),
        ],
        out_specs=pl.BlockSpec((tm, D_MODEL), lambda i, k: (i, 0)),
        out_shape=jax.ShapeDtypeStruct((SEQ, D_MODEL), _F32),
        scratch_shapes=[
            pltpu.VMEM((tm, D_MODEL), _BF16),
            pltpu.VMEM((nk, SUBLANES, tf), _F32),
            pltpu.VMEM((SUBLANES + tm, tf), _F32),
        ],
        compiler_params=pltpu.CompilerParams(dimension_semantics=("arbitrary", "arbitrary"),
                                             vmem_limit_bytes=VMEM_LIMIT),
        name="convffn_ln2",
    )(h, wg_bf, wu_bf, wd_bf, conv_w, conv_b, ln_g, ln_b)


def kernel(x, w_in, ssm_lambda_re, ssm_lambda_im, ssm_log_dt, ssm_b_re, ssm_b_im, ssm_c_re, ssm_c_im,
           ssm_d, ssm_glu_w, ssm_glu_b, sconv_w, norm_ssm_g, norm_conv_g, w_out, ln1_g, ln1_b,
           ffn_w_gate, ffn_w_up, ffn_conv_w, ffn_conv_b, ffn_w_down, ln2_g, ln2_b):
    assert x.shape == (1, SEQ, D_MODEL) and w_in.shape[0] == DEPTH
    h = x[0]
    perm, permt = _regroup_matrices()
    for l in range(DEPTH):
        row = lambda p: p[l].reshape(1, -1)
        u, yc = _proj_call(h, w_in[l].astype(_BF16), sconv_w[l], row(norm_conv_g))
        bmat, pows, cmat = _s5_operands(ssm_lambda_re[l], ssm_lambda_im[l], ssm_log_dt[l],
                                        ssm_b_re[l], ssm_b_im[l], ssm_c_re[l], ssm_c_im[l])
        ys = _ssm_call(u, perm, permt, bmat, pows, cmat, row(ssm_d), ssm_glu_w[l].astype(_BF16),
                       row(ssm_glu_b), row(norm_ssm_g))
        h = _outproj_call(ys, yc, w_out[l].astype(_BF16), h, row(ln1_g), row(ln1_b))
        h = _ffn_call(h, ffn_w_gate[l].astype(_BF16), ffn_w_up[l].astype(_BF16), ffn_w_down[l].astype(_BF16),
                      ffn_conv_w[l], row(ffn_conv_b), row(ln2_g), row(ln2_b))
    return h[None]
```

```python
import jax
import jax.numpy as jnp
from jax import lax
from jax.experimental import pallas as pl
from jax.experimental.pallas import tpu as pltpu

SEQ = 8192
D_MODEL = 2048
SSM_WIDTH = 1024
CONV_WIDTH = 1024
GROUP_CH = 16
GROUPS = SSM_WIDTH // GROUP_CH
STATE = 64
D_FF = 5632
DEPTH = 1
LN_EPS = 1e-5
RMS_EPS = 1e-6
ALPHA = (2.0 * DEPTH) ** 0.25

SUBLANES = 8
SLAB_GROUPS = 8
N_SLABS = GROUPS // SLAB_GROUPS
SLAB_CH = SLAB_GROUPS * GROUP_CH
SLAB_STATE = SLAB_GROUPS * STATE
SLAB_LANES = 2 * SLAB_STATE

TM_PROJ = 512
TM_SSM = 512
SSM_CHUNKS = SUBLANES
SSM_STEPS = TM_SSM // SSM_CHUNKS
TM_OUT = 512
TM_FFN = 512
TF_FFN = 512
FFN_ROW_SPLIT = 2
VMEM_LIMIT = 56 * 1024 * 1024

_F32 = jnp.float32
_BF16 = jnp.bfloat16


def _resident(shape):
    return pl.BlockSpec(shape, lambda *_: (0,) * len(shape), pipeline_mode=pl.Buffered(1))


def _layernorm(r, g, b):
    mu = jnp.mean(r, axis=-1, keepdims=True)
    rc = r - mu
    var = jnp.mean(rc * rc, axis=-1, keepdims=True)
    return rc * lax.rsqrt(var + LN_EPS) * g + b


def _rmsnorm(y, g):
    return y * lax.rsqrt(jnp.mean(y * y, axis=-1, keepdims=True) + RMS_EPS) * g


def _causal_taps(cur, halo_ref, stage_ref, first_tile):
    tm = cur.shape[0]

    @pl.when(first_tile)
    def _():
        stage_ref[0:SUBLANES, :] = jnp.zeros((SUBLANES, cur.shape[1]), _F32)

    @pl.when(jnp.logical_not(first_tile))
    def _():
        stage_ref[0:SUBLANES, :] = halo_ref[...]

    stage_ref[SUBLANES:SUBLANES + tm, :] = cur
    halo_ref[...] = cur[tm - SUBLANES:tm, :]
    return stage_ref[SUBLANES - 1:SUBLANES - 1 + tm, :], stage_ref[SUBLANES - 2:SUBLANES - 2 + tm, :]


def _proj_kernel(x_ref, wu_ref, wb_ref, wc_ref, wv_ref, cw_ref, g_ref, u_ref, yc_ref, halo_ref, stage_ref):
    xb = x_ref[...].astype(_BF16)
    u_ref[...] = jnp.dot(xb, wu_ref[...], preferred_element_type=_F32).astype(_BF16)
    gate_c = jnp.dot(xb, wc_ref[...], preferred_element_type=_F32)
    v = jnp.dot(xb, wv_ref[...], preferred_element_type=_F32)
    cv = gate_c * v
    cv1, cv2 = _causal_taps(cv, halo_ref, stage_ref, pl.program_id(0) == 0)
    conv = cw_ref[0:1, :] * cv2 + cw_ref[1:2, :] * cv1 + cw_ref[2:3, :] * cv
    gate_b = jnp.dot(xb, wb_ref[...], preferred_element_type=_F32)
    yc_ref[...] = _rmsnorm(gate_b * conv, g_ref[...]).astype(_BF16)


def _proj_call(x, w_in_bf, sconv_w, norm_conv_g):
    tm = TM_PROJ
    wspec = lambda j: pl.BlockSpec((D_MODEL, 1024), lambda i, j=j: (0, j), pipeline_mode=pl.Buffered(1))
    return pl.pallas_call(
        _proj_kernel,
        grid=(SEQ // tm,),
        in_specs=[
            pl.BlockSpec((tm, D_MODEL), lambda i: (i, 0)),
            wspec(0), wspec(1), wspec(2), wspec(3),
            _resident((3, CONV_WIDTH)),
            _resident((1, CONV_WIDTH)),
        ],
        out_specs=[
            pl.BlockSpec((tm, SSM_WIDTH), lambda i: (i, 0)),
            pl.BlockSpec((tm, CONV_WIDTH), lambda i: (i, 0)),
        ],
        out_shape=[
            jax.ShapeDtypeStruct((SEQ, SSM_WIDTH), _BF16),
            jax.ShapeDtypeStruct((SEQ, CONV_WIDTH), _BF16),
        ],
        scratch_shapes=[
            pltpu.VMEM((SUBLANES, CONV_WIDTH), _F32),
            pltpu.VMEM((SUBLANES + tm, CONV_WIDTH), _F32),
        ],
        compiler_params=pltpu.CompilerParams(dimension_semantics=("arbitrary",), vmem_limit_bytes=VMEM_LIMIT),
        name="proj_shortconv",
    )(x, w_in_bf, w_in_bf, w_in_bf, w_in_bf, sconv_w, norm_conv_g)


def _cmul_add(ar, ai, br, bi, cr, ci):
    return ar * br - ai * bi + cr, ar * bi + ai * br + ci


def _ssm_kernel(u_ref, perm_ref, permt_ref, bmat_ref, lam_ref, cmat_ref, d_ref, gw_ref, gb_ref, g_ref, ys_ref,
                bu_ref, xb_ref, state_ref, y_ref):
    nc, ns = SSM_CHUNKS, SSM_STEPS

    @pl.when(pl.program_id(0) == 0)
    def _():
        state_ref[...] = jnp.zeros(state_ref.shape, _F32)

    up = jnp.dot(perm_ref[...], u_ref[...], preferred_element_type=_F32)
    upb = up.astype(_BF16)
    re = slice(0, SLAB_STATE)
    im = slice(SLAB_STATE, SLAB_LANES)
    for k in range(N_SLABS):
        bu_ref[k] = jnp.dot(upb[:, k * SLAB_CH:(k + 1) * SLAB_CH], bmat_ref[k], preferred_element_type=_F32)
        lr = jnp.broadcast_to(lam_ref[k, 0:1, :], (nc, SLAB_STATE))
        li = jnp.broadcast_to(lam_ref[k, 1:2, :], (nc, SLAB_STATE))

        xr = jnp.zeros((nc, SLAB_STATE), _F32)
        xi = jnp.zeros((nc, SLAB_STATE), _F32)
        for s in range(ns):
            rows = slice(s * nc, (s + 1) * nc)
            xr, xi = _cmul_add(lr, li, xr, xi, bu_ref[k, rows, re], bu_ref[k, rows, im])

        tr = lam_ref[k, 2:3, :]
        ti = lam_ref[k, 3:4, :]
        cr = state_ref[k, 0:1, :]
        ci = state_ref[k, 1:2, :]
        starts_r, starts_i = [], []
        for c in range(nc):
            starts_r.append(cr)
            starts_i.append(ci)
            cr, ci = _cmul_add(tr, ti, cr, ci, xr[c:c + 1, :], xi[c:c + 1, :])
        state_ref[k, 0:1, :] = cr
        state_ref[k, 1:2, :] = ci
        xr = jnp.concatenate(starts_r, axis=0)
        xi = jnp.concatenate(starts_i, axis=0)

        for s in range(0, ns, 2):
            parts_r, parts_i = [], []
            for q in (s, s + 1):
                rows = slice(q * nc, (q + 1) * nc)
                xr, xi = _cmul_add(lr, li, xr, xi, bu_ref[k, rows, re], bu_ref[k, rows, im])
                parts_r.append(xr)
                parts_i.append(xi)
            rows2 = slice(s * nc, (s + 2) * nc)
            xb_ref[rows2, k * SLAB_LANES:k * SLAB_LANES + SLAB_STATE] = jnp.concatenate(parts_r, axis=0).astype(_BF16)
            xb_ref[rows2, k * SLAB_LANES + SLAB_STATE:(k + 1) * SLAB_LANES] = (
                jnp.concatenate(parts_i, axis=0).astype(_BF16))
        y_ref[:, k * SLAB_CH:(k + 1) * SLAB_CH] = jnp.dot(
            xb_ref[:, k * SLAB_LANES:(k + 1) * SLAB_LANES], cmat_ref[k], preferred_element_type=_F32)

    y = y_ref[...] + d_ref[...] * up
    g = jax.nn.gelu(y)
    z = jnp.dot(g.astype(_BF16), gw_ref[...], preferred_element_type=_F32) + gb_ref[...]
    out = _rmsnorm(g * jax.nn.sigmoid(z), g_ref[...]).astype(_BF16)
    ys_ref[...] = jnp.dot(permt_ref[...], out, preferred_element_type=_F32).astype(_BF16)


def _ssm_call(u, perm, permt, bmat, lam_rows, cmat, d_skip, glu_w_bf, glu_b, norm_g):
    tm = TM_SSM
    return pl.pallas_call(
        _ssm_kernel,
        grid=(SEQ // tm,),
        in_specs=[
            pl.BlockSpec((tm, SSM_WIDTH), lambda i: (i, 0)),
            _resident((tm, tm)),
            _resident((tm, tm)),
            _resident((N_SLABS, SLAB_CH, SLAB_LANES)),
            _resident((N_SLABS, 4, SLAB_STATE)),
            _resident((N_SLABS, SLAB_LANES, SLAB_CH)),
            _resident((1, SSM_WIDTH)),
            _resident((SSM_WIDTH, SSM_WIDTH)),
            _resident((1, SSM_WIDTH)),
            _resident((1, SSM_WIDTH)),
        ],
        out_specs=pl.BlockSpec((tm, SSM_WIDTH), lambda i: (i, 0)),
        out_shape=jax.ShapeDtypeStruct((SEQ, SSM_WIDTH), _BF16),
        scratch_shapes=[
            pltpu.VMEM((N_SLABS, tm, SLAB_LANES), _F32),
            pltpu.VMEM((tm, N_SLABS * SLAB_LANES), _BF16),
            pltpu.VMEM((N_SLABS, 2, SLAB_STATE), _F32),
            pltpu.VMEM((tm, SSM_WIDTH), _F32),
        ],
        compiler_params=pltpu.CompilerParams(dimension_semantics=("arbitrary",), vmem_limit_bytes=VMEM_LIMIT),
        name="s5_mixer",
    )(u, perm, permt, bmat, lam_rows, cmat, d_skip, glu_w_bf, glu_b, norm_g)


def _s5_operands(lam_re, lam_im, log_dt, b_re, b_im, c_re, c_im):
    dt = jnp.exp(log_dt)[:, None]
    mag = jnp.exp(lam_re * dt)
    bar_re = mag * jnp.cos(lam_im * dt)
    bar_im = mag * jnp.sin(lam_im * dt)
    inv = 1.0 / (lam_re * lam_re + lam_im * lam_im)
    coef_re = ((bar_re - 1.0) * lam_re + bar_im * lam_im) * inv
    coef_im = (bar_im * lam_re - (bar_re - 1.0) * lam_im) * inv
    bbar_re = coef_re[..., None] * b_re - coef_im[..., None] * b_im
    bbar_im = coef_re[..., None] * b_im + coef_im[..., None] * b_re
    eye = jnp.eye(SLAB_GROUPS, dtype=_F32)

    def b_block(part):
        part = part.reshape(N_SLABS, SLAB_GROUPS, STATE, GROUP_CH)
        blk = jnp.einsum('kgph,gj->kghjp', part, eye)
        return blk.reshape(N_SLABS, SLAB_CH, SLAB_STATE)

    def c_block(part):
        part = part.reshape(N_SLABS, SLAB_GROUPS, GROUP_CH, STATE)
        blk = jnp.einsum('kghp,gj->kgpjh', part, eye)
        return blk.reshape(N_SLABS, SLAB_STATE, SLAB_CH)

    bmat = jnp.concatenate([b_block(bbar_re), b_block(bbar_im)], axis=2).astype(_BF16)
    cmat = jnp.concatenate([c_block(c_re), c_block(-c_im)], axis=1).astype(_BF16)
    nmag = jnp.exp(SSM_STEPS * (lam_re * dt))
    rows = [bar_re, bar_im, nmag * jnp.cos(SSM_STEPS * (lam_im * dt)), nmag * jnp.sin(SSM_STEPS * (lam_im * dt))]
    lam_rows = jnp.stack([r.reshape(N_SLABS, SLAB_STATE) for r in rows], axis=1)
    return bmat, lam_rows, cmat


def _regroup_matrices():
    q = jnp.arange(TM_SSM)
    src = (q % SSM_CHUNKS) * SSM_STEPS + q // SSM_CHUNKS
    perm = (src[:, None] == jnp.arange(TM_SSM)[None, :]).astype(_BF16)
    return perm, perm.T


def _outproj_kernel(ys_ref, yc_ref, wt_ref, wb_ref, x_ref, g_ref, b_ref, h_ref):
    mix = jnp.dot(ys_ref[...], wt_ref[...], preferred_element_type=_F32)
    mix = mix + jnp.dot(yc_ref[...], wb_ref[...], preferred_element_type=_F32)
    h_ref[...] = _layernorm(ALPHA * x_ref[...] + mix, g_ref[...], b_ref[...])


def _outproj_call(ys, yc, w_out_bf, x, ln_g, ln_b):
    tm = TM_OUT
    half = lambda j: pl.BlockSpec((SSM_WIDTH, D_MODEL), lambda i, j=j: (j, 0), pipeline_mode=pl.Buffered(1))
    return pl.pallas_call(
        _outproj_kernel,
        grid=(SEQ // tm,),
        in_specs=[
            pl.BlockSpec((tm, SSM_WIDTH), lambda i: (i, 0)),
            pl.BlockSpec((tm, CONV_WIDTH), lambda i: (i, 0)),
            half(0), half(1),
            pl.BlockSpec((tm, D_MODEL), lambda i: (i, 0)),
            _resident((1, D_MODEL)),
            _resident((1, D_MODEL)),
        ],
        out_specs=pl.BlockSpec((tm, D_MODEL), lambda i: (i, 0)),
        out_shape=jax.ShapeDtypeStruct((SEQ, D_MODEL), _F32),
        compiler_params=pltpu.CompilerParams(dimension_semantics=("arbitrary",), vmem_limit_bytes=VMEM_LIMIT),
        name="outproj_ln1",
    )(ys, yc, w_out_bf, w_out_bf, x, ln_g, ln_b)


def _ffn_kernel(h_ref, wg_ref, wu_ref, wd_ref, cw_ref, cb_ref, g_ref, b_ref, o_ref,
                hb_ref, halo_ref, stage_ref):
    i = pl.program_id(0)
    k = pl.program_id(1)

    @pl.when(k == 0)
    def _():
        hb_ref[...] = h_ref[...].astype(_BF16)
        o_ref[...] = jnp.zeros(o_ref.shape, _F32)

    tm = hb_ref.shape[0]
    hm = tm // FFN_ROW_SPLIT

    @pl.when(i == 0)
    def _():
        stage_ref[0:SUBLANES, :] = jnp.zeros((SUBLANES, stage_ref.shape[1]), _F32)

    @pl.when(i > 0)
    def _():
        stage_ref[0:SUBLANES, :] = halo_ref[k]

    for r in range(FFN_ROW_SPLIT):
        rows = slice(r * hm, (r + 1) * hm)
        hb = hb_ref[rows, :]
        gate = jnp.dot(hb, wg_ref[...], preferred_element_type=_F32)
        up = jnp.dot(hb, wu_ref[...], preferred_element_type=_F32)
        stage_ref[SUBLANES + r * hm:SUBLANES + (r + 1) * hm, :] = gate
        g1 = stage_ref[SUBLANES - 1 + r * hm:SUBLANES - 1 + (r + 1) * hm, :]
        g2 = stage_ref[SUBLANES - 2 + r * hm:SUBLANES - 2 + (r + 1) * hm, :]
        a = cw_ref[0:1, :] * g2 + cw_ref[1:2, :] * g1 + cw_ref[2:3, :] * gate + cb_ref[...]
        act = (jax.nn.silu(a) * up).astype(_BF16)
        o_ref[rows, :] += jnp.dot(act, wd_ref[...], preferred_element_type=_F32)
    halo_ref[k] = stage_ref[tm:tm + SUBLANES, :]

    @pl.when(k == pl.num_programs(1) - 1)
    def _():
        o_ref[...] = _layernorm(ALPHA * h_ref[...] + o_ref[...], g_ref[...], b_ref[...])


def _ffn_call(h, wg_bf, wu_bf, wd_bf, conv_w, conv_b, ln_g, ln_b):
    tm, tf = TM_FFN, TF_FFN
    nk = D_FF // tf
    return pl.pallas_call(
        _ffn_kernel,
        grid=(SEQ // tm, nk),
        in_specs=[
            pl.BlockSpec((tm, D_MODEL), lambda i, k: (i, 0)),
            pl.BlockSpec((D_MODEL, tf), lambda i, k: (0, k)),
            pl.BlockSpec((D_MODEL, tf), lambda i, k: (0, k)),
            pl.BlockSpec((tf, D_MODEL), lambda i, k: (k, 0)),
            pl.BlockSpec((3, tf), lambda i, k: (0, k)),
            pl.BlockSpec((1, tf), lambda i, k: (0, k)),
            _resident((1, D_MODEL)),
            _resident((1, D_MODEL)),
        ],
        out_specs=pl.BlockSpec((tm, D_MODEL), lambda i, k: (i, 0)),
        out_shape=jax.ShapeDtypeStruct((SEQ, D_MODEL), _F32),
        scratch_shapes=[
            pltpu.VMEM((tm, D_MODEL), _BF16),
            pltpu.VMEM((nk, SUBLANES, tf), _F32),
            pltpu.VMEM((SUBLANES + tm, tf), _F32),
        ],
        compiler_params=pltpu.CompilerParams(dimension_semantics=("arbitrary", "arbitrary"),
                                             vmem_limit_bytes=VMEM_LIMIT),
        name="convffn_ln2",
    )(h, wg_bf, wu_bf, wd_bf, conv_w, conv_b, ln_g, ln_b)


def kernel(x, w_in, ssm_lambda_re, ssm_lambda_im, ssm_log_dt, ssm_b_re, ssm_b_im, ssm_c_re, ssm_c_im,
           ssm_d, ssm_glu_w, ssm_glu_b, sconv_w, norm_ssm_g, norm_conv_g, w_out, ln1_g, ln1_b,
           ffn_w_gate, ffn_w_up, ffn_conv_w, ffn_conv_b, ffn_w_down, ln2_g, ln2_b):
    assert x.shape == (1, SEQ, D_MODEL) and w_in.shape[0] == DEPTH
    h = x[0]
    perm, permt = _regroup_matrices()
    for l in range(DEPTH):
        row = lambda p: p[l].reshape(1, -1)
        u, yc = _proj_call(h, w_in[l].astype(_BF16), sconv_w[l], row(norm_conv_g))
        bmat, lam_rows, cmat = _s5_operands(ssm_lambda_re[l], ssm_lambda_im[l], ssm_log_dt[l],
                                            ssm_b_re[l], ssm_b_im[l], ssm_c_re[l], ssm_c_im[l])
        ys = _ssm_call(u, perm, permt, bmat, lam_rows, cmat, row(ssm_d), ssm_glu_w[l].astype(_BF16),
                       row(ssm_glu_b), row(norm_ssm_g))
        h = _outproj_call(ys, yc, w_out[l].astype(_BF16), h, row(ln1_g), row(ln1_b))
        h = _ffn_call(h, ffn_w_gate[l].astype(_BF16), ffn_w_up[l].astype(_BF16), ffn_w_down[l].astype(_BF16),
                      ffn_conv_w[l], row(ffn_conv_b), row(ln2_g), row(ln2_b))
    return h[None]
```

```python
import jax
import jax.numpy as jnp
from jax import lax
from jax.experimental import pallas as pl
from jax.experimental.pallas import tpu as pltpu

SEQ = 8192
D_MODEL = 2048
SSM_WIDTH = 1024
CONV_WIDTH = 1024
GROUP_CH = 16
GROUPS = SSM_WIDTH // GROUP_CH
STATE = 64
D_FF = 5632
DEPTH = 1
LN_EPS = 1e-5
RMS_EPS = 1e-6
ALPHA = (2.0 * DEPTH) ** 0.25

SUBLANES = 8
SLAB_GROUPS = 8
N_SLABS = GROUPS // SLAB_GROUPS
SLAB_CH = SLAB_GROUPS * GROUP_CH
SLAB_STATE = SLAB_GROUPS * STATE
SLAB_LANES = 2 * SLAB_STATE

TM_PROJ = 512
TM_SSM = 512
SSM_CHUNKS = SUBLANES
SSM_STEPS = TM_SSM // SSM_CHUNKS
TM_OUT = 512
TM_FFN = 1024
TF_FFN = 512
FFN_ROW_SPLIT = 2
VMEM_LIMIT = 56 * 1024 * 1024

_F32 = jnp.float32
_BF16 = jnp.bfloat16


def _resident(shape):
    return pl.BlockSpec(shape, lambda *_: (0,) * len(shape), pipeline_mode=pl.Buffered(1))


def _layernorm(r, g, b):
    mu = jnp.mean(r, axis=-1, keepdims=True)
    rc = r - mu
    var = jnp.mean(rc * rc, axis=-1, keepdims=True)
    return rc * lax.rsqrt(var + LN_EPS) * g + b


def _rmsnorm(y, g):
    return y * lax.rsqrt(jnp.mean(y * y, axis=-1, keepdims=True) + RMS_EPS) * g


def _causal_taps(cur, halo_ref, stage_ref, first_tile):
    tm = cur.shape[0]

    @pl.when(first_tile)
    def _():
        stage_ref[0:SUBLANES, :] = jnp.zeros((SUBLANES, cur.shape[1]), _F32)

    @pl.when(jnp.logical_not(first_tile))
    def _():
        stage_ref[0:SUBLANES, :] = halo_ref[...]

    stage_ref[SUBLANES:SUBLANES + tm, :] = cur
    halo_ref[...] = cur[tm - SUBLANES:tm, :]
    return stage_ref[SUBLANES - 1:SUBLANES - 1 + tm, :], stage_ref[SUBLANES - 2:SUBLANES - 2 + tm, :]


def _proj_kernel(x_ref, wu_ref, wb_ref, wc_ref, wv_ref, cw_ref, g_ref, u_ref, yc_ref, halo_ref, stage_ref):
    xb = x_ref[...].astype(_BF16)
    u_ref[...] = jnp.dot(xb, wu_ref[...], preferred_element_type=_F32).astype(_BF16)
    gate_c = jnp.dot(xb, wc_ref[...], preferred_element_type=_F32)
    v = jnp.dot(xb, wv_ref[...], preferred_element_type=_F32)
    cv = gate_c * v
    cv1, cv2 = _causal_taps(cv, halo_ref, stage_ref, pl.program_id(0) == 0)
    conv = cw_ref[0:1, :] * cv2 + cw_ref[1:2, :] * cv1 + cw_ref[2:3, :] * cv
    gate_b = jnp.dot(xb, wb_ref[...], preferred_element_type=_F32)
    yc_ref[...] = _rmsnorm(gate_b * conv, g_ref[...]).astype(_BF16)


def _proj_call(x, w_in_bf, sconv_w, norm_conv_g):
    tm = TM_PROJ
    wspec = lambda j: pl.BlockSpec((D_MODEL, 1024), lambda i, j=j: (0, j), pipeline_mode=pl.Buffered(1))
    return pl.pallas_call(
        _proj_kernel,
        grid=(SEQ // tm,),
        in_specs=[
            pl.BlockSpec((tm, D_MODEL), lambda i: (i, 0)),
            wspec(0), wspec(1), wspec(2), wspec(3),
            _resident((3, CONV_WIDTH)),
            _resident((1, CONV_WIDTH)),
        ],
        out_specs=[
            pl.BlockSpec((tm, SSM_WIDTH), lambda i: (i, 0)),
            pl.BlockSpec((tm, CONV_WIDTH), lambda i: (i, 0)),
        ],
        out_shape=[
            jax.ShapeDtypeStruct((SEQ, SSM_WIDTH), _BF16),
            jax.ShapeDtypeStruct((SEQ, CONV_WIDTH), _BF16),
        ],
        scratch_shapes=[
            pltpu.VMEM((SUBLANES, CONV_WIDTH), _F32),
            pltpu.VMEM((SUBLANES + tm, CONV_WIDTH), _F32),
        ],
        compiler_params=pltpu.CompilerParams(dimension_semantics=("arbitrary",), vmem_limit_bytes=VMEM_LIMIT),
        name="proj_shortconv",
    )(x, w_in_bf, w_in_bf, w_in_bf, w_in_bf, sconv_w, norm_conv_g)


def _cmul_add(ar, ai, br, bi, cr, ci):
    return ar * br - ai * bi + cr, ar * bi + ai * br + ci


def _ssm_kernel(u_ref, perm_ref, permt_ref, bmat_ref, lam_ref, cmat_ref, d_ref, gw_ref, gb_ref, g_ref, ys_ref,
                bu_ref, xb_ref, state_ref, y_ref):
    nc, ns = SSM_CHUNKS, SSM_STEPS

    @pl.when(pl.program_id(0) == 0)
    def _():
        state_ref[...] = jnp.zeros(state_ref.shape, _F32)

    upb = jnp.dot(perm_ref[...], u_ref[...], preferred_element_type=_F32).astype(_BF16)

    def expand(k):
        bu_ref[k] = jnp.dot(upb[:, k * SLAB_CH:(k + 1) * SLAB_CH], bmat_ref[k], preferred_element_type=_F32)

    expand(0)
    expand(1)
    re = slice(0, SLAB_STATE)
    im = slice(SLAB_STATE, SLAB_LANES)
    for k in range(N_SLABS):
        lr = jnp.broadcast_to(lam_ref[k, 0:1, :], (nc, SLAB_STATE))
        li = jnp.broadcast_to(lam_ref[k, 1:2, :], (nc, SLAB_STATE))

        xr = jnp.zeros((nc, SLAB_STATE), _F32)
        xi = jnp.zeros((nc, SLAB_STATE), _F32)
        for s in range(ns):
            rows = slice(s * nc, (s + 1) * nc)
            xr, xi = _cmul_add(lr, li, xr, xi, bu_ref[k, rows, re], bu_ref[k, rows, im])

        tr = lam_ref[k, 2:3, :]
        ti = lam_ref[k, 3:4, :]
        cr = state_ref[k, 0:1, :]
        ci = state_ref[k, 1:2, :]
        starts_r, starts_i = [], []
        for c in range(nc):
            starts_r.append(cr)
            starts_i.append(ci)
            cr, ci = _cmul_add(tr, ti, cr, ci, xr[c:c + 1, :], xi[c:c + 1, :])
        state_ref[k, 0:1, :] = cr
        state_ref[k, 1:2, :] = ci
        xr = jnp.concatenate(starts_r, axis=0)
        xi = jnp.concatenate(starts_i, axis=0)

        for s in range(0, ns, 2):
            parts_r, parts_i = [], []
            for q in (s, s + 1):
                rows = slice(q * nc, (q + 1) * nc)
                xr, xi = _cmul_add(lr, li, xr, xi, bu_ref[k, rows, re], bu_ref[k, rows, im])
                parts_r.append(xr)
                parts_i.append(xi)
            rows2 = slice(s * nc, (s + 2) * nc)
            xb_ref[rows2, k * SLAB_LANES:k * SLAB_LANES + SLAB_STATE] = jnp.concatenate(parts_r, axis=0).astype(_BF16)
            xb_ref[rows2, k * SLAB_LANES + SLAB_STATE:(k + 1) * SLAB_LANES] = (
                jnp.concatenate(parts_i, axis=0).astype(_BF16))
        if k + 2 < N_SLABS:
            expand(k + 2)
        y_ref[:, k * SLAB_CH:(k + 1) * SLAB_CH] = jnp.dot(
            xb_ref[:, k * SLAB_LANES:(k + 1) * SLAB_LANES], cmat_ref[k], preferred_element_type=_F32)

    tm = u_ref.shape[0]
    hm = tm // 2
    acc = None
    for r in range(2):
        rows = slice(r * hm, (r + 1) * hm)
        g = jax.nn.gelu(y_ref[rows, :] + d_ref[...] * upb[rows, :].astype(_F32))
        z = jnp.dot(g.astype(_BF16), gw_ref[...], preferred_element_type=_F32) + gb_ref[...]
        out = _rmsnorm(g * jax.nn.sigmoid(z), g_ref[...]).astype(_BF16)
        part = jnp.dot(permt_ref[:, rows], out, preferred_element_type=_F32)
        acc = part if acc is None else acc + part
    ys_ref[...] = acc.astype(_BF16)


def _ssm_call(u, perm, permt, bmat, lam_rows, cmat, d_skip, glu_w_bf, glu_b, norm_g):
    tm = TM_SSM
    return pl.pallas_call(
        _ssm_kernel,
        grid=(SEQ // tm,),
        in_specs=[
            pl.BlockSpec((tm, SSM_WIDTH), lambda i: (i, 0)),
            _resident((tm, tm)),
            _resident((tm, tm)),
            _resident((N_SLABS, SLAB_CH, SLAB_LANES)),
            _resident((N_SLABS, 4, SLAB_STATE)),
            _resident((N_SLABS, SLAB_LANES, SLAB_CH)),
            _resident((1, SSM_WIDTH)),
            _resident((SSM_WIDTH, SSM_WIDTH)),
            _resident((1, SSM_WIDTH)),
            _resident((1, SSM_WIDTH)),
        ],
        out_specs=pl.BlockSpec((tm, SSM_WIDTH), lambda i: (i, 0)),
        out_shape=jax.ShapeDtypeStruct((SEQ, SSM_WIDTH), _BF16),
        scratch_shapes=[
            pltpu.VMEM((N_SLABS, tm, SLAB_LANES), _F32),
            pltpu.VMEM((tm, N_SLABS * SLAB_LANES), _BF16),
            pltpu.VMEM((N_SLABS, 2, SLAB_STATE), _F32),
            pltpu.VMEM((tm, SSM_WIDTH), _F32),
        ],
        compiler_params=pltpu.CompilerParams(dimension_semantics=("arbitrary",), vmem_limit_bytes=VMEM_LIMIT),
        name="s5_mixer",
    )(u, perm, permt, bmat, lam_rows, cmat, d_skip, glu_w_bf, glu_b, norm_g)


def _s5_operands(lam_re, lam_im, log_dt, b_re, b_im, c_re, c_im):
    dt = jnp.exp(log_dt)[:, None]
    mag = jnp.exp(lam_re * dt)
    bar_re = mag * jnp.cos(lam_im * dt)
    bar_im = mag * jnp.sin(lam_im * dt)
    inv = 1.0 / (lam_re * lam_re + lam_im * lam_im)
    coef_re = ((bar_re - 1.0) * lam_re + bar_im * lam_im) * inv
    coef_im = (bar_im * lam_re - (bar_re - 1.0) * lam_im) * inv
    bbar_re = coef_re[..., None] * b_re - coef_im[..., None] * b_im
    bbar_im = coef_re[..., None] * b_im + coef_im[..., None] * b_re
    eye = jnp.eye(SLAB_GROUPS, dtype=_F32)

    def b_block(part):
        part = part.reshape(N_SLABS, SLAB_GROUPS, STATE, GROUP_CH)
        blk = jnp.einsum('kgph,gj->kghjp', part, eye)
        return blk.reshape(N_SLABS, SLAB_CH, SLAB_STATE)

    def c_block(part):
        part = part.reshape(N_SLABS, SLAB_GROUPS, GROUP_CH, STATE)
        blk = jnp.einsum('kghp,gj->kgpjh', part, eye)
        return blk.reshape(N_SLABS, SLAB_STATE, SLAB_CH)

    bmat = jnp.concatenate([b_block(bbar_re), b_block(bbar_im)], axis=2).astype(_BF16)
    cmat = jnp.concatenate([c_block(c_re), c_block(-c_im)], axis=1).astype(_BF16)
    nmag = jnp.exp(SSM_STEPS * (lam_re * dt))
    rows = [bar_re, bar_im, nmag * jnp.cos(SSM_STEPS * (lam_im * dt)), nmag * jnp.sin(SSM_STEPS * (lam_im * dt))]
    lam_rows = jnp.stack([r.reshape(N_SLABS, SLAB_STATE) for r in rows], axis=1)
    return bmat, lam_rows, cmat


def _regroup_matrices():
    q = jnp.arange(TM_SSM)
    src = (q % SSM_CHUNKS) * SSM_STEPS + q // SSM_CHUNKS
    perm = (src[:, None] == jnp.arange(TM_SSM)[None, :]).astype(_BF16)
    return perm, perm.T


def _outproj_kernel(ys_ref, yc_ref, wt_ref, wb_ref, x_ref, g_ref, b_ref, h_ref, hb_ref):
    mix = jnp.dot(ys_ref[...], wt_ref[...], preferred_element_type=_F32)
    mix = mix + jnp.dot(yc_ref[...], wb_ref[...], preferred_element_type=_F32)
    h = _layernorm(ALPHA * x_ref[...] + mix, g_ref[...], b_ref[...])
    h_ref[...] = h
    hb_ref[...] = h.astype(_BF16)


def _outproj_call(ys, yc, w_out_bf, x, ln_g, ln_b):
    tm = TM_OUT
    half = lambda j: pl.BlockSpec((SSM_WIDTH, D_MODEL), lambda i, j=j: (j, 0), pipeline_mode=pl.Buffered(1))
    return pl.pallas_call(
        _outproj_kernel,
        grid=(SEQ // tm,),
        in_specs=[
            pl.BlockSpec((tm, SSM_WIDTH), lambda i: (i, 0)),
            pl.BlockSpec((tm, CONV_WIDTH), lambda i: (i, 0)),
            half(0), half(1),
            pl.BlockSpec((tm, D_MODEL), lambda i: (i, 0)),
            _resident((1, D_MODEL)),
            _resident((1, D_MODEL)),
        ],
        out_specs=[
            pl.BlockSpec((tm, D_MODEL), lambda i: (i, 0)),
            pl.BlockSpec((tm, D_MODEL), lambda i: (i, 0)),
        ],
        out_shape=[
            jax.ShapeDtypeStruct((SEQ, D_MODEL), _F32),
            jax.ShapeDtypeStruct((SEQ, D_MODEL), _BF16),
        ],
        compiler_params=pltpu.CompilerParams(dimension_semantics=("arbitrary",), vmem_limit_bytes=VMEM_LIMIT),
        name="outproj_ln1",
    )(ys, yc, w_out_bf, w_out_bf, x, ln_g, ln_b)


def _ffn_up_kernel(hb_ref, wg_ref, wu_ref, cw_ref, cb_ref, act_ref, wgb_ref, wub_ref, halo_ref, stage_ref):
    i = pl.program_id(1)
    tm = hb_ref.shape[0]
    hm = tm // FFN_ROW_SPLIT

    @pl.when(i == 0)
    def _():
        wgb_ref[...] = wg_ref[...].astype(_BF16)
        wub_ref[...] = wu_ref[...].astype(_BF16)
        stage_ref[0:SUBLANES, :] = jnp.zeros((SUBLANES, stage_ref.shape[1]), _F32)

    @pl.when(i > 0)
    def _():
        stage_ref[0:SUBLANES, :] = halo_ref[...]

    for r in range(FFN_ROW_SPLIT):
        rows = slice(r * hm, (r + 1) * hm)
        hb = hb_ref[rows, :]
        gate = jnp.dot(hb, wgb_ref[...], preferred_element_type=_F32)
        up = jnp.dot(hb, wub_ref[...], preferred_element_type=_F32)
        stage_ref[SUBLANES + r * hm:SUBLANES + (r + 1) * hm, :] = gate
        g1 = stage_ref[SUBLANES - 1 + r * hm:SUBLANES - 1 + (r + 1) * hm, :]
        g2 = stage_ref[SUBLANES - 2 + r * hm:SUBLANES - 2 + (r + 1) * hm, :]
        a = cw_ref[0:1, :] * g2 + cw_ref[1:2, :] * g1 + cw_ref[2:3, :] * gate + cb_ref[...]
        act_ref[rows, :] = (jax.nn.silu(a) * up).astype(_BF16)
    halo_ref[...] = stage_ref[tm:tm + SUBLANES, :]


def _ffn_up_call(hb, w_gate, w_up, conv_w, conv_b):
    tm, tf = TM_FFN, TF_FFN
    return pl.pallas_call(
        _ffn_up_kernel,
        grid=(D_FF // tf, SEQ // tm),
        in_specs=[
            pl.BlockSpec((tm, D_MODEL), lambda k, i: (i, 0)),
            pl.BlockSpec((D_MODEL, tf), lambda k, i: (0, k)),
            pl.BlockSpec((D_MODEL, tf), lambda k, i: (0, k)),
            pl.BlockSpec((3, tf), lambda k, i: (0, k)),
            pl.BlockSpec((1, tf), lambda k, i: (0, k)),
        ],
        out_specs=pl.BlockSpec((tm, tf), lambda k, i: (i, k)),
        out_shape=jax.ShapeDtypeStruct((SEQ, D_FF), _BF16),
        scratch_shapes=[
            pltpu.VMEM((D_MODEL, tf), _BF16),
            pltpu.VMEM((D_MODEL, tf), _BF16),
            pltpu.VMEM((SUBLANES, tf), _F32),
            pltpu.VMEM((SUBLANES + tm, tf), _F32),
        ],
        compiler_params=pltpu.CompilerParams(dimension_semantics=("arbitrary", "arbitrary"),
                                             vmem_limit_bytes=VMEM_LIMIT),
        name="convffn_up",
    )(hb, w_gate, w_up, conv_w, conv_b)


def _ffn_down_kernel(act_ref, wd_ref, h_ref, g_ref, b_ref, o_ref):
    k = pl.program_id(1)

    @pl.when(k == 0)
    def _():
        o_ref[...] = jnp.zeros(o_ref.shape, _F32)

    o_ref[...] += jnp.dot(act_ref[...], wd_ref[...], preferred_element_type=_F32)

    @pl.when(k == pl.num_programs(1) - 1)
    def _():
        o_ref[...] = _layernorm(ALPHA * h_ref[...] + o_ref[...], g_ref[...], b_ref[...])


def _ffn_down_call(act, wd_bf, h, ln_g, ln_b):
    tm, tf = TM_FFN, TF_FFN
    return pl.pallas_call(
        _ffn_down_kernel,
        grid=(SEQ // tm, D_FF // tf),
        in_specs=[
            pl.BlockSpec((tm, tf), lambda i, k: (i, k)),
            pl.BlockSpec((tf, D_MODEL), lambda i, k: (k, 0)),
            pl.BlockSpec((tm, D_MODEL), lambda i, k: (i, 0)),
            _resident((1, D_MODEL)),
            _resident((1, D_MODEL)),
        ],
        out_specs=pl.BlockSpec((tm, D_MODEL), lambda i, k: (i, 0)),
        out_shape=jax.ShapeDtypeStruct((SEQ, D_MODEL), _F32),
        compiler_params=pltpu.CompilerParams(dimension_semantics=("arbitrary", "arbitrary"),
                                             vmem_limit_bytes=VMEM_LIMIT),
        name="convffn_down_ln2",
    )(act, wd_bf, h, ln_g, ln_b)


def kernel(x, w_in, ssm_lambda_re, ssm_lambda_im, ssm_log_dt, ssm_b_re, ssm_b_im, ssm_c_re, ssm_c_im,
           ssm_d, ssm_glu_w, ssm_glu_b, sconv_w, norm_ssm_g, norm_conv_g, w_out, ln1_g, ln1_b,
           ffn_w_gate, ffn_w_up, ffn_conv_w, ffn_conv_b, ffn_w_down, ln2_g, ln2_b):
    assert x.shape == (1, SEQ, D_MODEL) and w_in.shape[0] == DEPTH
    h = x[0]
    perm, permt = _regroup_matrices()
    for l in range(DEPTH):
        row = lambda p: p[l].reshape(1, -1)
        u, yc = _proj_call(h, w_in[l].astype(_BF16), sconv_w[l], row(norm_conv_g))
        bmat, lam_rows, cmat = _s5_operands(ssm_lambda_re[l], ssm_lambda_im[l], ssm_log_dt[l],
                                            ssm_b_re[l], ssm_b_im[l], ssm_c_re[l], ssm_c_im[l])
        ys = _ssm_call(u, perm, permt, bmat, lam_rows, cmat, row(ssm_d), ssm_glu_w[l].astype(_BF16),
                       row(ssm_glu_b), row(norm_ssm_g))
        h, hb = _outproj_call(ys, yc, w_out[l].astype(_BF16), h, row(ln1_g), row(ln1_b))
        act = _ffn_up_call(hb, ffn_w_gate[l], ffn_w_up[l], ffn_conv_w[l], row(ffn_conv_b))
        h = _ffn_down_call(act, ffn_w_down[l].astype(_BF16), h, row(ln2_g), row(ln2_b))
    return h[None]
```

```python
import jax
import jax.numpy as jnp
import numpy as np
from jax import lax
from jax.experimental import pallas as pl
from jax.experimental.pallas import tpu as pltpu

SEQ = 8192
D_MODEL = 2048
SSM_WIDTH = 1024
CONV_WIDTH = 1024
GROUP_CH = 16
GROUPS = SSM_WIDTH // GROUP_CH
STATE = 64
D_FF = 5632
DEPTH = 1
LN_EPS = 1e-5
RMS_EPS = 1e-6
ALPHA = (2.0 * DEPTH) ** 0.25

SUBLANES = 8
SLAB_GROUPS = 8
N_SLABS = GROUPS // SLAB_GROUPS
SLAB_CH = SLAB_GROUPS * GROUP_CH
SLAB_STATE = SLAB_GROUPS * STATE
SLAB_LANES = 2 * SLAB_STATE

TM_PROJ = 512
TM_SSM = 512
SSM_CHUNKS = SUBLANES
SSM_STEPS = TM_SSM // SSM_CHUNKS
TM_OUT = 512
TM_FFN = 1024
TF_FFN = 512
FFN_ROW_SPLIT = 2
VMEM_LIMIT = 56 * 1024 * 1024

_F32 = jnp.float32
_BF16 = jnp.bfloat16


def _resident(shape):
    return pl.BlockSpec(shape, lambda *_: (0,) * len(shape), pipeline_mode=pl.Buffered(1))


def _layernorm(r, g, b):
    mu = jnp.mean(r, axis=-1, keepdims=True)
    rc = r - mu
    var = jnp.mean(rc * rc, axis=-1, keepdims=True)
    return rc * lax.rsqrt(var + LN_EPS) * g + b


def _rmsnorm(y, g):
    return y * lax.rsqrt(jnp.mean(y * y, axis=-1, keepdims=True) + RMS_EPS) * g


def _causal_taps(cur, halo_ref, stage_ref, first_tile):
    tm = cur.shape[0]

    @pl.when(first_tile)
    def _():
        stage_ref[0:SUBLANES, :] = jnp.zeros((SUBLANES, cur.shape[1]), _F32)

    @pl.when(jnp.logical_not(first_tile))
    def _():
        stage_ref[0:SUBLANES, :] = halo_ref[...]

    stage_ref[SUBLANES:SUBLANES + tm, :] = cur
    halo_ref[...] = cur[tm - SUBLANES:tm, :]
    return stage_ref[SUBLANES - 1:SUBLANES - 1 + tm, :], stage_ref[SUBLANES - 2:SUBLANES - 2 + tm, :]


def _proj_kernel(x_ref, wu_ref, wb_ref, wc_ref, wv_ref, cw_ref, g_ref, u_ref, yc_ref, halo_ref, stage_ref):
    xb = x_ref[...].astype(_BF16)
    u_ref[...] = jnp.dot(xb, wu_ref[...], preferred_element_type=_F32).astype(_BF16)
    gate_c = jnp.dot(xb, wc_ref[...], preferred_element_type=_F32)
    v = jnp.dot(xb, wv_ref[...], preferred_element_type=_F32)
    cv = gate_c * v
    cv1, cv2 = _causal_taps(cv, halo_ref, stage_ref, pl.program_id(0) == 0)
    conv = cw_ref[0:1, :] * cv2 + cw_ref[1:2, :] * cv1 + cw_ref[2:3, :] * cv
    gate_b = jnp.dot(xb, wb_ref[...], preferred_element_type=_F32)
    yc_ref[...] = _rmsnorm(gate_b * conv, g_ref[...]).astype(_BF16)


def _proj_call(x, w_in_bf, sconv_w, norm_conv_g):
    tm = TM_PROJ
    wspec = lambda j: pl.BlockSpec((D_MODEL, 1024), lambda i, j=j: (0, j), pipeline_mode=pl.Buffered(1))
    return pl.pallas_call(
        _proj_kernel,
        grid=(SEQ // tm,),
        in_specs=[
            pl.BlockSpec((tm, D_MODEL), lambda i: (i, 0)),
            wspec(0), wspec(1), wspec(2), wspec(3),
            _resident((3, CONV_WIDTH)),
            _resident((1, CONV_WIDTH)),
        ],
        out_specs=[
            pl.BlockSpec((tm, SSM_WIDTH), lambda i: (i, 0)),
            pl.BlockSpec((tm, CONV_WIDTH), lambda i: (i, 0)),
        ],
        out_shape=[
            jax.ShapeDtypeStruct((SEQ, SSM_WIDTH), _BF16),
            jax.ShapeDtypeStruct((SEQ, CONV_WIDTH), _BF16),
        ],
        scratch_shapes=[
            pltpu.VMEM((SUBLANES, CONV_WIDTH), _F32),
            pltpu.VMEM((SUBLANES + tm, CONV_WIDTH), _F32),
        ],
        compiler_params=pltpu.CompilerParams(dimension_semantics=("arbitrary",), vmem_limit_bytes=VMEM_LIMIT),
        name="proj_shortconv",
    )(x, w_in_bf, w_in_bf, w_in_bf, w_in_bf, sconv_w, norm_conv_g)


def _cmul_add(ar, ai, br, bi, cr, ci):
    return ar * br - ai * bi + cr, ar * bi + ai * br + ci


def _ssm_kernel(u_ref, perm_ref, permt_ref, bmat_ref, lam_ref, cmat_ref, d_ref, gw_ref, gb_ref, g_ref, ys_ref,
                bu_ref, xb_ref, state_ref, y_ref):
    nc, ns = SSM_CHUNKS, SSM_STEPS

    @pl.when(pl.program_id(0) == 0)
    def _():
        state_ref[...] = jnp.zeros(state_ref.shape, _F32)

    upb = jnp.dot(perm_ref[...], u_ref[...], preferred_element_type=_F32).astype(_BF16)

    def expand(k):
        bu_ref[k] = jnp.dot(upb[:, k * SLAB_CH:(k + 1) * SLAB_CH], bmat_ref[k], preferred_element_type=_F32)

    expand(0)
    expand(1)
    re = slice(0, SLAB_STATE)
    im = slice(SLAB_STATE, SLAB_LANES)
    for k in range(N_SLABS):
        lr = jnp.broadcast_to(lam_ref[k, 0:1, :], (nc, SLAB_STATE))
        li = jnp.broadcast_to(lam_ref[k, 1:2, :], (nc, SLAB_STATE))

        xr = jnp.zeros((nc, SLAB_STATE), _F32)
        xi = jnp.zeros((nc, SLAB_STATE), _F32)
        for s in range(ns):
            rows = slice(s * nc, (s + 1) * nc)
            xr, xi = _cmul_add(lr, li, xr, xi, bu_ref[k, rows, re], bu_ref[k, rows, im])

        tr = lam_ref[k, 2:3, :]
        ti = lam_ref[k, 3:4, :]
        cr = state_ref[k, 0:1, :]
        ci = state_ref[k, 1:2, :]
        starts_r, starts_i = [], []
        for c in range(nc):
            starts_r.append(cr)
            starts_i.append(ci)
            cr, ci = _cmul_add(tr, ti, cr, ci, xr[c:c + 1, :], xi[c:c + 1, :])
        state_ref[k, 0:1, :] = cr
        state_ref[k, 1:2, :] = ci
        xr = jnp.concatenate(starts_r, axis=0)
        xi = jnp.concatenate(starts_i, axis=0)

        for s in range(0, ns, 2):
            parts_r, parts_i = [], []
            for q in (s, s + 1):
                rows = slice(q * nc, (q + 1) * nc)
                xr, xi = _cmul_add(lr, li, xr, xi, bu_ref[k, rows, re], bu_ref[k, rows, im])
                parts_r.append(xr)
                parts_i.append(xi)
            rows2 = slice(s * nc, (s + 2) * nc)
            xb_ref[rows2, k * SLAB_LANES:k * SLAB_LANES + SLAB_STATE] = jnp.concatenate(parts_r, axis=0).astype(_BF16)
            xb_ref[rows2, k * SLAB_LANES + SLAB_STATE:(k + 1) * SLAB_LANES] = (
                jnp.concatenate(parts_i, axis=0).astype(_BF16))
        if k + 2 < N_SLABS:
            expand(k + 2)
        y_ref[:, k * SLAB_CH:(k + 1) * SLAB_CH] = jnp.dot(
            xb_ref[:, k * SLAB_LANES:(k + 1) * SLAB_LANES], cmat_ref[k], preferred_element_type=_F32)

    tm = u_ref.shape[0]
    hm = tm // 2
    acc = None
    for r in range(2):
        rows = slice(r * hm, (r + 1) * hm)
        g = jax.nn.gelu(y_ref[rows, :] + d_ref[...] * upb[rows, :].astype(_F32))
        z = jnp.dot(g.astype(_BF16), gw_ref[...], preferred_element_type=_F32) + gb_ref[...]
        out = _rmsnorm(g * jax.nn.sigmoid(z), g_ref[...]).astype(_BF16)
        part = jnp.dot(permt_ref[:, rows], out, preferred_element_type=_F32)
        acc = part if acc is None else acc + part
    ys_ref[...] = acc.astype(_BF16)


def _ssm_call(u, perm, permt, bmat, lam_rows, cmat, d_skip, glu_w_bf, glu_b, norm_g):
    tm = TM_SSM
    return pl.pallas_call(
        _ssm_kernel,
        grid=(SEQ // tm,),
        in_specs=[
            pl.BlockSpec((tm, SSM_WIDTH), lambda i: (i, 0)),
            _resident((tm, tm)),
            _resident((tm, tm)),
            _resident((N_SLABS, SLAB_CH, SLAB_LANES)),
            _resident((N_SLABS, 4, SLAB_STATE)),
            _resident((N_SLABS, SLAB_LANES, SLAB_CH)),
            _resident((1, SSM_WIDTH)),
            _resident((SSM_WIDTH, SSM_WIDTH)),
            _resident((1, SSM_WIDTH)),
            _resident((1, SSM_WIDTH)),
        ],
        out_specs=pl.BlockSpec((tm, SSM_WIDTH), lambda i: (i, 0)),
        out_shape=jax.ShapeDtypeStruct((SEQ, SSM_WIDTH), _BF16),
        scratch_shapes=[
            pltpu.VMEM((N_SLABS, tm, SLAB_LANES), _F32),
            pltpu.VMEM((tm, N_SLABS * SLAB_LANES), _BF16),
            pltpu.VMEM((N_SLABS, 2, SLAB_STATE), _F32),
            pltpu.VMEM((tm, SSM_WIDTH), _F32),
        ],
        compiler_params=pltpu.CompilerParams(dimension_semantics=("arbitrary",), vmem_limit_bytes=VMEM_LIMIT),
        name="s5_mixer",
    )(u, perm, permt, bmat, lam_rows, cmat, d_skip, glu_w_bf, glu_b, norm_g)


def _s5_operands(lam_re, lam_im, log_dt, b_re, b_im, c_re, c_im):
    dt = jnp.exp(log_dt)[:, None]
    mag = jnp.exp(lam_re * dt)
    bar_re = mag * jnp.cos(lam_im * dt)
    bar_im = mag * jnp.sin(lam_im * dt)
    inv = 1.0 / (lam_re * lam_re + lam_im * lam_im)
    coef_re = ((bar_re - 1.0) * lam_re + bar_im * lam_im) * inv
    coef_im = (bar_im * lam_re - (bar_re - 1.0) * lam_im) * inv
    bbar_re = coef_re[..., None] * b_re - coef_im[..., None] * b_im
    bbar_im = coef_re[..., None] * b_im + coef_im[..., None] * b_re
    eye = np.eye(SLAB_GROUPS, dtype=np.float32)
    bb = jnp.stack([bbar_re, bbar_im]).reshape(2, N_SLABS, SLAB_GROUPS, STATE, GROUP_CH)
    bmat = (bb.transpose(1, 2, 4, 0, 3)[:, :, :, :, None, :] * eye[None, :, None, None, :, None])
    bmat = bmat.reshape(N_SLABS, SLAB_CH, SLAB_LANES).astype(_BF16)
    cc = jnp.stack([c_re, -c_im]).reshape(2, N_SLABS, SLAB_GROUPS, GROUP_CH, STATE)
    cmat = (cc.transpose(1, 0, 2, 4, 3)[:, :, :, :, None, :] * eye[None, None, :, None, :, None])
    cmat = cmat.reshape(N_SLABS, SLAB_LANES, SLAB_CH).astype(_BF16)
    nmag = jnp.exp(SSM_STEPS * (lam_re * dt))
    rows = [bar_re, bar_im, nmag * jnp.cos(SSM_STEPS * (lam_im * dt)), nmag * jnp.sin(SSM_STEPS * (lam_im * dt))]
    lam_rows = jnp.stack([r.reshape(N_SLABS, SLAB_STATE) for r in rows], axis=1)
    return bmat, lam_rows, cmat


def _regroup_matrices():
    q = np.arange(TM_SSM)
    src = (q % SSM_CHUNKS) * SSM_STEPS + q // SSM_CHUNKS
    perm = (src[:, None] == q[None, :]).astype(np.float32)
    return jnp.asarray(perm, _BF16), jnp.asarray(perm.T, _BF16)


def _outproj_kernel(ys_ref, yc_ref, wt_ref, wb_ref, x_ref, g_ref, b_ref, h_ref, hb_ref):
    mix = jnp.dot(ys_ref[...], wt_ref[...], preferred_element_type=_F32)
    mix = mix + jnp.dot(yc_ref[...], wb_ref[...], preferred_element_type=_F32)
    h = _layernorm(ALPHA * x_ref[...] + mix, g_ref[...], b_ref[...])
    h_ref[...] = h
    hb_ref[...] = h.astype(_BF16)


def _outproj_call(ys, yc, w_out_bf, x, ln_g, ln_b):
    tm = TM_OUT
    half = lambda j: pl.BlockSpec((SSM_WIDTH, D_MODEL), lambda i, j=j: (j, 0), pipeline_mode=pl.Buffered(1))
    return pl.pallas_call(
        _outproj_kernel,
        grid=(SEQ // tm,),
        in_specs=[
            pl.BlockSpec((tm, SSM_WIDTH), lambda i: (i, 0)),
            pl.BlockSpec((tm, CONV_WIDTH), lambda i: (i, 0)),
            half(0), half(1),
            pl.BlockSpec((tm, D_MODEL), lambda i: (i, 0)),
            _resident((1, D_MODEL)),
            _resident((1, D_MODEL)),
        ],
        out_specs=[
            pl.BlockSpec((tm, D_MODEL), lambda i: (i, 0)),
            pl.BlockSpec((tm, D_MODEL), lambda i: (i, 0)),
        ],
        out_shape=[
            jax.ShapeDtypeStruct((SEQ, D_MODEL), _F32),
            jax.ShapeDtypeStruct((SEQ, D_MODEL), _BF16),
        ],
        compiler_params=pltpu.CompilerParams(dimension_semantics=("arbitrary",), vmem_limit_bytes=VMEM_LIMIT),
        name="outproj_ln1",
    )(ys, yc, w_out_bf, w_out_bf, x, ln_g, ln_b)


def _ffn_up_kernel(hb_ref, wg_ref, wu_ref, cw_ref, cb_ref, act_ref, wgb_ref, wub_ref, halo_ref, stage_ref):
    i = pl.program_id(1)
    tm = hb_ref.shape[0]
    hm = tm // FFN_ROW_SPLIT

    @pl.when(i == 0)
    def _():
        wgb_ref[...] = wg_ref[...].astype(_BF16)
        wub_ref[...] = wu_ref[...].astype(_BF16)
        stage_ref[0:SUBLANES, :] = jnp.zeros((SUBLANES, stage_ref.shape[1]), _F32)

    @pl.when(i > 0)
    def _():
        stage_ref[0:SUBLANES, :] = halo_ref[...]

    for r in range(FFN_ROW_SPLIT):
        rows = slice(r * hm, (r + 1) * hm)
        hb = hb_ref[rows, :]
        gate = jnp.dot(hb, wgb_ref[...], preferred_element_type=_F32)
        up = jnp.dot(hb, wub_ref[...], preferred_element_type=_F32)
        stage_ref[SUBLANES + r * hm:SUBLANES + (r + 1) * hm, :] = gate
        g1 = stage_ref[SUBLANES - 1 + r * hm:SUBLANES - 1 + (r + 1) * hm, :]
        g2 = stage_ref[SUBLANES - 2 + r * hm:SUBLANES - 2 + (r + 1) * hm, :]
        a = cw_ref[0:1, :] * g2 + cw_ref[1:2, :] * g1 + cw_ref[2:3, :] * gate + cb_ref[...]
        act_ref[rows, :] = (jax.nn.silu(a) * up).astype(_BF16)
    halo_ref[...] = stage_ref[tm:tm + SUBLANES, :]


def _ffn_up_call(hb, w_gate, w_up, conv_w, conv_b):
    tm, tf = TM_FFN, TF_FFN
    return pl.pallas_call(
        _ffn_up_kernel,
        grid=(D_FF // tf, SEQ // tm),
        in_specs=[
            pl.BlockSpec((tm, D_MODEL), lambda k, i: (i, 0)),
            pl.BlockSpec((D_MODEL, tf), lambda k, i: (0, k)),
            pl.BlockSpec((D_MODEL, tf), lambda k, i: (0, k)),
            pl.BlockSpec((3, tf), lambda k, i: (0, k)),
            pl.BlockSpec((1, tf), lambda k, i: (0, k)),
        ],
        out_specs=pl.BlockSpec((tm, tf), lambda k, i: (i, k)),
        out_shape=jax.ShapeDtypeStruct((SEQ, D_FF), _BF16),
        scratch_shapes=[
            pltpu.VMEM((D_MODEL, tf), _BF16),
            pltpu.VMEM((D_MODEL, tf), _BF16),
            pltpu.VMEM((SUBLANES, tf), _F32),
            pltpu.VMEM((SUBLANES + tm, tf), _F32),
        ],
        compiler_params=pltpu.CompilerParams(dimension_semantics=("arbitrary", "arbitrary"),
                                             vmem_limit_bytes=VMEM_LIMIT),
        name="convffn_up",
    )(hb, w_gate, w_up, conv_w, conv_b)


def _ffn_down_kernel(act_ref, wd_ref, h_ref, g_ref, b_ref, o_ref):
    k = pl.program_id(1)

    @pl.when(k == 0)
    def _():
        o_ref[...] = jnp.zeros(o_ref.shape, _F32)

    o_ref[...] += jnp.dot(act_ref[...], wd_ref[...].astype(_BF16), preferred_element_type=_F32)

    @pl.when(k == pl.num_programs(1) - 1)
    def _():
        o_ref[...] = _layernorm(ALPHA * h_ref[...] + o_ref[...], g_ref[...], b_ref[...])


def _ffn_down_call(act, w_down, h, ln_g, ln_b):
    tm, tf = TM_FFN, TF_FFN
    return pl.pallas_call(
        _ffn_down_kernel,
        grid=(SEQ // tm, D_FF // tf),
        in_specs=[
            pl.BlockSpec((tm, tf), lambda i, k: (i, k)),
            pl.BlockSpec((tf, D_MODEL), lambda i, k: (k, 0)),
            pl.BlockSpec((tm, D_MODEL), lambda i, k: (i, 0)),
            _resident((1, D_MODEL)),
            _resident((1, D_MODEL)),
        ],
        out_specs=pl.BlockSpec((tm, D_MODEL), lambda i, k: (i, 0)),
        out_shape=jax.ShapeDtypeStruct((SEQ, D_MODEL), _F32),
        compiler_params=pltpu.CompilerParams(dimension_semantics=("arbitrary", "arbitrary"),
                                             vmem_limit_bytes=VMEM_LIMIT),
        name="convffn_down_ln2",
    )(act, w_down, h, ln_g, ln_b)


def kernel(x, w_in, ssm_lambda_re, ssm_lambda_im, ssm_log_dt, ssm_b_re, ssm_b_im, ssm_c_re, ssm_c_im,
           ssm_d, ssm_glu_w, ssm_glu_b, sconv_w, norm_ssm_g, norm_conv_g, w_out, ln1_g, ln1_b,
           ffn_w_gate, ffn_w_up, ffn_conv_w, ffn_conv_b, ffn_w_down, ln2_g, ln2_b):
    assert x.shape == (1, SEQ, D_MODEL) and w_in.shape[0] == DEPTH
    h = x[0]
    perm, permt = _regroup_matrices()
    for l in range(DEPTH):
        row = lambda p: p[l].reshape(1, -1)
        u, yc = _proj_call(h, w_in[l].astype(_BF16), sconv_w[l], row(norm_conv_g))
        bmat, lam_rows, cmat = _s5_operands(ssm_lambda_re[l], ssm_lambda_im[l], ssm_log_dt[l],
                                            ssm_b_re[l], ssm_b_im[l], ssm_c_re[l], ssm_c_im[l])
        ys = _ssm_call(u, perm, permt, bmat, lam_rows, cmat, row(ssm_d), ssm_glu_w[l].astype(_BF16),
                       row(ssm_glu_b), row(norm_ssm_g))
        h, hb = _outproj_call(ys, yc, w_out[l].astype(_BF16), h, row(ln1_g), row(ln1_b))
        act = _ffn_up_call(hb, ffn_w_gate[l], ffn_w_up[l], ffn_conv_w[l], row(ffn_conv_b))
        h = _ffn_down_call(act, ffn_w_down[l], h, row(ln2_g), row(ln2_b))
    return h[None]
```

```python
import jax
import jax.numpy as jnp
from jax import lax
from jax.experimental import pallas as pl
from jax.experimental.pallas import tpu as pltpu

SEQ = 8192
D_MODEL = 2048
SSM_WIDTH = 1024
CONV_WIDTH = 1024
GROUP_CH = 16
GROUPS = SSM_WIDTH // GROUP_CH
STATE = 64
D_FF = 5632
DEPTH = 1
LN_EPS = 1e-5
RMS_EPS = 1e-6
ALPHA = (2.0 * DEPTH) ** 0.25

SUBLANES = 8
SLAB_GROUPS = 8
N_SLABS = GROUPS // SLAB_GROUPS
SLAB_CH = SLAB_GROUPS * GROUP_CH
SLAB_STATE = SLAB_GROUPS * STATE
SLAB_LANES = 2 * SLAB_STATE

TM_MIX = 512
SSM_CHUNKS = SUBLANES
SSM_STEPS = TM_MIX // SSM_CHUNKS
MIX_ROW_SPLIT = 2
TM_FFN = 1024
TF_FFN = 512
FFN_ROW_SPLIT = 2
VMEM_LIMIT = 56 * 1024 * 1024

_F32 = jnp.float32
_BF16 = jnp.bfloat16


def _resident(shape):
    return pl.BlockSpec(shape, lambda *_: (0,) * len(shape), pipeline_mode=pl.Buffered(1))


def _layernorm(r, g, b):
    mu = jnp.mean(r, axis=-1, keepdims=True)
    rc = r - mu
    var = jnp.mean(rc * rc, axis=-1, keepdims=True)
    return rc * lax.rsqrt(var + LN_EPS) * g + b


def _rmsnorm(y, g):
    return y * lax.rsqrt(jnp.mean(y * y, axis=-1, keepdims=True) + RMS_EPS) * g


def _causal_taps(cur, halo_ref, stage_ref, first_tile):
    tm = cur.shape[0]

    @pl.when(first_tile)
    def _():
        stage_ref[0:SUBLANES, :] = jnp.zeros((SUBLANES, cur.shape[1]), _F32)

    @pl.when(jnp.logical_not(first_tile))
    def _():
        stage_ref[0:SUBLANES, :] = halo_ref[...]

    stage_ref[SUBLANES:SUBLANES + tm, :] = cur
    halo_ref[...] = cur[tm - SUBLANES:tm, :]
    return stage_ref[SUBLANES - 1:SUBLANES - 1 + tm, :], stage_ref[SUBLANES - 2:SUBLANES - 2 + tm, :]


def _chunk_rows(c):
    return pl.ds(c, SSM_STEPS, stride=SSM_CHUNKS)


def _proj_kernel(x_ref, wu_ref, wb_ref, wc_ref, wv_ref, cw_ref, g_ref, u_ref, yc_ref, halo_ref, stage_ref):
    xb = x_ref[...].astype(_BF16)
    u = jnp.dot(xb, wu_ref[...], preferred_element_type=_F32)
    for k in range(N_SLABS):
        for c in range(SSM_CHUNKS):
            u_ref[k, _chunk_rows(c), :] = u[c * SSM_STEPS:(c + 1) * SSM_STEPS, k * SLAB_CH:(k + 1) * SLAB_CH]
    gate_c = jnp.dot(xb, wc_ref[...], preferred_element_type=_F32)
    v = jnp.dot(xb, wv_ref[...], preferred_element_type=_F32)
    cv = gate_c * v
    cv1, cv2 = _causal_taps(cv, halo_ref, stage_ref, pl.program_id(0) == 0)
    conv = cw_ref[0:1, :] * cv2 + cw_ref[1:2, :] * cv1 + cw_ref[2:3, :] * cv
    gate_b = jnp.dot(xb, wb_ref[...], preferred_element_type=_F32)
    yc_ref[...] = _rmsnorm(gate_b * conv, g_ref[...]).astype(_BF16)


def _proj_call(x, w_in_bf, sconv_w, norm_conv_g):
    tm = TM_MIX
    wspec = lambda j: pl.BlockSpec((D_MODEL, 1024), lambda i, j=j: (0, j), pipeline_mode=pl.Buffered(1))
    return pl.pallas_call(
        _proj_kernel,
        grid=(SEQ // tm,),
        in_specs=[
            pl.BlockSpec((tm, D_MODEL), lambda i: (i, 0)),
            wspec(0), wspec(1), wspec(2), wspec(3),
            _resident((3, CONV_WIDTH)),
            _resident((1, CONV_WIDTH)),
        ],
        out_specs=[
            pl.BlockSpec((N_SLABS, tm, SLAB_CH), lambda i: (0, i, 0)),
            pl.BlockSpec((tm, CONV_WIDTH), lambda i: (i, 0)),
        ],
        out_shape=[
            jax.ShapeDtypeStruct((N_SLABS, SEQ, SLAB_CH), _F32),
            jax.ShapeDtypeStruct((SEQ, CONV_WIDTH), _BF16),
        ],
        scratch_shapes=[
            pltpu.VMEM((SUBLANES, CONV_WIDTH), _F32),
            pltpu.VMEM((SUBLANES + tm, CONV_WIDTH), _F32),
        ],
        compiler_params=pltpu.CompilerParams(dimension_semantics=("arbitrary",), vmem_limit_bytes=VMEM_LIMIT),
        name="proj_shortconv",
    )(x, w_in_bf, w_in_bf, w_in_bf, w_in_bf, sconv_w, norm_conv_g)


def _cmul_add(ar, ai, br, bi, cr, ci):
    return ar * br - ai * bi + cr, ar * bi + ai * br + ci


def _ssm_kernel(u_ref, bmat_ref, lam_ref, cmat_ref, d_ref, y_ref, bu_ref, xb_ref, state_ref):
    nc, ns = SSM_CHUNKS, SSM_STEPS

    @pl.when(pl.program_id(0) == 0)
    def _():
        state_ref[...] = jnp.zeros(state_ref.shape, _F32)

    def expand(k):
        bu_ref[k] = jnp.dot(u_ref[k].astype(_BF16), bmat_ref[k], preferred_element_type=_F32)

    expand(0)
    expand(1)
    re = slice(0, SLAB_STATE)
    im = slice(SLAB_STATE, SLAB_LANES)
    for k in range(N_SLABS):
        lr = jnp.broadcast_to(lam_ref[k, 0:1, :], (nc, SLAB_STATE))
        li = jnp.broadcast_to(lam_ref[k, 1:2, :], (nc, SLAB_STATE))

        xr = jnp.zeros((nc, SLAB_STATE), _F32)
        xi = jnp.zeros((nc, SLAB_STATE), _F32)
        for s in range(ns):
            rows = slice(s * nc, (s + 1) * nc)
            xr, xi = _cmul_add(lr, li, xr, xi, bu_ref[k, rows, re], bu_ref[k, rows, im])

        tr = lam_ref[k, 2:3, :]
        ti = lam_ref[k, 3:4, :]
        cr = state_ref[k, 0:1, :]
        ci = state_ref[k, 1:2, :]
        starts_r, starts_i = [], []
        for c in range(nc):
            starts_r.append(cr)
            starts_i.append(ci)
            cr, ci = _cmul_add(tr, ti, cr, ci, xr[c:c + 1, :], xi[c:c + 1, :])
        state_ref[k, 0:1, :] = cr
        state_ref[k, 1:2, :] = ci
        xr = jnp.concatenate(starts_r, axis=0)
        xi = jnp.concatenate(starts_i, axis=0)

        for s in range(0, ns, 2):
            parts_r, parts_i = [], []
            for q in (s, s + 1):
                rows = slice(q * nc, (q + 1) * nc)
                xr, xi = _cmul_add(lr, li, xr, xi, bu_ref[k, rows, re], bu_ref[k, rows, im])
                parts_r.append(xr)
                parts_i.append(xi)
            rows2 = slice(s * nc, (s + 2) * nc)
            xb_ref[k, rows2, re] = jnp.concatenate(parts_r, axis=0).astype(_BF16)
            xb_ref[k, rows2, im] = jnp.concatenate(parts_i, axis=0).astype(_BF16)
        if k + 2 < N_SLABS:
            expand(k + 2)
        y_ref[k] = (jnp.dot(xb_ref[k], cmat_ref[k], preferred_element_type=_F32)
                    + d_ref[k] * u_ref[k])


def _ssm_call(u, bmat, lam_rows, cmat, d_skip):
    tm = TM_MIX
    slab_rows = pl.BlockSpec((N_SLABS, tm, SLAB_CH), lambda i: (0, i, 0))
    return pl.pallas_call(
        _ssm_kernel,
        grid=(SEQ // tm,),
        in_specs=[
            slab_rows,
            _resident((N_SLABS, SLAB_CH, SLAB_LANES)),
            _resident((N_SLABS, 4, SLAB_STATE)),
            _resident((N_SLABS, SLAB_LANES, SLAB_CH)),
            _resident((N_SLABS, 1, SLAB_CH)),
        ],
        out_specs=slab_rows,
        out_shape=jax.ShapeDtypeStruct((N_SLABS, SEQ, SLAB_CH), _F32),
        scratch_shapes=[
            pltpu.VMEM((N_SLABS, tm, SLAB_LANES), _F32),
            pltpu.VMEM((N_SLABS, tm, SLAB_LANES), _BF16),
            pltpu.VMEM((N_SLABS, 2, SLAB_STATE), _F32),
        ],
        compiler_params=pltpu.CompilerParams(dimension_semantics=("arbitrary",), vmem_limit_bytes=VMEM_LIMIT),
        name="s5_core",
    )(u, bmat, lam_rows, cmat, d_skip)


def _s5_operands(lam_re, lam_im, log_dt, b_re, b_im, c_re, c_im):
    dt = jnp.exp(log_dt)[:, None]
    mag = jnp.exp(lam_re * dt)
    bar_re = mag * jnp.cos(lam_im * dt)
    bar_im = mag * jnp.sin(lam_im * dt)
    inv = 1.0 / (lam_re * lam_re + lam_im * lam_im)
    coef_re = ((bar_re - 1.0) * lam_re + bar_im * lam_im) * inv
    coef_im = (bar_im * lam_re - (bar_re - 1.0) * lam_im) * inv
    bbar_re = coef_re[..., None] * b_re - coef_im[..., None] * b_im
    bbar_im = coef_re[..., None] * b_im + coef_im[..., None] * b_re
    eye = jnp.eye(SLAB_GROUPS, dtype=_F32)

    def b_block(part):
        part = part.reshape(N_SLABS, SLAB_GROUPS, STATE, GROUP_CH)
        blk = jnp.einsum('kgph,gj->kghjp', part, eye)
        return blk.reshape(N_SLABS, SLAB_CH, SLAB_STATE)

    def c_block(part):
        part = part.reshape(N_SLABS, SLAB_GROUPS, GROUP_CH, STATE)
        blk = jnp.einsum('kghp,gj->kgpjh', part, eye)
        return blk.reshape(N_SLABS, SLAB_STATE, SLAB_CH)

    bmat = jnp.concatenate([b_block(bbar_re), b_block(bbar_im)], axis=2).astype(_BF16)
    cmat = jnp.concatenate([c_block(c_re), c_block(-c_im)], axis=1).astype(_BF16)
    nmag = jnp.exp(SSM_STEPS * (lam_re * dt))
    rows = [bar_re, bar_im, nmag * jnp.cos(SSM_STEPS * (lam_im * dt)), nmag * jnp.sin(SSM_STEPS * (lam_im * dt))]
    lam_rows = jnp.stack([r.reshape(N_SLABS, SLAB_STATE) for r in rows], axis=1)
    return bmat, lam_rows, cmat


def _outproj_kernel(y_ref, yc_ref, gw_ref, gb_ref, ng_ref, wt_ref, wb_ref, x_ref, g_ref, b_ref, h_ref, hb_ref):
    tm = x_ref.shape[0]
    hm = tm // MIX_ROW_SPLIT
    chunks_per_part = SSM_CHUNKS // MIX_ROW_SPLIT
    for r in range(MIX_ROW_SPLIT):
        rows = slice(r * hm, (r + 1) * hm)
        y = jnp.concatenate(
            [jnp.concatenate([y_ref[k, _chunk_rows(c), :] for k in range(N_SLABS)], axis=1)
             for c in range(r * chunks_per_part, (r + 1) * chunks_per_part)], axis=0)
        gl = jax.nn.gelu(y)
        z = jnp.dot(gl.astype(_BF16), gw_ref[...], preferred_element_type=_F32) + gb_ref[...]
        ys = _rmsnorm(gl * jax.nn.sigmoid(z), ng_ref[...]).astype(_BF16)
        mix = jnp.dot(ys, wt_ref[...], preferred_element_type=_F32)
        mix = mix + jnp.dot(yc_ref[rows, :], wb_ref[...], preferred_element_type=_F32)
        h = _layernorm(ALPHA * x_ref[rows, :] + mix, g_ref[...], b_ref[...])
        h_ref[rows, :] = h
        hb_ref[rows, :] = h.astype(_BF16)


def _outproj_call(y, yc, glu_w_bf, glu_b, norm_g, w_out_bf, x, ln_g, ln_b):
    tm = TM_MIX
    half = lambda j: pl.BlockSpec((SSM_WIDTH, D_MODEL), lambda i, j=j: (j, 0), pipeline_mode=pl.Buffered(1))
    return pl.pallas_call(
        _outproj_kernel,
        grid=(SEQ // tm,),
        in_specs=[
            pl.BlockSpec((N_SLABS, tm, SLAB_CH), lambda i: (0, i, 0)),
            pl.BlockSpec((tm, CONV_WIDTH), lambda i: (i, 0)),
            _resident((SSM_WIDTH, SSM_WIDTH)),
            _resident((1, SSM_WIDTH)),
            _resident((1, SSM_WIDTH)),
            half(0), half(1),
            pl.BlockSpec((tm, D_MODEL), lambda i: (i, 0)),
            _resident((1, D_MODEL)),
            _resident((1, D_MODEL)),
        ],
        out_specs=[
            pl.BlockSpec((tm, D_MODEL), lambda i: (i, 0)),
            pl.BlockSpec((tm, D_MODEL), lambda i: (i, 0)),
        ],
        out_shape=[
            jax.ShapeDtypeStruct((SEQ, D_MODEL), _F32),
            jax.ShapeDtypeStruct((SEQ, D_MODEL), _BF16),
        ],
        compiler_params=pltpu.CompilerParams(dimension_semantics=("arbitrary",), vmem_limit_bytes=VMEM_LIMIT),
        name="glu_outproj_ln1",
    )(y, yc, glu_w_bf, glu_b, norm_g, w_out_bf, w_out_bf, x, ln_g, ln_b)


def _ffn_up_kernel(hb_ref, wg_ref, wu_ref, cw_ref, cb_ref, act_ref, wgb_ref, wub_ref, halo_ref, stage_ref):
    i = pl.program_id(1)
    tm = hb_ref.shape[0]
    hm = tm // FFN_ROW_SPLIT

    @pl.when(i == 0)
    def _():
        wgb_ref[...] = wg_ref[...].astype(_BF16)
        wub_ref[...] = wu_ref[...].astype(_BF16)
        stage_ref[0:SUBLANES, :] = jnp.zeros((SUBLANES, stage_ref.shape[1]), _F32)

    @pl.when(i > 0)
    def _():
        stage_ref[0:SUBLANES, :] = halo_ref[...]

    for r in range(FFN_ROW_SPLIT):
        rows = slice(r * hm, (r + 1) * hm)
        hb = hb_ref[rows, :]
        gate = jnp.dot(hb, wgb_ref[...], preferred_element_type=_F32)
        up = jnp.dot(hb, wub_ref[...], preferred_element_type=_F32)
        stage_ref[SUBLANES + r * hm:SUBLANES + (r + 1) * hm, :] = gate
        g1 = stage_ref[SUBLANES - 1 + r * hm:SUBLANES - 1 + (r + 1) * hm, :]
        g2 = stage_ref[SUBLANES - 2 + r * hm:SUBLANES - 2 + (r + 1) * hm, :]
        a = cw_ref[0:1, :] * g2 + cw_ref[1:2, :] * g1 + cw_ref[2:3, :] * gate + cb_ref[...]
        act_ref[rows, :] = (jax.nn.silu(a) * up).astype(_BF16)
    halo_ref[...] = stage_ref[tm:tm + SUBLANES, :]


def _ffn_up_call(hb, w_gate, w_up, conv_w, conv_b):
    tm, tf = TM_FFN, TF_FFN
    return pl.pallas_call(
        _ffn_up_kernel,
        grid=(D_FF // tf, SEQ // tm),
        in_specs=[
            pl.BlockSpec((tm, D_MODEL), lambda k, i: (i, 0)),
            pl.BlockSpec((D_MODEL, tf), lambda k, i: (0, k)),
            pl.BlockSpec((D_MODEL, tf), lambda k, i: (0, k)),
            pl.BlockSpec((3, tf), lambda k, i: (0, k)),
            pl.BlockSpec((1, tf), lambda k, i: (0, k)),
        ],
        out_specs=pl.BlockSpec((tm, tf), lambda k, i: (i, k)),
        out_shape=jax.ShapeDtypeStruct((SEQ, D_FF), _BF16),
        scratch_shapes=[
            pltpu.VMEM((D_MODEL, tf), _BF16),
            pltpu.VMEM((D_MODEL, tf), _BF16),
            pltpu.VMEM((SUBLANES, tf), _F32),
            pltpu.VMEM((SUBLANES + tm, tf), _F32),
        ],
        compiler_params=pltpu.CompilerParams(dimension_semantics=("arbitrary", "arbitrary"),
                                             vmem_limit_bytes=VMEM_LIMIT),
        name="convffn_up",
    )(hb, w_gate, w_up, conv_w, conv_b)


def _ffn_down_kernel(act_ref, wd_ref, h_ref, g_ref, b_ref, o_ref):
    k = pl.program_id(1)

    @pl.when(k == 0)
    def _():
        o_ref[...] = jnp.zeros(o_ref.shape, _F32)

    o_ref[...] += jnp.dot(act_ref[...], wd_ref[...], preferred_element_type=_F32)

    @pl.when(k == pl.num_programs(1) - 1)
    def _():
        o_ref[...] = _layernorm(ALPHA * h_ref[...] + o_ref[...], g_ref[...], b_ref[...])


def _ffn_down_call(act, wd_bf, h, ln_g, ln_b):
    tm, tf = TM_FFN, TF_FFN
    return pl.pallas_call(
        _ffn_down_kernel,
        grid=(SEQ // tm, D_FF // tf),
        in_specs=[
            pl.BlockSpec((tm, tf), lambda i, k: (i, k)),
            pl.BlockSpec((tf, D_MODEL), lambda i, k: (k, 0)),
            pl.BlockSpec((tm, D_MODEL), lambda i, k: (i, 0)),
            _resident((1, D_MODEL)),
            _resident((1, D_MODEL)),
        ],
        out_specs=pl.BlockSpec((tm, D_MODEL), lambda i, k: (i, 0)),
        out_shape=jax.ShapeDtypeStruct((SEQ, D_MODEL), _F32),
        compiler_params=pltpu.CompilerParams(dimension_semantics=("arbitrary", "arbitrary"),
                                             vmem_limit_bytes=VMEM_LIMIT),
        name="convffn_down_ln2",
    )(act, wd_bf, h, ln_g, ln_b)


def kernel(x, w_in, ssm_lambda_re, ssm_lambda_im, ssm_log_dt, ssm_b_re, ssm_b_im, ssm_c_re, ssm_c_im,
           ssm_d, ssm_glu_w, ssm_glu_b, sconv_w, norm_ssm_g, norm_conv_g, w_out, ln1_g, ln1_b,
           ffn_w_gate, ffn_w_up, ffn_conv_w, ffn_conv_b, ffn_w_down, ln2_g, ln2_b):
    assert x.shape == (1, SEQ, D_MODEL) and w_in.shape[0] == DEPTH
    h = x[0]
    for l in range(DEPTH):
        row = lambda p: p[l].reshape(1, -1)
        u, yc = _proj_call(h, w_in[l].astype(_BF16), sconv_w[l], row(norm_conv_g))
        bmat, lam_rows, cmat = _s5_operands(ssm_lambda_re[l], ssm_lambda_im[l], ssm_log_dt[l],
                                            ssm_b_re[l], ssm_b_im[l], ssm_c_re[l], ssm_c_im[l])
        y = _ssm_call(u, bmat, lam_rows, cmat, ssm_d[l].reshape(N_SLABS, 1, SLAB_CH))
        h, hb = _outproj_call(y, yc, ssm_glu_w[l].astype(_BF16), row(ssm_glu_b), row(norm_ssm_g),
                              w_out[l].astype(_BF16), h, row(ln1_g), row(ln1_b))
        act = _ffn_up_call(hb, ffn_w_gate[l], ffn_w_up[l], ffn_conv_w[l], row(ffn_conv_b))
        h = _ffn_down_call(act, ffn_w_down[l].astype(_BF16), h, row(ln2_g), row(ln2_b))
    return h[None]
```

```python
import jax
import jax.numpy as jnp
from jax import lax
from jax.experimental import pallas as pl
from jax.experimental.pallas import tpu as pltpu

SEQ = 8192
D_MODEL = 2048
SSM_WIDTH = 1024
CONV_WIDTH = 1024
GROUP_CH = 16
GROUPS = SSM_WIDTH // GROUP_CH
STATE = 64
D_FF = 5632
DEPTH = 1
LN_EPS = 1e-5
RMS_EPS = 1e-6
ALPHA = (2.0 * DEPTH) ** 0.25

SUBLANES = 8
SLAB_GROUPS = 8
N_SLABS = GROUPS // SLAB_GROUPS
SLAB_CH = SLAB_GROUPS * GROUP_CH
SLAB_STATE = SLAB_GROUPS * STATE
SLAB_LANES = 2 * SLAB_STATE

TM_MIX = 512
SSM_CHUNKS = SUBLANES
SSM_STEPS = TM_MIX // SSM_CHUNKS
MIX_ROW_SPLIT = 2
TM_FFN_UP = 1024
TF_FFN_UP = 512
TM_FFN_DOWN = 1024
TF_FFN_DOWN = 512
FFN_ROW_SPLIT = 2
VMEM_LIMIT = 56 * 1024 * 1024

_F32 = jnp.float32
_BF16 = jnp.bfloat16


def _resident(shape):
    return pl.BlockSpec(shape, lambda *_: (0,) * len(shape), pipeline_mode=pl.Buffered(1))


def _layernorm(r, g, b):
    mu = jnp.mean(r, axis=-1, keepdims=True)
    rc = r - mu
    var = jnp.mean(rc * rc, axis=-1, keepdims=True)
    return rc * lax.rsqrt(var + LN_EPS) * g + b


def _rmsnorm(y, g):
    return y * lax.rsqrt(jnp.mean(y * y, axis=-1, keepdims=True) + RMS_EPS) * g


def _conv3(stage_ref, cur, row0, w_ref):
    n = cur.shape[0]
    stage_ref[SUBLANES + row0:SUBLANES + row0 + n, :] = cur
    x1 = stage_ref[SUBLANES - 1 + row0:SUBLANES - 1 + row0 + n, :]
    x2 = stage_ref[SUBLANES - 2 + row0:SUBLANES - 2 + row0 + n, :]
    return w_ref[0:1, :] * x2 + w_ref[1:2, :] * x1 + w_ref[2:3, :] * cur


def _stage_open(stage_ref, halo_ref, first_tile):
    @pl.when(first_tile)
    def _():
        stage_ref[0:SUBLANES, :] = jnp.zeros((SUBLANES, stage_ref.shape[1]), _F32)

    @pl.when(jnp.logical_not(first_tile))
    def _():
        stage_ref[0:SUBLANES, :] = halo_ref[...]


def _stage_close(stage_ref, halo_ref):
    tm = stage_ref.shape[0] - SUBLANES
    halo_ref[...] = stage_ref[tm:tm + SUBLANES, :]


def _chunk_rows(c):
    return pl.ds(c, SSM_STEPS, stride=SSM_CHUNKS)


def _proj_kernel(x_ref, wu_ref, wb_ref, wc_ref, wv_ref, cw_ref, g_ref, wo_ref, gw_ref,
                 u_ref, yc_ref, wob_ref, gwb_ref, halo_ref, stage_ref):
    tm = x_ref.shape[0]
    hm = tm // MIX_ROW_SPLIT
    chunks_per_part = SSM_CHUNKS // MIX_ROW_SPLIT

    wob_ref[...] = wo_ref[...].astype(_BF16)
    gwb_ref[...] = gw_ref[...].astype(_BF16)

    _stage_open(stage_ref, halo_ref, pl.program_id(0) == 0)
    for r in range(MIX_ROW_SPLIT):
        rows = slice(r * hm, (r + 1) * hm)
        xb = x_ref[rows, :].astype(_BF16)
        u = jnp.dot(xb, wu_ref[...], preferred_element_type=_F32)
        for k in range(N_SLABS):
            for c in range(chunks_per_part):
                u_ref[k, _chunk_rows(r * chunks_per_part + c), :] = (
                    u[c * SSM_STEPS:(c + 1) * SSM_STEPS, k * SLAB_CH:(k + 1) * SLAB_CH])
        gate_c = jnp.dot(xb, wc_ref[...], preferred_element_type=_F32)
        v = jnp.dot(xb, wv_ref[...], preferred_element_type=_F32)
        conv = _conv3(stage_ref, gate_c * v, r * hm, cw_ref)
        gate_b = jnp.dot(xb, wb_ref[...], preferred_element_type=_F32)
        yc_ref[rows, :] = _rmsnorm(gate_b * conv, g_ref[...]).astype(_BF16)
    _stage_close(stage_ref, halo_ref)


def _proj_call(x, w_in_bf, sconv_w, norm_conv_g, w_out, glu_w):
    tm = TM_MIX
    nt = SEQ // tm
    wspec = lambda j: pl.BlockSpec((D_MODEL, 1024), lambda i, j=j: (0, j), pipeline_mode=pl.Buffered(1))
    wo_rows, gw_rows = w_out.shape[0] // nt, glu_w.shape[0] // nt
    return pl.pallas_call(
        _proj_kernel,
        grid=(nt,),
        in_specs=[
            pl.BlockSpec((tm, D_MODEL), lambda i: (i, 0)),
            wspec(0), wspec(1), wspec(2), wspec(3),
            _resident((3, CONV_WIDTH)),
            _resident((1, CONV_WIDTH)),
            pl.BlockSpec((wo_rows, D_MODEL), lambda i: (i, 0)),
            pl.BlockSpec((gw_rows, SSM_WIDTH), lambda i: (i, 0)),
        ],
        out_specs=[
            pl.BlockSpec((N_SLABS, tm, SLAB_CH), lambda i: (0, i, 0)),
            pl.BlockSpec((tm, CONV_WIDTH), lambda i: (i, 0)),
            pl.BlockSpec((wo_rows, D_MODEL), lambda i: (i, 0)),
            pl.BlockSpec((gw_rows, SSM_WIDTH), lambda i: (i, 0)),
        ],
        out_shape=[
            jax.ShapeDtypeStruct((N_SLABS, SEQ, SLAB_CH), _F32),
            jax.ShapeDtypeStruct((SEQ, CONV_WIDTH), _BF16),
            jax.ShapeDtypeStruct(w_out.shape, _BF16),
            jax.ShapeDtypeStruct(glu_w.shape, _BF16),
        ],
        scratch_shapes=[
            pltpu.VMEM((SUBLANES, CONV_WIDTH), _F32),
            pltpu.VMEM((SUBLANES + tm, CONV_WIDTH), _F32),
        ],
        compiler_params=pltpu.CompilerParams(dimension_semantics=("arbitrary",), vmem_limit_bytes=VMEM_LIMIT),
        name="proj_shortconv",
    )(x, w_in_bf, w_in_bf, w_in_bf, w_in_bf, sconv_w, norm_conv_g, w_out, glu_w)


def _cmul_add(ar, ai, br, bi, cr, ci):
    return ar * br - ai * bi + cr, ar * bi + ai * br + ci


def _ssm_kernel(u_ref, bmat_ref, lam_ref, cmat_ref, d_ref, y_ref, bu_ref, xb_ref, state_ref):
    nc, ns = SSM_CHUNKS, SSM_STEPS

    @pl.when(pl.program_id(0) == 0)
    def _():
        state_ref[...] = jnp.zeros(state_ref.shape, _F32)

    def expand(k):
        bu_ref[k] = jnp.dot(u_ref[k].astype(_BF16), bmat_ref[k], preferred_element_type=_F32)

    expand(0)
    expand(1)
    re = slice(0, SLAB_STATE)
    im = slice(SLAB_STATE, SLAB_LANES)
    for k in range(N_SLABS):
        lr = jnp.broadcast_to(lam_ref[k, 0:1, :], (nc, SLAB_STATE))
        li = jnp.broadcast_to(lam_ref[k, 1:2, :], (nc, SLAB_STATE))

        xr = jnp.zeros((nc, SLAB_STATE), _F32)
        xi = jnp.zeros((nc, SLAB_STATE), _F32)
        for s in range(ns):
            rows = slice(s * nc, (s + 1) * nc)
            xr, xi = _cmul_add(lr, li, xr, xi, bu_ref[k, rows, re], bu_ref[k, rows, im])

        tr = lam_ref[k, 2:3, :]
        ti = lam_ref[k, 3:4, :]
        cr = state_ref[k, 0:1, :]
        ci = state_ref[k, 1:2, :]
        starts_r, starts_i = [], []
        for c in range(nc):
            starts_r.append(cr)
            starts_i.append(ci)
            cr, ci = _cmul_add(tr, ti, cr, ci, xr[c:c + 1, :], xi[c:c + 1, :])
        state_ref[k, 0:1, :] = cr
        state_ref[k, 1:2, :] = ci
        xr = jnp.concatenate(starts_r, axis=0)
        xi = jnp.concatenate(starts_i, axis=0)

        for s in range(0, ns, 2):
            parts_r, parts_i = [], []
            for q in (s, s + 1):
                rows = slice(q * nc, (q + 1) * nc)
                xr, xi = _cmul_add(lr, li, xr, xi, bu_ref[k, rows, re], bu_ref[k, rows, im])
                parts_r.append(xr)
                parts_i.append(xi)
            rows2 = slice(s * nc, (s + 2) * nc)
            xb_ref[k, rows2, re] = jnp.concatenate(parts_r, axis=0).astype(_BF16)
            xb_ref[k, rows2, im] = jnp.concatenate(parts_i, axis=0).astype(_BF16)
        if k + 2 < N_SLABS:
            expand(k + 2)
        y_ref[k] = (jnp.dot(xb_ref[k], cmat_ref[k], preferred_element_type=_F32)
                    + d_ref[k] * u_ref[k])


def _ssm_call(u, bmat, lam_rows, cmat, d_skip):
    tm = TM_MIX
    slab_rows = pl.BlockSpec((N_SLABS, tm, SLAB_CH), lambda i: (0, i, 0))
    return pl.pallas_call(
        _ssm_kernel,
        grid=(SEQ // tm,),
        in_specs=[
            slab_rows,
            _resident((N_SLABS, SLAB_CH, SLAB_LANES)),
            _resident((N_SLABS, 4, SLAB_STATE)),
            _resident((N_SLABS, SLAB_LANES, SLAB_CH)),
            _resident((N_SLABS, 1, SLAB_CH)),
        ],
        out_specs=slab_rows,
        out_shape=jax.ShapeDtypeStruct((N_SLABS, SEQ, SLAB_CH), _F32),
        scratch_shapes=[
            pltpu.VMEM((N_SLABS, tm, SLAB_LANES), _F32),
            pltpu.VMEM((N_SLABS, tm, SLAB_LANES), _BF16),
            pltpu.VMEM((N_SLABS, 2, SLAB_STATE), _F32),
        ],
        compiler_params=pltpu.CompilerParams(dimension_semantics=("arbitrary",), vmem_limit_bytes=VMEM_LIMIT),
        name="s5_core",
    )(u, bmat, lam_rows, cmat, d_skip)


def _s5_operands(lam_re, lam_im, log_dt, b_re, b_im, c_re, c_im):
    dt = jnp.exp(log_dt)[:, None]
    mag = jnp.exp(lam_re * dt)
    bar_re = mag * jnp.cos(lam_im * dt)
    bar_im = mag * jnp.sin(lam_im * dt)
    inv = 1.0 / (lam_re * lam_re + lam_im * lam_im)
    coef_re = ((bar_re - 1.0) * lam_re + bar_im * lam_im) * inv
    coef_im = (bar_im * lam_re - (bar_re - 1.0) * lam_im) * inv
    bbar_re = coef_re[..., None] * b_re - coef_im[..., None] * b_im
    bbar_im = coef_re[..., None] * b_im + coef_im[..., None] * b_re
    eye = jnp.eye(SLAB_GROUPS, dtype=_F32)

    def b_block(part):
        part = part.reshape(N_SLABS, SLAB_GROUPS, STATE, GROUP_CH)
        blk = jnp.einsum('kgph,gj->kghjp', part, eye)
        return blk.reshape(N_SLABS, SLAB_CH, SLAB_STATE)

    def c_block(part):
        part = part.reshape(N_SLABS, SLAB_GROUPS, GROUP_CH, STATE)
        blk = jnp.einsum('kghp,gj->kgpjh', part, eye)
        return blk.reshape(N_SLABS, SLAB_STATE, SLAB_CH)

    bmat = jnp.concatenate([b_block(bbar_re), b_block(bbar_im)], axis=2).astype(_BF16)
    cmat = jnp.concatenate([c_block(c_re), c_block(-c_im)], axis=1).astype(_BF16)
    nmag = jnp.exp(SSM_STEPS * (lam_re * dt))
    rows = [bar_re, bar_im, nmag * jnp.cos(SSM_STEPS * (lam_im * dt)), nmag * jnp.sin(SSM_STEPS * (lam_im * dt))]
    lam_rows = jnp.stack([r.reshape(N_SLABS, SLAB_STATE) for r in rows], axis=1)
    return bmat, lam_rows, cmat


def _outproj_kernel(y_ref, yc_ref, gw_ref, gb_ref, ng_ref, wt_ref, wb_ref, x_ref, g_ref, b_ref, h_ref, hb_ref):
    tm = x_ref.shape[0]
    hm = tm // MIX_ROW_SPLIT
    chunks_per_part = SSM_CHUNKS // MIX_ROW_SPLIT
    for r in range(MIX_ROW_SPLIT):
        rows = slice(r * hm, (r + 1) * hm)
        y = jnp.concatenate(
            [jnp.concatenate([y_ref[k, _chunk_rows(c), :] for k in range(N_SLABS)], axis=1)
             for c in range(r * chunks_per_part, (r + 1) * chunks_per_part)], axis=0)
        gl = jax.nn.gelu(y)
        z = jnp.dot(gl.astype(_BF16), gw_ref[...], preferred_element_type=_F32) + gb_ref[...]
        ys = _rmsnorm(gl * jax.nn.sigmoid(z), ng_ref[...]).astype(_BF16)
        mix = jnp.dot(ys, wt_ref[...], preferred_element_type=_F32)
        mix = mix + jnp.dot(yc_ref[rows, :], wb_ref[...], preferred_element_type=_F32)
        h = _layernorm(ALPHA * x_ref[rows, :] + mix, g_ref[...], b_ref[...])
        h_ref[rows, :] = h
        hb_ref[rows, :] = h.astype(_BF16)


def _outproj_call(y, yc, glu_w_bf, glu_b, norm_g, w_out_bf, x, ln_g, ln_b):
    tm = TM_MIX
    half = lambda j: pl.BlockSpec((SSM_WIDTH, D_MODEL), lambda i, j=j: (j, 0), pipeline_mode=pl.Buffered(1))
    return pl.pallas_call(
        _outproj_kernel,
        grid=(SEQ // tm,),
        in_specs=[
            pl.BlockSpec((N_SLABS, tm, SLAB_CH), lambda i: (0, i, 0)),
            pl.BlockSpec((tm, CONV_WIDTH), lambda i: (i, 0)),
            _resident((SSM_WIDTH, SSM_WIDTH)),
            _resident((1, SSM_WIDTH)),
            _resident((1, SSM_WIDTH)),
            half(0), half(1),
            pl.BlockSpec((tm, D_MODEL), lambda i: (i, 0)),
            _resident((1, D_MODEL)),
            _resident((1, D_MODEL)),
        ],
        out_specs=[
            pl.BlockSpec((tm, D_MODEL), lambda i: (i, 0)),
            pl.BlockSpec((tm, D_MODEL), lambda i: (i, 0)),
        ],
        out_shape=[
            jax.ShapeDtypeStruct((SEQ, D_MODEL), _F32),
            jax.ShapeDtypeStruct((SEQ, D_MODEL), _BF16),
        ],
        compiler_params=pltpu.CompilerParams(dimension_semantics=("arbitrary",), vmem_limit_bytes=VMEM_LIMIT),
        name="glu_outproj_ln1",
    )(y, yc, glu_w_bf, glu_b, norm_g, w_out_bf, w_out_bf, x, ln_g, ln_b)


def _ffn_up_kernel(hb_ref, wg_ref, wu_ref, cw_ref, cb_ref, wd_ref, act_ref, wdb_ref,
                   wgb_ref, wub_ref, halo_ref, stage_ref):
    i = pl.program_id(1)
    tm = hb_ref.shape[0]
    hm = tm // FFN_ROW_SPLIT

    wdb_ref[...] = wd_ref[...].astype(_BF16)

    @pl.when(i == 0)
    def _():
        wgb_ref[...] = wg_ref[...].astype(_BF16)
        wub_ref[...] = wu_ref[...].astype(_BF16)

    _stage_open(stage_ref, halo_ref, i == 0)
    for r in range(FFN_ROW_SPLIT):
        rows = slice(r * hm, (r + 1) * hm)
        hb = hb_ref[rows, :]
        gate = jnp.dot(hb, wgb_ref[...], preferred_element_type=_F32)
        up = jnp.dot(hb, wub_ref[...], preferred_element_type=_F32)
        a = _conv3(stage_ref, gate, r * hm, cw_ref) + cb_ref[...]
        act_ref[rows, :] = (jax.nn.silu(a) * up).astype(_BF16)
    _stage_close(stage_ref, halo_ref)


def _ffn_up_call(hb, w_gate, w_up, conv_w, conv_b, w_down):
    tm, tf = TM_FFN_UP, TF_FFN_UP
    nk, ni = D_FF // tf, SEQ // tm
    wd_rows = D_FF // (nk * ni)
    return pl.pallas_call(
        _ffn_up_kernel,
        grid=(nk, ni),
        in_specs=[
            pl.BlockSpec((tm, D_MODEL), lambda k, i: (i, 0)),
            pl.BlockSpec((D_MODEL, tf), lambda k, i: (0, k)),
            pl.BlockSpec((D_MODEL, tf), lambda k, i: (0, k)),
            pl.BlockSpec((3, tf), lambda k, i: (0, k)),
            pl.BlockSpec((1, tf), lambda k, i: (0, k)),
            pl.BlockSpec((wd_rows, D_MODEL), lambda k, i: (k * ni + i, 0)),
        ],
        out_specs=[
            pl.BlockSpec((tm, tf), lambda k, i: (i, k)),
            pl.BlockSpec((wd_rows, D_MODEL), lambda k, i: (k * ni + i, 0)),
        ],
        out_shape=[
            jax.ShapeDtypeStruct((SEQ, D_FF), _BF16),
            jax.ShapeDtypeStruct((D_FF, D_MODEL), _BF16),
        ],
        scratch_shapes=[
            pltpu.VMEM((D_MODEL, tf), _BF16),
            pltpu.VMEM((D_MODEL, tf), _BF16),
            pltpu.VMEM((SUBLANES, tf), _F32),
            pltpu.VMEM((SUBLANES + tm, tf), _F32),
        ],
        compiler_params=pltpu.CompilerParams(dimension_semantics=("arbitrary", "arbitrary"),
                                             vmem_limit_bytes=VMEM_LIMIT),
        name="convffn_up",
    )(hb, w_gate, w_up, conv_w, conv_b, w_down)


def _ffn_down_kernel(act_ref, wd_ref, h_ref, g_ref, b_ref, o_ref):
    k = pl.program_id(1)

    @pl.when(k == 0)
    def _():
        o_ref[...] = jnp.zeros(o_ref.shape, _F32)

    o_ref[...] += jnp.dot(act_ref[...], wd_ref[...], preferred_element_type=_F32)

    @pl.when(k == pl.num_programs(1) - 1)
    def _():
        o_ref[...] = _layernorm(ALPHA * h_ref[...] + o_ref[...], g_ref[...], b_ref[...])


def _ffn_down_call(act, wd_bf, h, ln_g, ln_b):
    tm, tf = TM_FFN_DOWN, TF_FFN_DOWN
    return pl.pallas_call(
        _ffn_down_kernel,
        grid=(SEQ // tm, D_FF // tf),
        in_specs=[
            pl.BlockSpec((tm, tf), lambda i, k: (i, k)),
            pl.BlockSpec((tf, D_MODEL), lambda i, k: (k, 0)),
            pl.BlockSpec((tm, D_MODEL), lambda i, k: (i, 0)),
            _resident((1, D_MODEL)),
            _resident((1, D_MODEL)),
        ],
        out_specs=pl.BlockSpec((tm, D_MODEL), lambda i, k: (i, 0)),
        out_shape=jax.ShapeDtypeStruct((SEQ, D_MODEL), _F32),
        compiler_params=pltpu.CompilerParams(dimension_semantics=("arbitrary", "arbitrary"),
                                             vmem_limit_bytes=VMEM_LIMIT),
        name="convffn_down_ln2",
    )(act, wd_bf, h, ln_g, ln_b)


def kernel(x, w_in, ssm_lambda_re, ssm_lambda_im, ssm_log_dt, ssm_b_re, ssm_b_im, ssm_c_re, ssm_c_im,
           ssm_d, ssm_glu_w, ssm_glu_b, sconv_w, norm_ssm_g, norm_conv_g, w_out, ln1_g, ln1_b,
           ffn_w_gate, ffn_w_up, ffn_conv_w, ffn_conv_b, ffn_w_down, ln2_g, ln2_b):
    assert x.shape == (1, SEQ, D_MODEL) and w_in.shape[0] == DEPTH
    h = x[0]
    for l in range(DEPTH):
        row = lambda p: p[l].reshape(1, -1)
        u, yc, w_out_bf, glu_w_bf = _proj_call(h, w_in[l].astype(_BF16), sconv_w[l], row(norm_conv_g),
                                               w_out[l], ssm_glu_w[l])
        bmat, lam_rows, cmat = _s5_operands(ssm_lambda_re[l], ssm_lambda_im[l], ssm_log_dt[l],
                                            ssm_b_re[l], ssm_b_im[l], ssm_c_re[l], ssm_c_im[l])
        y = _ssm_call(u, bmat, lam_rows, cmat, ssm_d[l].reshape(N_SLABS, 1, SLAB_CH))
        h, hb = _outproj_call(y, yc, glu_w_bf, row(ssm_glu_b), row(norm_ssm_g),
                              w_out_bf, h, row(ln1_g), row(ln1_b))
        act, w_down_bf = _ffn_up_call(hb, ffn_w_gate[l], ffn_w_up[l], ffn_conv_w[l], row(ffn_conv_b),
                                      ffn_w_down[l])
        h = _ffn_down_call(act, w_down_bf, h, row(ln2_g), row(ln2_b))
    return h[None]
```

```python
import jax
import jax.numpy as jnp
from jax import lax
from jax.experimental import pallas as pl
from jax.experimental.pallas import tpu as pltpu

SEQ = 8192
D_MODEL = 2048
SSM_WIDTH = 1024
CONV_WIDTH = 1024
GROUP_CH = 16
GROUPS = SSM_WIDTH // GROUP_CH
STATE = 64
D_FF = 5632
DEPTH = 1
LN_EPS = 1e-5
RMS_EPS = 1e-6
ALPHA = (2.0 * DEPTH) ** 0.25

SUBLANES = 8
SLAB_GROUPS = 8
N_SLABS = GROUPS // SLAB_GROUPS
SLAB_CH = SLAB_GROUPS * GROUP_CH
SLAB_STATE = SLAB_GROUPS * STATE
SLAB_LANES = 2 * SLAB_STATE

TM_MIX = 512
SSM_CHUNKS = SUBLANES
SSM_STEPS = TM_MIX // SSM_CHUNKS
MIX_ROW_SPLIT = 2
TM_FFN_UP = 1024
TF_FFN_UP = 512
TM_FFN_DOWN = 512
FFN_ROW_SPLIT = 2
VMEM_LIMIT = 56 * 1024 * 1024
VMEM_LIMIT_DOWN = 60 * 1024 * 1024

_F32 = jnp.float32
_BF16 = jnp.bfloat16


def _resident(shape):
    return pl.BlockSpec(shape, lambda *_: (0,) * len(shape), pipeline_mode=pl.Buffered(1))


def _layernorm(r, g, b):
    mu = jnp.mean(r, axis=-1, keepdims=True)
    rc = r - mu
    var = jnp.mean(rc * rc, axis=-1, keepdims=True)
    return rc * lax.rsqrt(var + LN_EPS) * g + b


def _rmsnorm(y, g):
    return y * lax.rsqrt(jnp.mean(y * y, axis=-1, keepdims=True) + RMS_EPS) * g


def _pack_rows(v):
    return pltpu.bitcast(v, jnp.int32)


def _unpack_rows(v):
    return pltpu.bitcast(v, _BF16)


def _conv3(stage_ref, cur, row0, w_ref):
    n = cur.shape[0]
    stage_ref[SUBLANES + row0:SUBLANES + row0 + n, :] = cur
    x1 = stage_ref[SUBLANES - 1 + row0:SUBLANES - 1 + row0 + n, :]
    x2 = stage_ref[SUBLANES - 2 + row0:SUBLANES - 2 + row0 + n, :]
    return w_ref[0:1, :] * x2 + w_ref[1:2, :] * x1 + w_ref[2:3, :] * cur


def _stage_open(stage_ref, halo_ref, first_tile):
    @pl.when(first_tile)
    def _():
        stage_ref[0:SUBLANES, :] = jnp.zeros((SUBLANES, stage_ref.shape[1]), _F32)

    @pl.when(jnp.logical_not(first_tile))
    def _():
        stage_ref[0:SUBLANES, :] = halo_ref[...]


def _stage_close(stage_ref, halo_ref):
    tm = stage_ref.shape[0] - SUBLANES
    halo_ref[...] = stage_ref[tm:tm + SUBLANES, :]


def _chunk_rows(c):
    return pl.ds(c, SSM_STEPS, stride=SSM_CHUNKS)


def _proj_kernel(x_ref, wu_ref, wb_ref, wc_ref, wv_ref, cw_ref, g_ref, wo_ref, gw_ref,
                 u_ref, yc_ref, wob_ref, gwb_ref, halo_ref, stage_ref):
    tm = x_ref.shape[0]
    hm = tm // MIX_ROW_SPLIT
    chunks_per_part = SSM_CHUNKS // MIX_ROW_SPLIT

    wob_ref[...] = wo_ref[...].astype(_BF16)
    gwb_ref[...] = gw_ref[...].astype(_BF16)

    _stage_open(stage_ref, halo_ref, pl.program_id(0) == 0)
    for r in range(MIX_ROW_SPLIT):
        rows = slice(r * hm, (r + 1) * hm)
        xb = x_ref[rows, :].astype(_BF16)
        u = jnp.dot(xb, wu_ref[...], preferred_element_type=_F32)
        for k in range(N_SLABS):
            for c in range(chunks_per_part):
                u_ref[k, _chunk_rows(r * chunks_per_part + c), :] = (
                    u[c * SSM_STEPS:(c + 1) * SSM_STEPS, k * SLAB_CH:(k + 1) * SLAB_CH])
        gate_c = jnp.dot(xb, wc_ref[...], preferred_element_type=_F32)
        v = jnp.dot(xb, wv_ref[...], preferred_element_type=_F32)
        conv = _conv3(stage_ref, gate_c * v, r * hm, cw_ref)
        gate_b = jnp.dot(xb, wb_ref[...], preferred_element_type=_F32)
        yc_ref[rows, :] = _rmsnorm(gate_b * conv, g_ref[...]).astype(_BF16)
    _stage_close(stage_ref, halo_ref)


def _proj_call(x, w_in_bf, sconv_w, norm_conv_g, w_out, glu_w):
    tm = TM_MIX
    nt = SEQ // tm
    wspec = lambda j: pl.BlockSpec((D_MODEL, 1024), lambda i, j=j: (0, j), pipeline_mode=pl.Buffered(1))
    wo_rows, gw_rows = w_out.shape[0] // nt, glu_w.shape[0] // nt
    return pl.pallas_call(
        _proj_kernel,
        grid=(nt,),
        in_specs=[
            pl.BlockSpec((tm, D_MODEL), lambda i: (i, 0)),
            wspec(0), wspec(1), wspec(2), wspec(3),
            _resident((3, CONV_WIDTH)),
            _resident((1, CONV_WIDTH)),
            pl.BlockSpec((wo_rows, D_MODEL), lambda i: (i, 0)),
            pl.BlockSpec((gw_rows, SSM_WIDTH), lambda i: (i, 0)),
        ],
        out_specs=[
            pl.BlockSpec((N_SLABS, tm, SLAB_CH), lambda i: (0, i, 0)),
            pl.BlockSpec((tm, CONV_WIDTH), lambda i: (i, 0)),
            pl.BlockSpec((wo_rows, D_MODEL), lambda i: (i, 0)),
            pl.BlockSpec((gw_rows, SSM_WIDTH), lambda i: (i, 0)),
        ],
        out_shape=[
            jax.ShapeDtypeStruct((N_SLABS, SEQ, SLAB_CH), _F32),
            jax.ShapeDtypeStruct((SEQ, CONV_WIDTH), _BF16),
            jax.ShapeDtypeStruct(w_out.shape, _BF16),
            jax.ShapeDtypeStruct(glu_w.shape, _BF16),
        ],
        scratch_shapes=[
            pltpu.VMEM((SUBLANES, CONV_WIDTH), _F32),
            pltpu.VMEM((SUBLANES + tm, CONV_WIDTH), _F32),
        ],
        compiler_params=pltpu.CompilerParams(dimension_semantics=("arbitrary",), vmem_limit_bytes=VMEM_LIMIT),
        name="proj_shortconv",
    )(x, w_in_bf, w_in_bf, w_in_bf, w_in_bf, sconv_w, norm_conv_g, w_out, glu_w)


def _cmul_add(ar, ai, br, bi, cr, ci):
    return ar * br - ai * bi + cr, ar * bi + ai * br + ci


def _ssm_kernel(u_ref, bmat_ref, lam_ref, cmat_ref, d_ref, y_ref, bu_ref, xb_ref, state_ref):
    nc, ns = SSM_CHUNKS, SSM_STEPS

    @pl.when(pl.program_id(0) == 0)
    def _():
        state_ref[...] = jnp.zeros(state_ref.shape, _F32)

    def expand(k):
        bu_ref[k] = jnp.dot(u_ref[k].astype(_BF16), bmat_ref[k], preferred_element_type=_F32)

    expand(0)
    expand(1)
    re = slice(0, SLAB_STATE)
    im = slice(SLAB_STATE, SLAB_LANES)
    for k in range(N_SLABS):
        lr = jnp.broadcast_to(lam_ref[k, 0:1, :], (nc, SLAB_STATE))
        li = jnp.broadcast_to(lam_ref[k, 1:2, :], (nc, SLAB_STATE))

        xr = jnp.zeros((nc, SLAB_STATE), _F32)
        xi = jnp.zeros((nc, SLAB_STATE), _F32)
        for s in range(ns):
            rows = slice(s * nc, (s + 1) * nc)
            xr, xi = _cmul_add(lr, li, xr, xi, bu_ref[k, rows, re], bu_ref[k, rows, im])

        tr = lam_ref[k, 2:3, :]
        ti = lam_ref[k, 3:4, :]
        cr = state_ref[k, 0:1, :]
        ci = state_ref[k, 1:2, :]
        starts_r, starts_i = [], []
        for c in range(nc):
            starts_r.append(cr)
            starts_i.append(ci)
            cr, ci = _cmul_add(tr, ti, cr, ci, xr[c:c + 1, :], xi[c:c + 1, :])
        state_ref[k, 0:1, :] = cr
        state_ref[k, 1:2, :] = ci
        xr = jnp.concatenate(starts_r, axis=0)
        xi = jnp.concatenate(starts_i, axis=0)

        for s in range(0, ns, 2):
            parts_r, parts_i = [], []
            for q in (s, s + 1):
                rows = slice(q * nc, (q + 1) * nc)
                xr, xi = _cmul_add(lr, li, xr, xi, bu_ref[k, rows, re], bu_ref[k, rows, im])
                parts_r.append(xr)
                parts_i.append(xi)
            rows2 = slice(s * nc, (s + 2) * nc)
            xb_ref[k, rows2, re] = jnp.concatenate(parts_r, axis=0).astype(_BF16)
            xb_ref[k, rows2, im] = jnp.concatenate(parts_i, axis=0).astype(_BF16)
        if k + 2 < N_SLABS:
            expand(k + 2)
        y_ref[k] = (jnp.dot(xb_ref[k], cmat_ref[k], preferred_element_type=_F32)
                    + d_ref[k] * u_ref[k])


def _ssm_call(u, bmat, lam_rows, cmat, d_skip):
    tm = TM_MIX
    slab_rows = pl.BlockSpec((N_SLABS, tm, SLAB_CH), lambda i: (0, i, 0))
    return pl.pallas_call(
        _ssm_kernel,
        grid=(SEQ // tm,),
        in_specs=[
            slab_rows,
            _resident((N_SLABS, SLAB_CH, SLAB_LANES)),
            _resident((N_SLABS, 4, SLAB_STATE)),
            _resident((N_SLABS, SLAB_LANES, SLAB_CH)),
            _resident((N_SLABS, 1, SLAB_CH)),
        ],
        out_specs=slab_rows,
        out_shape=jax.ShapeDtypeStruct((N_SLABS, SEQ, SLAB_CH), _F32),
        scratch_shapes=[
            pltpu.VMEM((N_SLABS, tm, SLAB_LANES), _F32),
            pltpu.VMEM((N_SLABS, tm, SLAB_LANES), _BF16),
            pltpu.VMEM((N_SLABS, 2, SLAB_STATE), _F32),
        ],
        compiler_params=pltpu.CompilerParams(dimension_semantics=("arbitrary",), vmem_limit_bytes=VMEM_LIMIT),
        name="s5_core",
    )(u, bmat, lam_rows, cmat, d_skip)


def _s5_operands(lam_re, lam_im, log_dt, b_re, b_im, c_re, c_im):
    dt = jnp.exp(log_dt)[:, None]
    mag = jnp.exp(lam_re * dt)
    bar_re = mag * jnp.cos(lam_im * dt)
    bar_im = mag * jnp.sin(lam_im * dt)
    inv = 1.0 / (lam_re * lam_re + lam_im * lam_im)
    coef_re = ((bar_re - 1.0) * lam_re + bar_im * lam_im) * inv
    coef_im = (bar_im * lam_re - (bar_re - 1.0) * lam_im) * inv
    bbar_re = coef_re[..., None] * b_re - coef_im[..., None] * b_im
    bbar_im = coef_re[..., None] * b_im + coef_im[..., None] * b_re
    eye = jnp.eye(SLAB_GROUPS, dtype=_F32)

    def b_block(part):
        part = part.reshape(N_SLABS, SLAB_GROUPS, STATE, GROUP_CH)
        blk = jnp.einsum('kgph,gj->kghjp', part, eye)
        return blk.reshape(N_SLABS, SLAB_CH, SLAB_STATE)

    def c_block(part):
        part = part.reshape(N_SLABS, SLAB_GROUPS, GROUP_CH, STATE)
        blk = jnp.einsum('kghp,gj->kgpjh', part, eye)
        return blk.reshape(N_SLABS, SLAB_STATE, SLAB_CH)

    bmat = jnp.concatenate([b_block(bbar_re), b_block(bbar_im)], axis=2).astype(_BF16)
    cmat = jnp.concatenate([c_block(c_re), c_block(-c_im)], axis=1).astype(_BF16)
    nmag = jnp.exp(SSM_STEPS * (lam_re * dt))
    rows = [bar_re, bar_im, nmag * jnp.cos(SSM_STEPS * (lam_im * dt)), nmag * jnp.sin(SSM_STEPS * (lam_im * dt))]
    lam_rows = jnp.stack([r.reshape(N_SLABS, SLAB_STATE) for r in rows], axis=1)
    return bmat, lam_rows, cmat


def _outproj_kernel(y_ref, yc_ref, gw_ref, gb_ref, ng_ref, wt_ref, wb_ref, x_ref, g_ref, b_ref, h_ref, hb_ref):
    tm = x_ref.shape[0]
    hm = tm // MIX_ROW_SPLIT
    chunks_per_part = SSM_CHUNKS // MIX_ROW_SPLIT
    for r in range(MIX_ROW_SPLIT):
        rows = slice(r * hm, (r + 1) * hm)
        y = jnp.concatenate(
            [jnp.concatenate([y_ref[k, _chunk_rows(c), :] for k in range(N_SLABS)], axis=1)
             for c in range(r * chunks_per_part, (r + 1) * chunks_per_part)], axis=0)
        gl = jax.nn.gelu(y)
        z = jnp.dot(gl.astype(_BF16), gw_ref[...], preferred_element_type=_F32) + gb_ref[...]
        ys = _rmsnorm(gl * jax.nn.sigmoid(z), ng_ref[...]).astype(_BF16)
        mix = jnp.dot(ys, wt_ref[...], preferred_element_type=_F32)
        mix = mix + jnp.dot(yc_ref[rows, :], wb_ref[...], preferred_element_type=_F32)
        h = _layernorm(ALPHA * x_ref[rows, :] + mix, g_ref[...], b_ref[...])
        h_ref[rows, :] = h
        hb_ref[r * hm // 2:(r + 1) * hm // 2, :] = _pack_rows(h.astype(_BF16))


def _outproj_call(y, yc, glu_w_bf, glu_b, norm_g, w_out_bf, x, ln_g, ln_b):
    tm = TM_MIX
    half = lambda j: pl.BlockSpec((SSM_WIDTH, D_MODEL), lambda i, j=j: (j, 0), pipeline_mode=pl.Buffered(1))
    return pl.pallas_call(
        _outproj_kernel,
        grid=(SEQ // tm,),
        in_specs=[
            pl.BlockSpec((N_SLABS, tm, SLAB_CH), lambda i: (0, i, 0)),
            pl.BlockSpec((tm, CONV_WIDTH), lambda i: (i, 0)),
            _resident((SSM_WIDTH, SSM_WIDTH)),
            _resident((1, SSM_WIDTH)),
            _resident((1, SSM_WIDTH)),
            half(0), half(1),
            pl.BlockSpec((tm, D_MODEL), lambda i: (i, 0)),
            _resident((1, D_MODEL)),
            _resident((1, D_MODEL)),
        ],
        out_specs=[
            pl.BlockSpec((tm, D_MODEL), lambda i: (i, 0)),
            pl.BlockSpec((tm // 2, D_MODEL), lambda i: (i, 0)),
        ],
        out_shape=[
            jax.ShapeDtypeStruct((SEQ, D_MODEL), _F32),
            jax.ShapeDtypeStruct((SEQ // 2, D_MODEL), jnp.int32),
        ],
        compiler_params=pltpu.CompilerParams(dimension_semantics=("arbitrary",), vmem_limit_bytes=VMEM_LIMIT),
        name="glu_outproj_ln1",
    )(y, yc, glu_w_bf, glu_b, norm_g, w_out_bf, w_out_bf, x, ln_g, ln_b)


def _ffn_up_kernel(hb_ref, wg_ref, wu_ref, cw_ref, cb_ref, wd_ref, act_ref, wdb_ref,
                   wgb_ref, wub_ref, halo_ref, stage_ref):
    i = pl.program_id(1)
    tm = act_ref.shape[0]
    hm = tm // FFN_ROW_SPLIT

    wdb_ref[...] = wd_ref[...].astype(_BF16)

    @pl.when(i == 0)
    def _():
        wgb_ref[...] = wg_ref[...].astype(_BF16)
        wub_ref[...] = wu_ref[...].astype(_BF16)

    _stage_open(stage_ref, halo_ref, i == 0)
    for r in range(FFN_ROW_SPLIT):
        rows = slice(r * hm, (r + 1) * hm)
        hb = _unpack_rows(hb_ref[r * hm // 2:(r + 1) * hm // 2, :])
        gate = jnp.dot(hb, wgb_ref[...], preferred_element_type=_F32)
        up = jnp.dot(hb, wub_ref[...], preferred_element_type=_F32)
        a = _conv3(stage_ref, gate, r * hm, cw_ref) + cb_ref[...]
        act_ref[rows, :] = (jax.nn.silu(a) * up).astype(_BF16)
    _stage_close(stage_ref, halo_ref)


def _ffn_up_call(hb, w_gate, w_up, conv_w, conv_b, w_down):
    tm, tf = TM_FFN_UP, TF_FFN_UP
    nk, ni = D_FF // tf, SEQ // tm
    wd_rows = D_FF // (nk * ni)
    return pl.pallas_call(
        _ffn_up_kernel,
        grid=(nk, ni),
        in_specs=[
            pl.BlockSpec((tm // 2, D_MODEL), lambda k, i: (i, 0)),
            pl.BlockSpec((D_MODEL, tf), lambda k, i: (0, k)),
            pl.BlockSpec((D_MODEL, tf), lambda k, i: (0, k)),
            pl.BlockSpec((3, tf), lambda k, i: (0, k)),
            pl.BlockSpec((1, tf), lambda k, i: (0, k)),
            pl.BlockSpec((wd_rows, D_MODEL), lambda k, i: (k * ni + i, 0)),
        ],
        out_specs=[
            pl.BlockSpec((tm, tf), lambda k, i: (i, k)),
            pl.BlockSpec((wd_rows, D_MODEL), lambda k, i: (k * ni + i, 0)),
        ],
        out_shape=[
            jax.ShapeDtypeStruct((SEQ, D_FF), _BF16),
            jax.ShapeDtypeStruct((D_FF, D_MODEL), _BF16),
        ],
        scratch_shapes=[
            pltpu.VMEM((D_MODEL, tf), _BF16),
            pltpu.VMEM((D_MODEL, tf), _BF16),
            pltpu.VMEM((SUBLANES, tf), _F32),
            pltpu.VMEM((SUBLANES + tm, tf), _F32),
        ],
        compiler_params=pltpu.CompilerParams(dimension_semantics=("arbitrary", "arbitrary"),
                                             vmem_limit_bytes=VMEM_LIMIT),
        name="convffn_up",
    )(hb, w_gate, w_up, conv_w, conv_b, w_down)


def _ffn_down_kernel(act_ref, wd_ref, h_ref, g_ref, b_ref, o_ref):
    tm = o_ref.shape[0]
    hm = tm // FFN_ROW_SPLIT
    for r in range(FFN_ROW_SPLIT):
        rows = slice(r * hm, (r + 1) * hm)
        f = jnp.dot(act_ref[rows, :], wd_ref[...], preferred_element_type=_F32)
        o_ref[rows, :] = _layernorm(ALPHA * h_ref[rows, :] + f, g_ref[...], b_ref[...])


def _ffn_down_call(act, wd_bf, h, ln_g, ln_b):
    tm = TM_FFN_DOWN
    return pl.pallas_call(
        _ffn_down_kernel,
        grid=(SEQ // tm,),
        in_specs=[
            pl.BlockSpec((tm, D_FF), lambda i: (i, 0)),
            _resident((D_FF, D_MODEL)),
            pl.BlockSpec((tm, D_MODEL), lambda i: (i, 0)),
            _resident((1, D_MODEL)),
            _resident((1, D_MODEL)),
        ],
        out_specs=pl.BlockSpec((tm, D_MODEL), lambda i: (i, 0)),
        out_shape=jax.ShapeDtypeStruct((SEQ, D_MODEL), _F32),
        compiler_params=pltpu.CompilerParams(dimension_semantics=("arbitrary",),
                                             vmem_limit_bytes=VMEM_LIMIT_DOWN),
        name="convffn_down_ln2",
    )(act, wd_bf, h, ln_g, ln_b)


def kernel(x, w_in, ssm_lambda_re, ssm_lambda_im, ssm_log_dt, ssm_b_re, ssm_b_im, ssm_c_re, ssm_c_im,
           ssm_d, ssm_glu_w, ssm_glu_b, sconv_w, norm_ssm_g, norm_conv_g, w_out, ln1_g, ln1_b,
           ffn_w_gate, ffn_w_up, ffn_conv_w, ffn_conv_b, ffn_w_down, ln2_g, ln2_b):
    assert x.shape == (1, SEQ, D_MODEL) and w_in.shape[0] == DEPTH
    h = x[0]
    for l in range(DEPTH):
        row = lambda p: p[l].reshape(1, -1)
        u, yc, w_out_bf, glu_w_bf = _proj_call(h, w_in[l].astype(_BF16), sconv_w[l], row(norm_conv_g),
                                               w_out[l], ssm_glu_w[l])
        bmat, lam_rows, cmat = _s5_operands(ssm_lambda_re[l], ssm_lambda_im[l], ssm_log_dt[l],
                                            ssm_b_re[l], ssm_b_im[l], ssm_c_re[l], ssm_c_im[l])
        y = _ssm_call(u, bmat, lam_rows, cmat, ssm_d[l].reshape(N_SLABS, 1, SLAB_CH))
        h, hb = _outproj_call(y, yc, glu_w_bf, row(ssm_glu_b), row(norm_ssm_g),
                              w_out_bf, h, row(ln1_g), row(ln1_b))
        act, w_down_bf = _ffn_up_call(hb, ffn_w_gate[l], ffn_w_up[l], ffn_conv_w[l], row(ffn_conv_b),
                                      ffn_w_down[l])
        h = _ffn_down_call(act, w_down_bf, h, row(ln2_g), row(ln2_b))
    return h[None]
```

```python
import jax
import jax.numpy as jnp
from jax import lax
from jax.experimental import pallas as pl
from jax.experimental.pallas import tpu as pltpu

SEQ = 8192
D_MODEL = 2048
SSM_WIDTH = 1024
CONV_WIDTH = 1024
GROUP_CH = 16
GROUPS = SSM_WIDTH // GROUP_CH
STATE = 64
D_FF = 5632
DEPTH = 1
LN_EPS = 1e-5
RMS_EPS = 1e-6
ALPHA = (2.0 * DEPTH) ** 0.25

SUBLANES = 8
SLAB_GROUPS = 8
N_SLABS = GROUPS // SLAB_GROUPS
SLAB_CH = SLAB_GROUPS * GROUP_CH
SLAB_STATE = SLAB_GROUPS * STATE
SLAB_LANES = 2 * SLAB_STATE

TM_MIX = 512
SSM_CHUNKS = SUBLANES
SSM_STEPS = TM_MIX // SSM_CHUNKS
MIX_ROW_SPLIT = 2
TM_FFN_UP = 1024
TF_FFN_UP = 512
TM_FFN_DOWN = 512
FFN_UP_ROW_SPLIT = 4
FFN_ROW_SPLIT = 2
VMEM_LIMIT = 56 * 1024 * 1024
VMEM_LIMIT_DOWN = 60 * 1024 * 1024

_F32 = jnp.float32
_BF16 = jnp.bfloat16


def _resident(shape):
    return pl.BlockSpec(shape, lambda *_: (0,) * len(shape), pipeline_mode=pl.Buffered(1))


def _layernorm(r, g, b):
    mu = jnp.mean(r, axis=-1, keepdims=True)
    rc = r - mu
    var = jnp.mean(rc * rc, axis=-1, keepdims=True)
    return rc * lax.rsqrt(var + LN_EPS) * g + b


def _rmsnorm(y, g):
    return y * lax.rsqrt(jnp.mean(y * y, axis=-1, keepdims=True) + RMS_EPS) * g


def _pack_rows(v):
    return pltpu.bitcast(v, jnp.int32)


def _unpack_rows(v):
    return pltpu.bitcast(v, _BF16)


def _conv3(stage_ref, cur, row0, w_ref):
    n = cur.shape[0]
    stage_ref[SUBLANES + row0:SUBLANES + row0 + n, :] = cur
    x1 = stage_ref[SUBLANES - 1 + row0:SUBLANES - 1 + row0 + n, :]
    x2 = stage_ref[SUBLANES - 2 + row0:SUBLANES - 2 + row0 + n, :]
    return w_ref[0:1, :] * x2 + w_ref[1:2, :] * x1 + w_ref[2:3, :] * cur


def _stage_open(stage_ref, halo_ref, first_tile):
    @pl.when(first_tile)
    def _():
        stage_ref[0:SUBLANES, :] = jnp.zeros((SUBLANES, stage_ref.shape[1]), _F32)

    @pl.when(jnp.logical_not(first_tile))
    def _():
        stage_ref[0:SUBLANES, :] = halo_ref[...]


def _stage_close(stage_ref, halo_ref):
    tm = stage_ref.shape[0] - SUBLANES
    halo_ref[...] = stage_ref[tm:tm + SUBLANES, :]


def _chunk_rows(c):
    return pl.ds(c, SSM_STEPS, stride=SSM_CHUNKS)


def _proj_kernel(x_ref, wu_ref, wb_ref, wc_ref, wv_ref, cw_ref, g_ref, wo_ref, gw_ref,
                 u_ref, yc_ref, wob_ref, gwb_ref, halo_ref, stage_ref):
    tm = x_ref.shape[0]
    hm = tm // MIX_ROW_SPLIT
    chunks_per_part = SSM_CHUNKS // MIX_ROW_SPLIT

    wob_ref[...] = wo_ref[...].astype(_BF16)
    gwb_ref[...] = gw_ref[...].astype(_BF16)

    _stage_open(stage_ref, halo_ref, pl.program_id(0) == 0)
    for r in range(MIX_ROW_SPLIT):
        rows = slice(r * hm, (r + 1) * hm)
        xb = x_ref[rows, :].astype(_BF16)
        u = jnp.dot(xb, wu_ref[...], preferred_element_type=_F32)
        for k in range(N_SLABS):
            for c in range(chunks_per_part):
                u_ref[k, _chunk_rows(r * chunks_per_part + c), :] = (
                    u[c * SSM_STEPS:(c + 1) * SSM_STEPS, k * SLAB_CH:(k + 1) * SLAB_CH])
        gate_c = jnp.dot(xb, wc_ref[...], preferred_element_type=_F32)
        v = jnp.dot(xb, wv_ref[...], preferred_element_type=_F32)
        conv = _conv3(stage_ref, gate_c * v, r * hm, cw_ref)
        gate_b = jnp.dot(xb, wb_ref[...], preferred_element_type=_F32)
        yc_ref[rows, :] = _rmsnorm(gate_b * conv, g_ref[...]).astype(_BF16)
    _stage_close(stage_ref, halo_ref)


def _proj_call(x, w_in_bf, sconv_w, norm_conv_g, w_out, glu_w):
    tm = TM_MIX
    nt = SEQ // tm
    wspec = lambda j: pl.BlockSpec((D_MODEL, 1024), lambda i, j=j: (0, j), pipeline_mode=pl.Buffered(1))
    wo_rows, gw_rows = w_out.shape[0] // nt, glu_w.shape[0] // nt
    return pl.pallas_call(
        _proj_kernel,
        grid=(nt,),
        in_specs=[
            pl.BlockSpec((tm, D_MODEL), lambda i: (i, 0)),
            wspec(0), wspec(1), wspec(2), wspec(3),
            _resident((3, CONV_WIDTH)),
            _resident((1, CONV_WIDTH)),
            pl.BlockSpec((wo_rows, D_MODEL), lambda i: (i, 0)),
            pl.BlockSpec((gw_rows, SSM_WIDTH), lambda i: (i, 0)),
        ],
        out_specs=[
            pl.BlockSpec((N_SLABS, tm, SLAB_CH), lambda i: (0, i, 0)),
            pl.BlockSpec((tm, CONV_WIDTH), lambda i: (i, 0)),
            pl.BlockSpec((wo_rows, D_MODEL), lambda i: (i, 0)),
            pl.BlockSpec((gw_rows, SSM_WIDTH), lambda i: (i, 0)),
        ],
        out_shape=[
            jax.ShapeDtypeStruct((N_SLABS, SEQ, SLAB_CH), _F32),
            jax.ShapeDtypeStruct((SEQ, CONV_WIDTH), _BF16),
            jax.ShapeDtypeStruct(w_out.shape, _BF16),
            jax.ShapeDtypeStruct(glu_w.shape, _BF16),
        ],
        scratch_shapes=[
            pltpu.VMEM((SUBLANES, CONV_WIDTH), _F32),
            pltpu.VMEM((SUBLANES + tm, CONV_WIDTH), _F32),
        ],
        compiler_params=pltpu.CompilerParams(dimension_semantics=("arbitrary",), vmem_limit_bytes=VMEM_LIMIT),
        name="proj_shortconv",
    )(x, w_in_bf, w_in_bf, w_in_bf, w_in_bf, sconv_w, norm_conv_g, w_out, glu_w)


def _cmul_add(ar, ai, br, bi, cr, ci):
    return ar * br - ai * bi + cr, ar * bi + ai * br + ci


_NT = (((1,), (1,)), ((), ()))


def _s5_prepare(lre_ref, lim_ref, ldt_ref, bre_ref, bim_ref, cre_ref, cim_ref,
                bmat_ref, cmat_ref, lam_ref, bstage_ref, cstage_ref):
    lam_re, lam_im = lre_ref[...], lim_ref[...]
    dt = jnp.exp(ldt_ref[...])
    ang = lam_im * dt
    mag = jnp.exp(lam_re * dt)
    bar_re, bar_im = mag * jnp.cos(ang), mag * jnp.sin(ang)
    inv = 1.0 / (lam_re * lam_re + lam_im * lam_im)
    coef_re = ((bar_re - 1.0) * lam_re + bar_im * lam_im) * inv
    coef_im = (bar_im * lam_re - (bar_re - 1.0) * lam_im) * inv
    nmag = jnp.exp(SSM_STEPS * (lam_re * dt))
    pow_re, pow_im = nmag * jnp.cos(SSM_STEPS * ang), nmag * jnp.sin(SSM_STEPS * ang)
    for k in range(N_SLABS):
        bstage_ref[...] = jnp.zeros(bstage_ref.shape, _F32)
        cstage_ref[...] = jnp.zeros(cstage_ref.shape, _F32)
        for g in range(SLAB_GROUPS):
            grp = k * SLAB_GROUPS + g
            row = slice(grp, grp + 1)
            st = slice(g * STATE, (g + 1) * STATE)
            st_im = slice(SLAB_STATE + g * STATE, SLAB_STATE + (g + 1) * STATE)
            ch = slice(g * GROUP_CH, (g + 1) * GROUP_CH)
            for r, src in enumerate((bar_re, bar_im, pow_re, pow_im)):
                lam_ref[k, r:r + 1, st] = src[row, :]
            bstage_ref[st, ch] = bre_ref[grp]
            bstage_ref[st_im, ch] = bim_ref[grp]
            c_re, c_im = cre_ref[grp], cim_ref[grp]
            cstage_ref[ch, st] = c_re * coef_re[row, :] - c_im * coef_im[row, :]
            cstage_ref[ch, st_im] = -(c_re * coef_im[row, :] + c_im * coef_re[row, :])
        bmat_ref[k] = bstage_ref[...].astype(_BF16)
        cmat_ref[k] = cstage_ref[...].astype(_BF16)


def _ssm_kernel(u_ref, lre_ref, lim_ref, ldt_ref, bre_ref, bim_ref, cre_ref, cim_ref, d_ref, y_ref,
                bu_ref, xb_ref, state_ref, bmat_ref, cmat_ref, lam_ref, bstage_ref, cstage_ref):
    nc, ns = SSM_CHUNKS, SSM_STEPS

    @pl.when(pl.program_id(0) == 0)
    def _():
        state_ref[...] = jnp.zeros(state_ref.shape, _F32)
        _s5_prepare(lre_ref, lim_ref, ldt_ref, bre_ref, bim_ref, cre_ref, cim_ref,
                    bmat_ref, cmat_ref, lam_ref, bstage_ref, cstage_ref)

    def expand(k):
        bu_ref[k] = lax.dot_general(u_ref[k].astype(_BF16), bmat_ref[k], _NT, preferred_element_type=_F32)

    expand(0)
    expand(1)
    re = slice(0, SLAB_STATE)
    im = slice(SLAB_STATE, SLAB_LANES)
    for k in range(N_SLABS):
        lr = jnp.broadcast_to(lam_ref[k, 0:1, :], (nc, SLAB_STATE))
        li = jnp.broadcast_to(lam_ref[k, 1:2, :], (nc, SLAB_STATE))

        xr = jnp.zeros((nc, SLAB_STATE), _F32)
        xi = jnp.zeros((nc, SLAB_STATE), _F32)
        for s in range(ns):
            rows = slice(s * nc, (s + 1) * nc)
            xr, xi = _cmul_add(lr, li, xr, xi, bu_ref[k, rows, re], bu_ref[k, rows, im])

        tr = lam_ref[k, 2:3, :]
        ti = lam_ref[k, 3:4, :]
        cr = state_ref[k, 0:1, :]
        ci = state_ref[k, 1:2, :]
        starts_r, starts_i = [], []
        for c in range(nc):
            starts_r.append(cr)
            starts_i.append(ci)
            cr, ci = _cmul_add(tr, ti, cr, ci, xr[c:c + 1, :], xi[c:c + 1, :])
        state_ref[k, 0:1, :] = cr
        state_ref[k, 1:2, :] = ci
        xr = jnp.concatenate(starts_r, axis=0)
        xi = jnp.concatenate(starts_i, axis=0)

        for s in range(0, ns, 2):
            parts_r, parts_i = [], []
            for q in (s, s + 1):
                rows = slice(q * nc, (q + 1) * nc)
                xr, xi = _cmul_add(lr, li, xr, xi, bu_ref[k, rows, re], bu_ref[k, rows, im])
                parts_r.append(xr)
                parts_i.append(xi)
            rows2 = slice(s * nc, (s + 2) * nc)
            xb_ref[k, rows2, re] = jnp.concatenate(parts_r, axis=0).astype(_BF16)
            xb_ref[k, rows2, im] = jnp.concatenate(parts_i, axis=0).astype(_BF16)
        if k + 2 < N_SLABS:
            expand(k + 2)
        y_ref[k] = (lax.dot_general(xb_ref[k], cmat_ref[k], _NT, preferred_element_type=_F32)
                    + d_ref[k] * u_ref[k])


def _ssm_call(u, lam_re, lam_im, log_dt, b_re, b_im, c_re, c_im, d_skip):
    tm = TM_MIX
    slab_rows = pl.BlockSpec((N_SLABS, tm, SLAB_CH), lambda i: (0, i, 0))
    return pl.pallas_call(
        _ssm_kernel,
        grid=(SEQ // tm,),
        in_specs=[
            slab_rows,
            _resident((GROUPS, STATE)),
            _resident((GROUPS, STATE)),
            _resident((GROUPS, 1)),
            _resident((GROUPS, STATE, GROUP_CH)),
            _resident((GROUPS, STATE, GROUP_CH)),
            _resident((GROUPS, GROUP_CH, STATE)),
            _resident((GROUPS, GROUP_CH, STATE)),
            _resident((N_SLABS, 1, SLAB_CH)),
        ],
        out_specs=slab_rows,
        out_shape=jax.ShapeDtypeStruct((N_SLABS, SEQ, SLAB_CH), _F32),
        scratch_shapes=[
            pltpu.VMEM((N_SLABS, tm, SLAB_LANES), _F32),
            pltpu.VMEM((N_SLABS, tm, SLAB_LANES), _BF16),
            pltpu.VMEM((N_SLABS, 2, SLAB_STATE), _F32),
            pltpu.VMEM((N_SLABS, SLAB_LANES, SLAB_CH), _BF16),
            pltpu.VMEM((N_SLABS, SLAB_CH, SLAB_LANES), _BF16),
            pltpu.VMEM((N_SLABS, 4, SLAB_STATE), _F32),
            pltpu.VMEM((SLAB_LANES, SLAB_CH), _F32),
            pltpu.VMEM((SLAB_CH, SLAB_LANES), _F32),
        ],
        compiler_params=pltpu.CompilerParams(dimension_semantics=("arbitrary",), vmem_limit_bytes=VMEM_LIMIT),
        name="s5_core",
    )(u, lam_re, lam_im, log_dt, b_re, b_im, c_re, c_im, d_skip)


def _outproj_kernel(y_ref, yc_ref, gw_ref, gb_ref, ng_ref, wt_ref, wb_ref, x_ref, g_ref, b_ref, h_ref, hb_ref):
    tm = x_ref.shape[0]
    hm = tm // MIX_ROW_SPLIT
    chunks_per_part = SSM_CHUNKS // MIX_ROW_SPLIT
    for r in range(MIX_ROW_SPLIT):
        rows = slice(r * hm, (r + 1) * hm)
        y = jnp.concatenate(
            [jnp.concatenate([y_ref[k, _chunk_rows(c), :] for k in range(N_SLABS)], axis=1)
             for c in range(r * chunks_per_part, (r + 1) * chunks_per_part)], axis=0)
        gl = jax.nn.gelu(y)
        z = jnp.dot(gl.astype(_BF16), gw_ref[...], preferred_element_type=_F32) + gb_ref[...]
        ys = _rmsnorm(gl * jax.nn.sigmoid(z), ng_ref[...]).astype(_BF16)
        mix = jnp.dot(ys, wt_ref[...], preferred_element_type=_F32)
        mix = mix + jnp.dot(yc_ref[rows, :], wb_ref[...], preferred_element_type=_F32)
        h = _layernorm(ALPHA * x_ref[rows, :] + mix, g_ref[...], b_ref[...])
        h_ref[rows, :] = h
        hb_ref[r * hm // 2:(r + 1) * hm // 2, :] = _pack_rows(h.astype(_BF16))


def _outproj_call(y, yc, glu_w_bf, glu_b, norm_g, w_out_bf, x, ln_g, ln_b):
    tm = TM_MIX
    half = lambda j: pl.BlockSpec((SSM_WIDTH, D_MODEL), lambda i, j=j: (j, 0), pipeline_mode=pl.Buffered(1))
    return pl.pallas_call(
        _outproj_kernel,
        grid=(SEQ // tm,),
        in_specs=[
            pl.BlockSpec((N_SLABS, tm, SLAB_CH), lambda i: (0, i, 0)),
            pl.BlockSpec((tm, CONV_WIDTH), lambda i: (i, 0)),
            _resident((SSM_WIDTH, SSM_WIDTH)),
            _resident((1, SSM_WIDTH)),
            _resident((1, SSM_WIDTH)),
            half(0), half(1),
            pl.BlockSpec((tm, D_MODEL), lambda i: (i, 0)),
            _resident((1, D_MODEL)),
            _resident((1, D_MODEL)),
        ],
        out_specs=[
            pl.BlockSpec((tm, D_MODEL), lambda i: (i, 0)),
            pl.BlockSpec((tm // 2, D_MODEL), lambda i: (i, 0)),
        ],
        out_shape=[
            jax.ShapeDtypeStruct((SEQ, D_MODEL), _F32),
            jax.ShapeDtypeStruct((SEQ // 2, D_MODEL), jnp.int32),
        ],
        compiler_params=pltpu.CompilerParams(dimension_semantics=("arbitrary",), vmem_limit_bytes=VMEM_LIMIT),
        name="glu_outproj_ln1",
    )(y, yc, glu_w_bf, glu_b, norm_g, w_out_bf, w_out_bf, x, ln_g, ln_b)


def _ffn_up_kernel(hb_ref, wg_ref, wu_ref, cw_ref, cb_ref, wd_ref, act_ref, wdb_ref,
                   wgb_ref, wub_ref, halo_ref, stage_ref):
    i = pl.program_id(1)
    tm = act_ref.shape[0]
    hm = tm // FFN_UP_ROW_SPLIT

    wdb_ref[...] = wd_ref[...].astype(_BF16)

    @pl.when(i == 0)
    def _():
        wgb_ref[...] = wg_ref[...].astype(_BF16)
        wub_ref[...] = wu_ref[...].astype(_BF16)

    _stage_open(stage_ref, halo_ref, i == 0)
    for r in range(FFN_UP_ROW_SPLIT):
        rows = slice(r * hm, (r + 1) * hm)
        hb = _unpack_rows(hb_ref[r * hm // 2:(r + 1) * hm // 2, :])
        gate = jnp.dot(hb, wgb_ref[...], preferred_element_type=_F32)
        up = jnp.dot(hb, wub_ref[...], preferred_element_type=_F32)
        a = _conv3(stage_ref, gate, r * hm, cw_ref) + cb_ref[...]
        act_ref[rows, :] = (jax.nn.silu(a) * up).astype(_BF16)
    _stage_close(stage_ref, halo_ref)


def _ffn_up_call(hb, w_gate, w_up, conv_w, conv_b, w_down):
    tm, tf = TM_FFN_UP, TF_FFN_UP
    nk, ni = D_FF // tf, SEQ // tm
    wd_rows = D_FF // (nk * ni)
    return pl.pallas_call(
        _ffn_up_kernel,
        grid=(nk, ni),
        in_specs=[
            pl.BlockSpec((tm // 2, D_MODEL), lambda k, i: (i, 0)),
            pl.BlockSpec((D_MODEL, tf), lambda k, i: (0, k)),
            pl.BlockSpec((D_MODEL, tf), lambda k, i: (0, k)),
            pl.BlockSpec((3, tf), lambda k, i: (0, k)),
            pl.BlockSpec((1, tf), lambda k, i: (0, k)),
            pl.BlockSpec((wd_rows, D_MODEL), lambda k, i: (k * ni + i, 0)),
        ],
        out_specs=[
            pl.BlockSpec((tm, tf), lambda k, i: (i, k)),
            pl.BlockSpec((wd_rows, D_MODEL), lambda k, i: (k * ni + i, 0)),
        ],
        out_shape=[
            jax.ShapeDtypeStruct((SEQ, D_FF), _BF16),
            jax.ShapeDtypeStruct((D_FF, D_MODEL), _BF16),
        ],
        scratch_shapes=[
            pltpu.VMEM((D_MODEL, tf), _BF16),
            pltpu.VMEM((D_MODEL, tf), _BF16),
            pltpu.VMEM((SUBLANES, tf), _F32),
            pltpu.VMEM((SUBLANES + tm, tf), _F32),
        ],
        compiler_params=pltpu.CompilerParams(dimension_semantics=("arbitrary", "arbitrary"),
                                             vmem_limit_bytes=VMEM_LIMIT),
        name="convffn_up",
    )(hb, w_gate, w_up, conv_w, conv_b, w_down)


def _ffn_down_kernel(act_ref, wd_ref, h_ref, g_ref, b_ref, o_ref):
    tm = o_ref.shape[0]
    hm = tm // FFN_ROW_SPLIT
    for r in range(FFN_ROW_SPLIT):
        rows = slice(r * hm, (r + 1) * hm)
        f = jnp.dot(act_ref[rows, :], wd_ref[...], preferred_element_type=_F32)
        o_ref[rows, :] = _layernorm(ALPHA * h_ref[rows, :] + f, g_ref[...], b_ref[...])


def _ffn_down_call(act, wd_bf, h, ln_g, ln_b):
    tm = TM_FFN_DOWN
    return pl.pallas_call(
        _ffn_down_kernel,
        grid=(SEQ // tm,),
        in_specs=[
            pl.BlockSpec((tm, D_FF), lambda i: (i, 0)),
            _resident((D_FF, D_MODEL)),
            pl.BlockSpec((tm, D_MODEL), lambda i: (i, 0)),
            _resident((1, D_MODEL)),
            _resident((1, D_MODEL)),
        ],
        out_specs=pl.BlockSpec((tm, D_MODEL), lambda i: (i, 0)),
        out_shape=jax.ShapeDtypeStruct((SEQ, D_MODEL), _F32),
        compiler_params=pltpu.CompilerParams(dimension_semantics=("arbitrary",),
                                             vmem_limit_bytes=VMEM_LIMIT_DOWN),
        name="convffn_down_ln2",
    )(act, wd_bf, h, ln_g, ln_b)


def kernel(x, w_in, ssm_lambda_re, ssm_lambda_im, ssm_log_dt, ssm_b_re, ssm_b_im, ssm_c_re, ssm_c_im,
           ssm_d, ssm_glu_w, ssm_glu_b, sconv_w, norm_ssm_g, norm_conv_g, w_out, ln1_g, ln1_b,
           ffn_w_gate, ffn_w_up, ffn_conv_w, ffn_conv_b, ffn_w_down, ln2_g, ln2_b):
    assert x.shape == (1, SEQ, D_MODEL) and w_in.shape[0] == DEPTH
    h = x[0]
    for l in range(DEPTH):
        row = lambda p: p[l].reshape(1, -1)
        u, yc, w_out_bf, glu_w_bf = _proj_call(h, w_in[l].astype(_BF16), sconv_w[l], row(norm_conv_g),
                                               w_out[l], ssm_glu_w[l])
        y = _ssm_call(u, ssm_lambda_re[l], ssm_lambda_im[l], ssm_log_dt[l].reshape(GROUPS, 1),
                      ssm_b_re[l], ssm_b_im[l], ssm_c_re[l], ssm_c_im[l], ssm_d[l].reshape(N_SLABS, 1, SLAB_CH))
        h, hb = _outproj_call(y, yc, glu_w_bf, row(ssm_glu_b), row(norm_ssm_g),
                              w_out_bf, h, row(ln1_g), row(ln1_b))
        act, w_down_bf = _ffn_up_call(hb, ffn_w_gate[l], ffn_w_up[l], ffn_conv_w[l], row(ffn_conv_b),
                                      ffn_w_down[l])
        h = _ffn_down_call(act, w_down_bf, h, row(ln2_g), row(ln2_b))
    return h[None]
```

```python
import jax
import jax.numpy as jnp
from jax import lax
from jax.experimental import pallas as pl
from jax.experimental.pallas import tpu as pltpu

SEQ = 8192
D_MODEL = 2048
SSM_WIDTH = 1024
CONV_WIDTH = 1024
GROUP_CH = 16
GROUPS = SSM_WIDTH // GROUP_CH
STATE = 64
D_FF = 5632
DEPTH = 1
LN_EPS = 1e-5
RMS_EPS = 1e-6
ALPHA = (2.0 * DEPTH) ** 0.25

SUBLANES = 8
SLAB_GROUPS = 8
N_SLABS = GROUPS // SLAB_GROUPS
SLAB_CH = SLAB_GROUPS * GROUP_CH
SLAB_STATE = SLAB_GROUPS * STATE
SLAB_LANES = 2 * SLAB_STATE

TM_MIX = 512
SSM_CHUNKS = SUBLANES
SSM_STEPS = TM_MIX // SSM_CHUNKS
MIX_ROW_SPLIT = 2
TM_FFN_UP = 1024
TF_FFN_UP = 512
TM_FFN_DOWN = 512
FFN_UP_ROW_SPLIT = 2
FFN_ROW_SPLIT = 2
VMEM_LIMIT = 56 * 1024 * 1024
VMEM_LIMIT_DOWN = 60 * 1024 * 1024

_F32 = jnp.float32
_BF16 = jnp.bfloat16


def _resident(shape):
    return pl.BlockSpec(shape, lambda *_: (0,) * len(shape), pipeline_mode=pl.Buffered(1))


def _layernorm(r, g, b):
    mu = jnp.mean(r, axis=-1, keepdims=True)
    rc = r - mu
    var = jnp.mean(rc * rc, axis=-1, keepdims=True)
    return rc * lax.rsqrt(var + LN_EPS) * g + b


def _rmsnorm(y, g):
    return y * lax.rsqrt(jnp.mean(y * y, axis=-1, keepdims=True) + RMS_EPS) * g


def _pack_rows(v):
    return pltpu.bitcast(v, jnp.int32)


def _unpack_rows(v):
    return pltpu.bitcast(v, _BF16)


def _conv3(stage_ref, cur, row0, w_ref):
    n = cur.shape[0]
    stage_ref[SUBLANES + row0:SUBLANES + row0 + n, :] = cur
    x1 = stage_ref[SUBLANES - 1 + row0:SUBLANES - 1 + row0 + n, :]
    x2 = stage_ref[SUBLANES - 2 + row0:SUBLANES - 2 + row0 + n, :]
    return w_ref[0:1, :] * x2 + w_ref[1:2, :] * x1 + w_ref[2:3, :] * cur


def _stage_open(stage_ref, halo_ref, first_tile):
    @pl.when(first_tile)
    def _():
        stage_ref[0:SUBLANES, :] = jnp.zeros((SUBLANES, stage_ref.shape[1]), _F32)

    @pl.when(jnp.logical_not(first_tile))
    def _():
        stage_ref[0:SUBLANES, :] = halo_ref[...]


def _stage_close(stage_ref, halo_ref):
    tm = stage_ref.shape[0] - SUBLANES
    halo_ref[...] = stage_ref[tm:tm + SUBLANES, :]


def _chunk_rows(c):
    return pl.ds(c, SSM_STEPS, stride=SSM_CHUNKS)


def _proj_kernel(x_ref, wu_ref, wb_ref, wc_ref, wv_ref, cw_ref, g_ref, wo_ref, gw_ref,
                 u_ref, yc_ref, wob_ref, gwb_ref, halo_ref, stage_ref):
    tm = x_ref.shape[0]
    hm = tm // MIX_ROW_SPLIT
    chunks_per_part = SSM_CHUNKS // MIX_ROW_SPLIT

    wob_ref[...] = wo_ref[...].astype(_BF16)
    gwb_ref[...] = gw_ref[...].astype(_BF16)

    _stage_open(stage_ref, halo_ref, pl.program_id(0) == 0)
    for r in range(MIX_ROW_SPLIT):
        rows = slice(r * hm, (r + 1) * hm)
        xb = x_ref[rows, :].astype(_BF16)
        u = jnp.dot(xb, wu_ref[...], preferred_element_type=_F32)
        for k in range(N_SLABS):
            for c in range(chunks_per_part):
                u_ref[k, _chunk_rows(r * chunks_per_part + c), :] = (
                    u[c * SSM_STEPS:(c + 1) * SSM_STEPS, k * SLAB_CH:(k + 1) * SLAB_CH])
        gate_c = jnp.dot(xb, wc_ref[...], preferred_element_type=_F32)
        v = jnp.dot(xb, wv_ref[...], preferred_element_type=_F32)
        conv = _conv3(stage_ref, gate_c * v, r * hm, cw_ref)
        gate_b = jnp.dot(xb, wb_ref[...], preferred_element_type=_F32)
        yc_ref[rows, :] = _rmsnorm(gate_b * conv, g_ref[...]).astype(_BF16)
    _stage_close(stage_ref, halo_ref)


def _proj_call(x, w_in_bf, sconv_w, norm_conv_g, w_out, glu_w):
    tm = TM_MIX
    nt = SEQ // tm
    wspec = lambda j: pl.BlockSpec((D_MODEL, 1024), lambda i, j=j: (0, j), pipeline_mode=pl.Buffered(1))
    wo_rows, gw_rows = w_out.shape[0] // nt, glu_w.shape[0] // nt
    return pl.pallas_call(
        _proj_kernel,
        grid=(nt,),
        in_specs=[
            pl.BlockSpec((tm, D_MODEL), lambda i: (i, 0)),
            wspec(0), wspec(1), wspec(2), wspec(3),
            _resident((3, CONV_WIDTH)),
            _resident((1, CONV_WIDTH)),
            pl.BlockSpec((wo_rows, D_MODEL), lambda i: (i, 0)),
            pl.BlockSpec((gw_rows, SSM_WIDTH), lambda i: (i, 0)),
        ],
        out_specs=[
            pl.BlockSpec((N_SLABS, tm, SLAB_CH), lambda i: (0, i, 0)),
            pl.BlockSpec((tm, CONV_WIDTH), lambda i: (i, 0)),
            pl.BlockSpec((wo_rows, D_MODEL), lambda i: (i, 0)),
            pl.BlockSpec((gw_rows, SSM_WIDTH), lambda i: (i, 0)),
        ],
        out_shape=[
            jax.ShapeDtypeStruct((N_SLABS, SEQ, SLAB_CH), _F32),
            jax.ShapeDtypeStruct((SEQ, CONV_WIDTH), _BF16),
            jax.ShapeDtypeStruct(w_out.shape, _BF16),
            jax.ShapeDtypeStruct(glu_w.shape, _BF16),
        ],
        scratch_shapes=[
            pltpu.VMEM((SUBLANES, CONV_WIDTH), _F32),
            pltpu.VMEM((SUBLANES + tm, CONV_WIDTH), _F32),
        ],
        compiler_params=pltpu.CompilerParams(dimension_semantics=("arbitrary",), vmem_limit_bytes=VMEM_LIMIT),
        name="proj_shortconv",
    )(x, w_in_bf, w_in_bf, w_in_bf, w_in_bf, sconv_w, norm_conv_g, w_out, glu_w)


def _cmul_add(ar, ai, br, bi, cr, ci):
    return ar * br - ai * bi + cr, ar * bi + ai * br + ci


_NT = (((1,), (1,)), ((), ()))
_TN = (((0,), (0,)), ((), ()))


def _s5_prepare(lre_ref, lim_ref, ldt_ref, bre_ref, bim_ref, cre_ref, cim_ref,
                bmat_ref, cmat_ref, lam_ref, bstage_ref, cstage_ref):
    lam_re, lam_im = lre_ref[...], lim_ref[...]
    dt = jnp.exp(ldt_ref[...])
    ang = lam_im * dt
    mag = jnp.exp(lam_re * dt)
    bar_re, bar_im = mag * jnp.cos(ang), mag * jnp.sin(ang)
    inv = 1.0 / (lam_re * lam_re + lam_im * lam_im)
    coef_re = ((bar_re - 1.0) * lam_re + bar_im * lam_im) * inv
    coef_im = (bar_im * lam_re - (bar_re - 1.0) * lam_im) * inv
    nmag = jnp.exp(SSM_STEPS * (lam_re * dt))
    pow_re, pow_im = nmag * jnp.cos(SSM_STEPS * ang), nmag * jnp.sin(SSM_STEPS * ang)
    eye = (lax.broadcasted_iota(jnp.int32, (SLAB_CH, SLAB_CH), 0)
           == lax.broadcasted_iota(jnp.int32, (SLAB_CH, SLAB_CH), 1)).astype(_BF16)
    for k in range(N_SLABS):
        bstage_ref[...] = jnp.zeros(bstage_ref.shape, _F32)
        cstage_ref[...] = jnp.zeros(cstage_ref.shape, _F32)
        for g in range(SLAB_GROUPS):
            grp = k * SLAB_GROUPS + g
            row = slice(grp, grp + 1)
            st = slice(g * STATE, (g + 1) * STATE)
            st_im = slice(SLAB_STATE + g * STATE, SLAB_STATE + (g + 1) * STATE)
            ch = slice(g * GROUP_CH, (g + 1) * GROUP_CH)
            for r, src in enumerate((bar_re, bar_im, pow_re, pow_im)):
                lam_ref[k, r:r + 1, st] = src[row, :]
            bstage_ref[st, ch] = bre_ref[grp]
            bstage_ref[st_im, ch] = bim_ref[grp]
            c_re, c_im = cre_ref[grp], cim_ref[grp]
            cstage_ref[ch, st] = c_re * coef_re[row, :] - c_im * coef_im[row, :]
            cstage_ref[ch, st_im] = -(c_re * coef_im[row, :] + c_im * coef_re[row, :])
        bmat_ref[k] = lax.dot_general(eye, bstage_ref[...].astype(_BF16), _NT,
                                      preferred_element_type=_F32).astype(_BF16)
        cmat_ref[k] = lax.dot_general(cstage_ref[...].astype(_BF16), eye, _TN,
                                      preferred_element_type=_F32).astype(_BF16)


def _ssm_kernel(u_ref, lre_ref, lim_ref, ldt_ref, bre_ref, bim_ref, cre_ref, cim_ref, d_ref, y_ref,
                bu_ref, xb_ref, state_ref, bmat_ref, cmat_ref, lam_ref, bstage_ref, cstage_ref):
    nc, ns = SSM_CHUNKS, SSM_STEPS

    @pl.when(pl.program_id(0) == 0)
    def _():
        state_ref[...] = jnp.zeros(state_ref.shape, _F32)
        _s5_prepare(lre_ref, lim_ref, ldt_ref, bre_ref, bim_ref, cre_ref, cim_ref,
                    bmat_ref, cmat_ref, lam_ref, bstage_ref, cstage_ref)

    def expand(k):
        bu_ref[k] = jnp.dot(u_ref[k].astype(_BF16), bmat_ref[k], preferred_element_type=_F32)

    expand(0)
    expand(1)
    re = slice(0, SLAB_STATE)
    im = slice(SLAB_STATE, SLAB_LANES)
    for k in range(N_SLABS):
        lr = jnp.broadcast_to(lam_ref[k, 0:1, :], (nc, SLAB_STATE))
        li = jnp.broadcast_to(lam_ref[k, 1:2, :], (nc, SLAB_STATE))

        xr = jnp.zeros((nc, SLAB_STATE), _F32)
        xi = jnp.zeros((nc, SLAB_STATE), _F32)
        for s in range(ns):
            rows = slice(s * nc, (s + 1) * nc)
            xr, xi = _cmul_add(lr, li, xr, xi, bu_ref[k, rows, re], bu_ref[k, rows, im])

        tr = lam_ref[k, 2:3, :]
        ti = lam_ref[k, 3:4, :]
        cr = state_ref[k, 0:1, :]
        ci = state_ref[k, 1:2, :]
        starts_r, starts_i = [], []
        for c in range(nc):
            starts_r.append(cr)
            starts_i.append(ci)
            cr, ci = _cmul_add(tr, ti, cr, ci, xr[c:c + 1, :], xi[c:c + 1, :])
        state_ref[k, 0:1, :] = cr
        state_ref[k, 1:2, :] = ci
        xr = jnp.concatenate(starts_r, axis=0)
        xi = jnp.concatenate(starts_i, axis=0)

        for s in range(0, ns, 2):
            parts_r, parts_i = [], []
            for q in (s, s + 1):
                rows = slice(q * nc, (q + 1) * nc)
                xr, xi = _cmul_add(lr, li, xr, xi, bu_ref[k, rows, re], bu_ref[k, rows, im])
                parts_r.append(xr)
                parts_i.append(xi)
            rows2 = slice(s * nc, (s + 2) * nc)
            xb_ref[k, rows2, re] = jnp.concatenate(parts_r, axis=0).astype(_BF16)
            xb_ref[k, rows2, im] = jnp.concatenate(parts_i, axis=0).astype(_BF16)
        if k + 2 < N_SLABS:
            expand(k + 2)
        y_ref[k] = (jnp.dot(xb_ref[k], cmat_ref[k], preferred_element_type=_F32)
                    + d_ref[k] * u_ref[k])


def _ssm_call(u, lam_re, lam_im, log_dt, b_re, b_im, c_re, c_im, d_skip):
    tm = TM_MIX
    slab_rows = pl.BlockSpec((N_SLABS, tm, SLAB_CH), lambda i: (0, i, 0))
    return pl.pallas_call(
        _ssm_kernel,
        grid=(SEQ // tm,),
        in_specs=[
            slab_rows,
            _resident((GROUPS, STATE)),
            _resident((GROUPS, STATE)),
            _resident((GROUPS, 1)),
            _resident((GROUPS, STATE, GROUP_CH)),
            _resident((GROUPS, STATE, GROUP_CH)),
            _resident((GROUPS, GROUP_CH, STATE)),
            _resident((GROUPS, GROUP_CH, STATE)),
            _resident((N_SLABS, 1, SLAB_CH)),
        ],
        out_specs=slab_rows,
        out_shape=jax.ShapeDtypeStruct((N_SLABS, SEQ, SLAB_CH), _F32),
        scratch_shapes=[
            pltpu.VMEM((N_SLABS, tm, SLAB_LANES), _F32),
            pltpu.VMEM((N_SLABS, tm, SLAB_LANES), _BF16),
            pltpu.VMEM((N_SLABS, 2, SLAB_STATE), _F32),
            pltpu.VMEM((N_SLABS, SLAB_CH, SLAB_LANES), _BF16),
            pltpu.VMEM((N_SLABS, SLAB_LANES, SLAB_CH), _BF16),
            pltpu.VMEM((N_SLABS, 4, SLAB_STATE), _F32),
            pltpu.VMEM((SLAB_LANES, SLAB_CH), _F32),
            pltpu.VMEM((SLAB_CH, SLAB_LANES), _F32),
        ],
        compiler_params=pltpu.CompilerParams(dimension_semantics=("arbitrary",), vmem_limit_bytes=VMEM_LIMIT),
        name="s5_core",
    )(u, lam_re, lam_im, log_dt, b_re, b_im, c_re, c_im, d_skip)


def _outproj_kernel(y_ref, yc_ref, gw_ref, gb_ref, ng_ref, wt_ref, wb_ref, x_ref, g_ref, b_ref, h_ref, hb_ref):
    tm = x_ref.shape[0]
    hm = tm // MIX_ROW_SPLIT
    chunks_per_part = SSM_CHUNKS // MIX_ROW_SPLIT
    for r in range(MIX_ROW_SPLIT):
        rows = slice(r * hm, (r + 1) * hm)
        y = jnp.concatenate(
            [jnp.concatenate([y_ref[k, _chunk_rows(c), :] for k in range(N_SLABS)], axis=1)
             for c in range(r * chunks_per_part, (r + 1) * chunks_per_part)], axis=0)
        gl = jax.nn.gelu(y)
        z = jnp.dot(gl.astype(_BF16), gw_ref[...], preferred_element_type=_F32) + gb_ref[...]
        ys = _rmsnorm(gl * jax.nn.sigmoid(z), ng_ref[...]).astype(_BF16)
        mix = jnp.dot(ys, wt_ref[...], preferred_element_type=_F32)
        mix = mix + jnp.dot(yc_ref[rows, :], wb_ref[...], preferred_element_type=_F32)
        h = _layernorm(ALPHA * x_ref[rows, :] + mix, g_ref[...], b_ref[...])
        h_ref[rows, :] = h
        hb_ref[r * hm // 2:(r + 1) * hm // 2, :] = _pack_rows(h.astype(_BF16))


def _outproj_call(y, yc, glu_w_bf, glu_b, norm_g, w_out_bf, x, ln_g, ln_b):
    tm = TM_MIX
    half = lambda j: pl.BlockSpec((SSM_WIDTH, D_MODEL), lambda i, j=j: (j, 0), pipeline_mode=pl.Buffered(1))
    return pl.pallas_call(
        _outproj_kernel,
        grid=(SEQ // tm,),
        in_specs=[
            pl.BlockSpec((N_SLABS, tm, SLAB_CH), lambda i: (0, i, 0)),
            pl.BlockSpec((tm, CONV_WIDTH), lambda i: (i, 0)),
            _resident((SSM_WIDTH, SSM_WIDTH)),
            _resident((1, SSM_WIDTH)),
            _resident((1, SSM_WIDTH)),
            half(0), half(1),
            pl.BlockSpec((tm, D_MODEL), lambda i: (i, 0)),
            _resident((1, D_MODEL)),
            _resident((1, D_MODEL)),
        ],
        out_specs=[
            pl.BlockSpec((tm, D_MODEL), lambda i: (i, 0)),
            pl.BlockSpec((tm // 2, D_MODEL), lambda i: (i, 0)),
        ],
        out_shape=[
            jax.ShapeDtypeStruct((SEQ, D_MODEL), _F32),
            jax.ShapeDtypeStruct((SEQ // 2, D_MODEL), jnp.int32),
        ],
        compiler_params=pltpu.CompilerParams(dimension_semantics=("arbitrary",), vmem_limit_bytes=VMEM_LIMIT),
        name="glu_outproj_ln1",
    )(y, yc, glu_w_bf, glu_b, norm_g, w_out_bf, w_out_bf, x, ln_g, ln_b)


def _ffn_up_kernel(hb_ref, wg_ref, wu_ref, cw_ref, cb_ref, wd_ref, act_ref, wdb_ref,
                   wgb_ref, wub_ref, halo_ref, stage_ref):
    i = pl.program_id(1)
    tm = act_ref.shape[0]
    hm = tm // FFN_UP_ROW_SPLIT

    wdb_ref[...] = wd_ref[...].astype(_BF16)

    @pl.when(i == 0)
    def _():
        wgb_ref[...] = wg_ref[...].astype(_BF16)
        wub_ref[...] = wu_ref[...].astype(_BF16)

    _stage_open(stage_ref, halo_ref, i == 0)
    for r in range(FFN_UP_ROW_SPLIT):
        rows = slice(r * hm, (r + 1) * hm)
        hb = _unpack_rows(hb_ref[r * hm // 2:(r + 1) * hm // 2, :])
        gate = jnp.dot(hb, wgb_ref[...], preferred_element_type=_F32)
        up = jnp.dot(hb, wub_ref[...], preferred_element_type=_F32)
        a = _conv3(stage_ref, gate, r * hm, cw_ref) + cb_ref[...]
        act_ref[rows, :] = (jax.nn.silu(a) * up).astype(_BF16)
    _stage_close(stage_ref, halo_ref)


def _ffn_up_call(hb, w_gate, w_up, conv_w, conv_b, w_down):
    tm, tf = TM_FFN_UP, TF_FFN_UP
    nk, ni = D_FF // tf, SEQ // tm
    wd_rows = D_FF // (nk * ni)
    return pl.pallas_call(
        _ffn_up_kernel,
        grid=(nk, ni),
        in_specs=[
            pl.BlockSpec((tm // 2, D_MODEL), lambda k, i: (i, 0)),
            pl.BlockSpec((D_MODEL, tf), lambda k, i: (0, k)),
            pl.BlockSpec((D_MODEL, tf), lambda k, i: (0, k)),
            pl.BlockSpec((3, tf), lambda k, i: (0, k)),
            pl.BlockSpec((1, tf), lambda k, i: (0, k)),
            pl.BlockSpec((wd_rows, D_MODEL), lambda k, i: (k * ni + i, 0)),
        ],
        out_specs=[
            pl.BlockSpec((tm, tf), lambda k, i: (i, k)),
            pl.BlockSpec((wd_rows, D_MODEL), lambda k, i: (k * ni + i, 0)),
        ],
        out_shape=[
            jax.ShapeDtypeStruct((SEQ, D_FF), _BF16),
            jax.ShapeDtypeStruct((D_FF, D_MODEL), _BF16),
        ],
        scratch_shapes=[
            pltpu.VMEM((D_MODEL, tf), _BF16),
            pltpu.VMEM((D_MODEL, tf), _BF16),
            pltpu.VMEM((SUBLANES, tf), _F32),
            pltpu.VMEM((SUBLANES + tm, tf), _F32),
        ],
        compiler_params=pltpu.CompilerParams(dimension_semantics=("arbitrary", "arbitrary"),
                                             vmem_limit_bytes=VMEM_LIMIT),
        name="convffn_up",
    )(hb, w_gate, w_up, conv_w, conv_b, w_down)


def _ffn_down_kernel(act_ref, wd_ref, h_ref, g_ref, b_ref, o_ref):
    tm = o_ref.shape[0]
    hm = tm // FFN_ROW_SPLIT
    for r in range(FFN_ROW_SPLIT):
        rows = slice(r * hm, (r + 1) * hm)
        f = jnp.dot(act_ref[rows, :], wd_ref[...], preferred_element_type=_F32)
        o_ref[rows, :] = _layernorm(ALPHA * h_ref[rows, :] + f, g_ref[...], b_ref[...])


def _ffn_down_call(act, wd_bf, h, ln_g, ln_b):
    tm = TM_FFN_DOWN
    return pl.pallas_call(
        _ffn_down_kernel,
        grid=(SEQ // tm,),
        in_specs=[
            pl.BlockSpec((tm, D_FF), lambda i: (i, 0)),
            _resident((D_FF, D_MODEL)),
            pl.BlockSpec((tm, D_MODEL), lambda i: (i, 0)),
            _resident((1, D_MODEL)),
            _resident((1, D_MODEL)),
        ],
        out_specs=pl.BlockSpec((tm, D_MODEL), lambda i: (i, 0)),
        out_shape=jax.ShapeDtypeStruct((SEQ, D_MODEL), _F32),
        compiler_params=pltpu.CompilerParams(dimension_semantics=("arbitrary",),
                                             vmem_limit_bytes=VMEM_LIMIT_DOWN),
        name="convffn_down_ln2",
    )(act, wd_bf, h, ln_g, ln_b)


def kernel(x, w_in, ssm_lambda_re, ssm_lambda_im, ssm_log_dt, ssm_b_re, ssm_b_im, ssm_c_re, ssm_c_im,
           ssm_d, ssm_glu_w, ssm_glu_b, sconv_w, norm_ssm_g, norm_conv_g, w_out, ln1_g, ln1_b,
           ffn_w_gate, ffn_w_up, ffn_conv_w, ffn_conv_b, ffn_w_down, ln2_g, ln2_b):
    assert x.shape == (1, SEQ, D_MODEL) and w_in.shape[0] == DEPTH
    h = x[0]
    for l in range(DEPTH):
        row = lambda p: p[l].reshape(1, -1)
        u, yc, w_out_bf, glu_w_bf = _proj_call(h, w_in[l].astype(_BF16), sconv_w[l], row(norm_conv_g),
                                               w_out[l], ssm_glu_w[l])
        y = _ssm_call(u, ssm_lambda_re[l], ssm_lambda_im[l], ssm_log_dt[l].reshape(GROUPS, 1),
                      ssm_b_re[l], ssm_b_im[l], ssm_c_re[l], ssm_c_im[l], ssm_d[l].reshape(N_SLABS, 1, SLAB_CH))
        h, hb = _outproj_call(y, yc, glu_w_bf, row(ssm_glu_b), row(norm_ssm_g),
                              w_out_bf, h, row(ln1_g), row(ln1_b))
        act, w_down_bf = _ffn_up_call(hb, ffn_w_gate[l], ffn_w_up[l], ffn_conv_w[l], row(ffn_conv_b),
                                      ffn_w_down[l])
        h = _ffn_down_call(act, w_down_bf, h, row(ln2_g), row(ln2_b))
    return h[None]
```

```python
import jax
import jax.numpy as jnp
from jax import lax
from jax.experimental import pallas as pl
from jax.experimental.pallas import tpu as pltpu

SEQ = 8192
D_MODEL = 2048
SSM_WIDTH = 1024
CONV_WIDTH = 1024
GROUP_CH = 16
GROUPS = SSM_WIDTH // GROUP_CH
STATE = 64
D_FF = 5632
DEPTH = 1
LN_EPS = 1e-5
RMS_EPS = 1e-6
ALPHA = (2.0 * DEPTH) ** 0.25

SUBLANES = 8
SLAB_GROUPS = 8
N_SLABS = GROUPS // SLAB_GROUPS
SLAB_CH = SLAB_GROUPS * GROUP_CH
SLAB_STATE = SLAB_GROUPS * STATE
SLAB_LANES = 2 * SLAB_STATE

TM_MIX = 512
SSM_CHUNKS = SUBLANES
SSM_STEPS = TM_MIX // SSM_CHUNKS
MIX_ROW_SPLIT = 2
TM_FFN_UP = 1024
TF_FFN_UP = 512
TM_FFN_DOWN = 512
FFN_UP_ROW_PARTS = (512, 512)
FFN_ROW_SPLIT = 2
VMEM_LIMIT = 56 * 1024 * 1024
VMEM_LIMIT_DOWN = 60 * 1024 * 1024

_F32 = jnp.float32
_BF16 = jnp.bfloat16


def _resident(shape):
    return pl.BlockSpec(shape, lambda *_: (0,) * len(shape), pipeline_mode=pl.Buffered(1))


def _layernorm(r, g, b):
    mu = jnp.mean(r, axis=-1, keepdims=True)
    rc = r - mu
    var = jnp.mean(rc * rc, axis=-1, keepdims=True)
    return rc * lax.rsqrt(var + LN_EPS) * g + b


def _rmsnorm(y, g):
    return y * lax.rsqrt(jnp.mean(y * y, axis=-1, keepdims=True) + RMS_EPS) * g


def _pack_rows(v):
    return pltpu.bitcast(v, jnp.int32)


def _unpack_rows(v):
    return pltpu.bitcast(v, _BF16)


def _conv3(stage_ref, cur, row0, w_ref):
    n = cur.shape[0]
    stage_ref[SUBLANES + row0:SUBLANES + row0 + n, :] = cur
    x1 = stage_ref[SUBLANES - 1 + row0:SUBLANES - 1 + row0 + n, :]
    x2 = stage_ref[SUBLANES - 2 + row0:SUBLANES - 2 + row0 + n, :]
    return w_ref[0:1, :] * x2 + w_ref[1:2, :] * x1 + w_ref[2:3, :] * cur


def _stage_open(stage_ref, halo_ref, first_tile):
    @pl.when(first_tile)
    def _():
        stage_ref[0:SUBLANES, :] = jnp.zeros((SUBLANES, stage_ref.shape[1]), _F32)

    @pl.when(jnp.logical_not(first_tile))
    def _():
        stage_ref[0:SUBLANES, :] = halo_ref[...]


def _stage_close(stage_ref, halo_ref):
    tm = stage_ref.shape[0] - SUBLANES
    halo_ref[...] = stage_ref[tm:tm + SUBLANES, :]


def _chunk_rows(c):
    return pl.ds(c, SSM_STEPS, stride=SSM_CHUNKS)


def _proj_kernel(x_ref, wu_ref, wb_ref, wc_ref, wv_ref, cw_ref, g_ref, wo_ref, gw_ref,
                 u_ref, yc_ref, wob_ref, gwb_ref, halo_ref, stage_ref):
    tm = x_ref.shape[0]
    hm = tm // MIX_ROW_SPLIT
    chunks_per_part = SSM_CHUNKS // MIX_ROW_SPLIT

    wob_ref[...] = _pack_rows(wo_ref[...].astype(_BF16))
    gwb_ref[...] = _pack_rows(gw_ref[...].astype(_BF16))

    _stage_open(stage_ref, halo_ref, pl.program_id(0) == 0)
    for r in range(MIX_ROW_SPLIT):
        rows = slice(r * hm, (r + 1) * hm)
        xb = x_ref[rows, :].astype(_BF16)
        u = jnp.dot(xb, wu_ref[...], preferred_element_type=_F32)
        for k in range(N_SLABS):
            for c in range(chunks_per_part):
                u_ref[k, _chunk_rows(r * chunks_per_part + c), :] = (
                    u[c * SSM_STEPS:(c + 1) * SSM_STEPS, k * SLAB_CH:(k + 1) * SLAB_CH])
        gate_c = jnp.dot(xb, wc_ref[...], preferred_element_type=_F32)
        v = jnp.dot(xb, wv_ref[...], preferred_element_type=_F32)
        conv = _conv3(stage_ref, gate_c * v, r * hm, cw_ref)
        gate_b = jnp.dot(xb, wb_ref[...], preferred_element_type=_F32)
        yc_ref[r * hm // 2:(r + 1) * hm // 2, :] = _pack_rows(_rmsnorm(gate_b * conv, g_ref[...]).astype(_BF16))
    _stage_close(stage_ref, halo_ref)


def _proj_call(x, w_in_bf, sconv_w, norm_conv_g, w_out, glu_w):
    tm = TM_MIX
    nt = SEQ // tm
    wspec = lambda j: pl.BlockSpec((D_MODEL, 1024), lambda i, j=j: (0, j), pipeline_mode=pl.Buffered(1))
    wo_rows, gw_rows = w_out.shape[0] // nt, glu_w.shape[0] // nt
    return pl.pallas_call(
        _proj_kernel,
        grid=(nt,),
        in_specs=[
            pl.BlockSpec((tm, D_MODEL), lambda i: (i, 0)),
            wspec(0), wspec(1), wspec(2), wspec(3),
            _resident((3, CONV_WIDTH)),
            _resident((1, CONV_WIDTH)),
            pl.BlockSpec((wo_rows, D_MODEL), lambda i: (i, 0)),
            pl.BlockSpec((gw_rows, SSM_WIDTH), lambda i: (i, 0)),
        ],
        out_specs=[
            pl.BlockSpec((N_SLABS, tm, SLAB_CH), lambda i: (0, i, 0)),
            pl.BlockSpec((tm // 2, CONV_WIDTH), lambda i: (i, 0)),
            pl.BlockSpec((wo_rows // 2, D_MODEL), lambda i: (i, 0)),
            pl.BlockSpec((gw_rows // 2, SSM_WIDTH), lambda i: (i, 0)),
        ],
        out_shape=[
            jax.ShapeDtypeStruct((N_SLABS, SEQ, SLAB_CH), _F32),
            jax.ShapeDtypeStruct((SEQ // 2, CONV_WIDTH), jnp.int32),
            jax.ShapeDtypeStruct((w_out.shape[0] // 2, D_MODEL), jnp.int32),
            jax.ShapeDtypeStruct((glu_w.shape[0] // 2, SSM_WIDTH), jnp.int32),
        ],
        scratch_shapes=[
            pltpu.VMEM((SUBLANES, CONV_WIDTH), _F32),
            pltpu.VMEM((SUBLANES + tm, CONV_WIDTH), _F32),
        ],
        compiler_params=pltpu.CompilerParams(dimension_semantics=("arbitrary",), vmem_limit_bytes=VMEM_LIMIT),
        name="proj_shortconv",
    )(x, w_in_bf, w_in_bf, w_in_bf, w_in_bf, sconv_w, norm_conv_g, w_out, glu_w)


def _cmul_add(ar, ai, br, bi, cr, ci):
    return ar * br - ai * bi + cr, ar * bi + ai * br + ci


_NT = (((1,), (1,)), ((), ()))
_TN = (((0,), (0,)), ((), ()))


def _s5_prepare(lre_ref, lim_ref, ldt_ref, bre_ref, bim_ref, cre_ref, cim_ref,
                bmat_ref, cmat_ref, lam_ref, bstage_ref, cstage_ref):
    lam_re, lam_im = lre_ref[...], lim_ref[...]
    dt = jnp.exp(ldt_ref[...])
    ang = lam_im * dt
    mag = jnp.exp(lam_re * dt)
    bar_re, bar_im = mag * jnp.cos(ang), mag * jnp.sin(ang)
    inv = 1.0 / (lam_re * lam_re + lam_im * lam_im)
    coef_re = ((bar_re - 1.0) * lam_re + bar_im * lam_im) * inv
    coef_im = (bar_im * lam_re - (bar_re - 1.0) * lam_im) * inv
    nmag = jnp.exp(SSM_STEPS * (lam_re * dt))
    pow_re, pow_im = nmag * jnp.cos(SSM_STEPS * ang), nmag * jnp.sin(SSM_STEPS * ang)
    eye = (lax.broadcasted_iota(jnp.int32, (SLAB_CH, SLAB_CH), 0)
           == lax.broadcasted_iota(jnp.int32, (SLAB_CH, SLAB_CH), 1)).astype(_BF16)
    for k in range(N_SLABS):
        bstage_ref[...] = jnp.zeros(bstage_ref.shape, _F32)
        cstage_ref[...] = jnp.zeros(cstage_ref.shape, _F32)
        for g in range(SLAB_GROUPS):
            grp = k * SLAB_GROUPS + g
            row = slice(grp, grp + 1)
            st = slice(g * STATE, (g + 1) * STATE)
            st_im = slice(SLAB_STATE + g * STATE, SLAB_STATE + (g + 1) * STATE)
            ch = slice(g * GROUP_CH, (g + 1) * GROUP_CH)
            for r, src in enumerate((bar_re, bar_im, pow_re, pow_im)):
                lam_ref[k, r:r + 1, st] = src[row, :]
            bstage_ref[st, ch] = bre_ref[grp]
            bstage_ref[st_im, ch] = bim_ref[grp]
            c_re, c_im = cre_ref[grp], cim_ref[grp]
            cstage_ref[ch, st] = c_re * coef_re[row, :] - c_im * coef_im[row, :]
            cstage_ref[ch, st_im] = -(c_re * coef_im[row, :] + c_im * coef_re[row, :])
        bmat_ref[k] = lax.dot_general(eye, bstage_ref[...].astype(_BF16), _NT,
                                      preferred_element_type=_F32).astype(_BF16)
        cmat_ref[k] = lax.dot_general(cstage_ref[...].astype(_BF16), eye, _TN,
                                      preferred_element_type=_F32).astype(_BF16)


def _ssm_kernel(u_ref, lre_ref, lim_ref, ldt_ref, bre_ref, bim_ref, cre_ref, cim_ref, d_ref, y_ref,
                bu_ref, xb_ref, state_ref, bmat_ref, cmat_ref, lam_ref, bstage_ref, cstage_ref):
    nc, ns = SSM_CHUNKS, SSM_STEPS

    @pl.when(pl.program_id(0) == 0)
    def _():
        state_ref[...] = jnp.zeros(state_ref.shape, _F32)
        _s5_prepare(lre_ref, lim_ref, ldt_ref, bre_ref, bim_ref, cre_ref, cim_ref,
                    bmat_ref, cmat_ref, lam_ref, bstage_ref, cstage_ref)

    def expand(k):
        bu_ref[k] = jnp.dot(u_ref[k].astype(_BF16), bmat_ref[k], preferred_element_type=_F32)

    expand(0)
    expand(1)
    re = slice(0, SLAB_STATE)
    im = slice(SLAB_STATE, SLAB_LANES)
    for k in range(N_SLABS):
        lr = jnp.broadcast_to(lam_ref[k, 0:1, :], (nc, SLAB_STATE))
        li = jnp.broadcast_to(lam_ref[k, 1:2, :], (nc, SLAB_STATE))

        xr = jnp.zeros((nc, SLAB_STATE), _F32)
        xi = jnp.zeros((nc, SLAB_STATE), _F32)
        for s in range(ns):
            rows = slice(s * nc, (s + 1) * nc)
            xr, xi = _cmul_add(lr, li, xr, xi, bu_ref[k, rows, re], bu_ref[k, rows, im])

        tr = lam_ref[k, 2:3, :]
        ti = lam_ref[k, 3:4, :]
        cr = state_ref[k, 0:1, :]
        ci = state_ref[k, 1:2, :]
        starts_r, starts_i = [], []
        for c in range(nc):
            starts_r.append(cr)
            starts_i.append(ci)
            cr, ci = _cmul_add(tr, ti, cr, ci, xr[c:c + 1, :], xi[c:c + 1, :])
        state_ref[k, 0:1, :] = cr
        state_ref[k, 1:2, :] = ci
        xr = jnp.concatenate(starts_r, axis=0)
        xi = jnp.concatenate(starts_i, axis=0)

        for s in range(0, ns, 2):
            parts_r, parts_i = [], []
            for q in (s, s + 1):
                rows = slice(q * nc, (q + 1) * nc)
                xr, xi = _cmul_add(lr, li, xr, xi, bu_ref[k, rows, re], bu_ref[k, rows, im])
                parts_r.append(xr)
                parts_i.append(xi)
            rows2 = slice(s * nc, (s + 2) * nc)
            xb_ref[k, rows2, re] = jnp.concatenate(parts_r, axis=0).astype(_BF16)
            xb_ref[k, rows2, im] = jnp.concatenate(parts_i, axis=0).astype(_BF16)
        if k + 2 < N_SLABS:
            expand(k + 2)
        y_ref[k] = (jnp.dot(xb_ref[k], cmat_ref[k], preferred_element_type=_F32)
                    + d_ref[k] * u_ref[k])


def _ssm_call(u, lam_re, lam_im, log_dt, b_re, b_im, c_re, c_im, d_skip):
    tm = TM_MIX
    slab_rows = pl.BlockSpec((N_SLABS, tm, SLAB_CH), lambda i: (0, i, 0))
    return pl.pallas_call(
        _ssm_kernel,
        grid=(SEQ // tm,),
        in_specs=[
            slab_rows,
            _resident((GROUPS, STATE)),
            _resident((GROUPS, STATE)),
            _resident((GROUPS, 1)),
            _resident((GROUPS, STATE, GROUP_CH)),
            _resident((GROUPS, STATE, GROUP_CH)),
            _resident((GROUPS, GROUP_CH, STATE)),
            _resident((GROUPS, GROUP_CH, STATE)),
            _resident((N_SLABS, 1, SLAB_CH)),
        ],
        out_specs=slab_rows,
        out_shape=jax.ShapeDtypeStruct((N_SLABS, SEQ, SLAB_CH), _F32),
        scratch_shapes=[
            pltpu.VMEM((N_SLABS, tm, SLAB_LANES), _F32),
            pltpu.VMEM((N_SLABS, tm, SLAB_LANES), _BF16),
            pltpu.VMEM((N_SLABS, 2, SLAB_STATE), _F32),
            pltpu.VMEM((N_SLABS, SLAB_CH, SLAB_LANES), _BF16),
            pltpu.VMEM((N_SLABS, SLAB_LANES, SLAB_CH), _BF16),
            pltpu.VMEM((N_SLABS, 4, SLAB_STATE), _F32),
            pltpu.VMEM((SLAB_LANES, SLAB_CH), _F32),
            pltpu.VMEM((SLAB_CH, SLAB_LANES), _F32),
        ],
        compiler_params=pltpu.CompilerParams(dimension_semantics=("arbitrary",), vmem_limit_bytes=VMEM_LIMIT),
        name="s5_core",
    )(u, lam_re, lam_im, log_dt, b_re, b_im, c_re, c_im, d_skip)


def _outproj_kernel(y_ref, yc_ref, gw_ref, gb_ref, ng_ref, wt_ref, wb_ref, x_ref, g_ref, b_ref, h_ref, hb_ref):
    tm = x_ref.shape[0]
    hm = tm // MIX_ROW_SPLIT
    chunks_per_part = SSM_CHUNKS // MIX_ROW_SPLIT
    for r in range(MIX_ROW_SPLIT):
        rows = slice(r * hm, (r + 1) * hm)
        y = jnp.concatenate(
            [jnp.concatenate([y_ref[k, _chunk_rows(c), :] for k in range(N_SLABS)], axis=1)
             for c in range(r * chunks_per_part, (r + 1) * chunks_per_part)], axis=0)
        gl = jax.nn.gelu(y)
        z = jnp.dot(gl.astype(_BF16), _unpack_rows(gw_ref[...]), preferred_element_type=_F32) + gb_ref[...]
        ys = _rmsnorm(gl * jax.nn.sigmoid(z), ng_ref[...]).astype(_BF16)
        mix = jnp.dot(ys, _unpack_rows(wt_ref[...]), preferred_element_type=_F32)
        yc = _unpack_rows(yc_ref[r * hm // 2:(r + 1) * hm // 2, :])
        mix = mix + jnp.dot(yc, _unpack_rows(wb_ref[...]), preferred_element_type=_F32)
        h = _layernorm(ALPHA * x_ref[rows, :] + mix, g_ref[...], b_ref[...])
        h_ref[rows, :] = h
        hb_ref[r * hm // 2:(r + 1) * hm // 2, :] = _pack_rows(h.astype(_BF16))


def _outproj_call(y, yc, glu_w_bf, glu_b, norm_g, w_out_bf, x, ln_g, ln_b):
    tm = TM_MIX
    half = lambda j: pl.BlockSpec((SSM_WIDTH // 2, D_MODEL), lambda i, j=j: (j, 0), pipeline_mode=pl.Buffered(1))
    return pl.pallas_call(
        _outproj_kernel,
        grid=(SEQ // tm,),
        in_specs=[
            pl.BlockSpec((N_SLABS, tm, SLAB_CH), lambda i: (0, i, 0)),
            pl.BlockSpec((tm // 2, CONV_WIDTH), lambda i: (i, 0)),
            _resident((SSM_WIDTH // 2, SSM_WIDTH)),
            _resident((1, SSM_WIDTH)),
            _resident((1, SSM_WIDTH)),
            half(0), half(1),
            pl.BlockSpec((tm, D_MODEL), lambda i: (i, 0)),
            _resident((1, D_MODEL)),
            _resident((1, D_MODEL)),
        ],
        out_specs=[
            pl.BlockSpec((tm, D_MODEL), lambda i: (i, 0)),
            pl.BlockSpec((tm // 2, D_MODEL), lambda i: (i, 0)),
        ],
        out_shape=[
            jax.ShapeDtypeStruct((SEQ, D_MODEL), _F32),
            jax.ShapeDtypeStruct((SEQ // 2, D_MODEL), jnp.int32),
        ],
        compiler_params=pltpu.CompilerParams(dimension_semantics=("arbitrary",), vmem_limit_bytes=VMEM_LIMIT),
        name="glu_outproj_ln1",
    )(y, yc, glu_w_bf, glu_b, norm_g, w_out_bf, w_out_bf, x, ln_g, ln_b)


def _ffn_up_kernel(hb_ref, wg_ref, wu_ref, cw_ref, cb_ref, wd_ref, act_ref, wdb_ref,
                   wgb_ref, wub_ref, halo_ref, stage_ref):
    i = pl.program_id(1)

    wdb_ref[...] = _pack_rows(wd_ref[...].astype(_BF16))

    @pl.when(i == 0)
    def _():
        wgb_ref[...] = wg_ref[...].astype(_BF16)
        wub_ref[...] = wu_ref[...].astype(_BF16)

    _stage_open(stage_ref, halo_ref, i == 0)
    row0 = 0
    for n in FFN_UP_ROW_PARTS:
        hb = _unpack_rows(hb_ref[row0 // 2:(row0 + n) // 2, :])
        gate = jnp.dot(hb, wgb_ref[...], preferred_element_type=_F32)
        up = jnp.dot(hb, wub_ref[...], preferred_element_type=_F32)
        a = _conv3(stage_ref, gate, row0, cw_ref) + cb_ref[...]
        act_ref[row0 // 2:(row0 + n) // 2, :] = _pack_rows((jax.nn.silu(a) * up).astype(_BF16))
        row0 += n
    _stage_close(stage_ref, halo_ref)


def _ffn_up_call(hb, w_gate, w_up, conv_w, conv_b, w_down):
    tm, tf = TM_FFN_UP, TF_FFN_UP
    nk, ni = D_FF // tf, SEQ // tm
    wd_rows = D_FF // (nk * ni)
    return pl.pallas_call(
        _ffn_up_kernel,
        grid=(nk, ni),
        in_specs=[
            pl.BlockSpec((tm // 2, D_MODEL), lambda k, i: (i, 0)),
            pl.BlockSpec((D_MODEL, tf), lambda k, i: (0, k)),
            pl.BlockSpec((D_MODEL, tf), lambda k, i: (0, k)),
            pl.BlockSpec((3, tf), lambda k, i: (0, k)),
            pl.BlockSpec((1, tf), lambda k, i: (0, k)),
            pl.BlockSpec((wd_rows, D_MODEL), lambda k, i: (k * ni + i, 0)),
        ],
        out_specs=[
            pl.BlockSpec((tm // 2, tf), lambda k, i: (i, k)),
            pl.BlockSpec((wd_rows // 2, D_MODEL), lambda k, i: (k * ni + i, 0)),
        ],
        out_shape=[
            jax.ShapeDtypeStruct((SEQ // 2, D_FF), jnp.int32),
            jax.ShapeDtypeStruct((D_FF // 2, D_MODEL), jnp.int32),
        ],
        scratch_shapes=[
            pltpu.VMEM((D_MODEL, tf), _BF16),
            pltpu.VMEM((D_MODEL, tf), _BF16),
            pltpu.VMEM((SUBLANES, tf), _F32),
            pltpu.VMEM((SUBLANES + tm, tf), _F32),
        ],
        compiler_params=pltpu.CompilerParams(dimension_semantics=("arbitrary", "arbitrary"),
                                             vmem_limit_bytes=VMEM_LIMIT),
        name="convffn_up",
    )(hb, w_gate, w_up, conv_w, conv_b, w_down)


def _ffn_down_kernel(act_ref, wd_ref, h_ref, g_ref, b_ref, o_ref):
    tm = o_ref.shape[0]
    hm = tm // FFN_ROW_SPLIT
    for r in range(FFN_ROW_SPLIT):
        rows = slice(r * hm, (r + 1) * hm)
        act = _unpack_rows(act_ref[r * hm // 2:(r + 1) * hm // 2, :])
        f = jnp.dot(act, _unpack_rows(wd_ref[...]), preferred_element_type=_F32)
        o_ref[rows, :] = _layernorm(ALPHA * h_ref[rows, :] + f, g_ref[...], b_ref[...])


def _ffn_down_call(act, wd_bf, h, ln_g, ln_b):
    tm = TM_FFN_DOWN
    return pl.pallas_call(
        _ffn_down_kernel,
        grid=(SEQ // tm,),
        in_specs=[
            pl.BlockSpec((tm // 2, D_FF), lambda i: (i, 0)),
            _resident((D_FF // 2, D_MODEL)),
            pl.BlockSpec((tm, D_MODEL), lambda i: (i, 0)),
            _resident((1, D_MODEL)),
            _resident((1, D_MODEL)),
        ],
        out_specs=pl.BlockSpec((tm, D_MODEL), lambda i: (i, 0)),
        out_shape=jax.ShapeDtypeStruct((SEQ, D_MODEL), _F32),
        compiler_params=pltpu.CompilerParams(dimension_semantics=("arbitrary",),
                                             vmem_limit_bytes=VMEM_LIMIT_DOWN),
        name="convffn_down_ln2",
    )(act, wd_bf, h, ln_g, ln_b)


def kernel(x, w_in, ssm_lambda_re, ssm_lambda_im, ssm_log_dt, ssm_b_re, ssm_b_im, ssm_c_re, ssm_c_im,
           ssm_d, ssm_glu_w, ssm_glu_b, sconv_w, norm_ssm_g, norm_conv_g, w_out, ln1_g, ln1_b,
           ffn_w_gate, ffn_w_up, ffn_conv_w, ffn_conv_b, ffn_w_down, ln2_g, ln2_b):
    assert x.shape == (1, SEQ, D_MODEL) and w_in.shape[0] == DEPTH
    h = x[0]
    for l in range(DEPTH):
        row = lambda p: p[l].reshape(1, -1)
        u, yc, w_out_bf, glu_w_bf = _proj_call(h, w_in[l].astype(_BF16), sconv_w[l], row(norm_conv_g),
                                               w_out[l], ssm_glu_w[l])
        y = _ssm_call(u, ssm_lambda_re[l], ssm_lambda_im[l], ssm_log_dt[l].reshape(GROUPS, 1),
                      ssm_b_re[l], ssm_b_im[l], ssm_c_re[l], ssm_c_im[l], ssm_d[l].reshape(N_SLABS, 1, SLAB_CH))
        h, hb = _outproj_call(y, yc, glu_w_bf, row(ssm_glu_b), row(norm_ssm_g),
                              w_out_bf, h, row(ln1_g), row(ln1_b))
        act, w_down_bf = _ffn_up_call(hb, ffn_w_gate[l], ffn_w_up[l], ffn_conv_w[l], row(ffn_conv_b),
                                      ffn_w_down[l])
        h = _ffn_down_call(act, w_down_bf, h, row(ln2_g), row(ln2_b))
    return h[None]
```

```python
import jax
import jax.numpy as jnp
from jax import lax
from jax.experimental import pallas as pl
from jax.experimental.pallas import tpu as pltpu

SEQ = 8192
D_MODEL = 2048
SSM_WIDTH = 1024
CONV_WIDTH = 1024
GROUP_CH = 16
GROUPS = SSM_WIDTH // GROUP_CH
STATE = 64
D_FF = 5632
DEPTH = 1
LN_EPS = 1e-5
RMS_EPS = 1e-6
ALPHA = (2.0 * DEPTH) ** 0.25

SUBLANES = 8
SLAB_GROUPS = 8
N_SLABS = GROUPS // SLAB_GROUPS
SLAB_CH = SLAB_GROUPS * GROUP_CH
SLAB_STATE = SLAB_GROUPS * STATE
SLAB_LANES = 2 * SLAB_STATE

TM_MIX = 512
SSM_CHUNKS = SUBLANES
SSM_STEPS = TM_MIX // SSM_CHUNKS
MIX_ROW_SPLIT = 2
TM_FFN_UP = 2048
TF_FFN_UP = 512
TM_FFN_DOWN = 512
FFN_UP_ROW_PARTS = (1024, 1024)
FFN_ROW_SPLIT = 2
VMEM_LIMIT = 56 * 1024 * 1024
VMEM_LIMIT_DOWN = 60 * 1024 * 1024

_F32 = jnp.float32
_BF16 = jnp.bfloat16


def _resident(shape):
    return pl.BlockSpec(shape, lambda *_: (0,) * len(shape), pipeline_mode=pl.Buffered(1))


def _layernorm(r, g, b):
    mu = jnp.mean(r, axis=-1, keepdims=True)
    rc = r - mu
    var = jnp.mean(rc * rc, axis=-1, keepdims=True)
    return rc * lax.rsqrt(var + LN_EPS) * g + b


def _rmsnorm(y, g):
    return y * lax.rsqrt(jnp.mean(y * y, axis=-1, keepdims=True) + RMS_EPS) * g


def _pack_rows(v):
    return pltpu.bitcast(v, jnp.int32)


def _unpack_rows(v):
    return pltpu.bitcast(v, _BF16)


def _conv3(stage_ref, cur, row0, w_ref):
    n = cur.shape[0]
    stage_ref[SUBLANES + row0:SUBLANES + row0 + n, :] = cur
    x1 = stage_ref[SUBLANES - 1 + row0:SUBLANES - 1 + row0 + n, :]
    x2 = stage_ref[SUBLANES - 2 + row0:SUBLANES - 2 + row0 + n, :]
    return w_ref[0:1, :] * x2 + w_ref[1:2, :] * x1 + w_ref[2:3, :] * cur


def _stage_open(stage_ref, halo_ref, first_tile):
    @pl.when(first_tile)
    def _():
        stage_ref[0:SUBLANES, :] = jnp.zeros((SUBLANES, stage_ref.shape[1]), _F32)

    @pl.when(jnp.logical_not(first_tile))
    def _():
        stage_ref[0:SUBLANES, :] = halo_ref[...]


def _stage_close(stage_ref, halo_ref):
    tm = stage_ref.shape[0] - SUBLANES
    halo_ref[...] = stage_ref[tm:tm + SUBLANES, :]


def _chunk_rows(c):
    return pl.ds(c, SSM_STEPS, stride=SSM_CHUNKS)


def _proj_kernel(x_ref, wu_ref, wb_ref, wc_ref, wv_ref, cw_ref, g_ref, wo_ref, gw_ref,
                 u_ref, yc_ref, wob_ref, gwb_ref, halo_ref, stage_ref):
    tm = x_ref.shape[0]
    hm = tm // MIX_ROW_SPLIT
    chunks_per_part = SSM_CHUNKS // MIX_ROW_SPLIT

    wob_ref[...] = _pack_rows(wo_ref[...].astype(_BF16))
    gwb_ref[...] = _pack_rows(gw_ref[...].astype(_BF16))

    _stage_open(stage_ref, halo_ref, pl.program_id(0) == 0)
    for r in range(MIX_ROW_SPLIT):
        rows = slice(r * hm, (r + 1) * hm)
        xb = x_ref[rows, :].astype(_BF16)
        u = jnp.dot(xb, wu_ref[...], preferred_element_type=_F32)
        for k in range(N_SLABS):
            for c in range(chunks_per_part):
                u_ref[k, _chunk_rows(r * chunks_per_part + c), :] = (
                    u[c * SSM_STEPS:(c + 1) * SSM_STEPS, k * SLAB_CH:(k + 1) * SLAB_CH])
        gate_c = jnp.dot(xb, wc_ref[...], preferred_element_type=_F32)
        v = jnp.dot(xb, wv_ref[...], preferred_element_type=_F32)
        conv = _conv3(stage_ref, gate_c * v, r * hm, cw_ref)
        gate_b = jnp.dot(xb, wb_ref[...], preferred_element_type=_F32)
        yc_ref[r * hm // 2:(r + 1) * hm // 2, :] = _pack_rows(_rmsnorm(gate_b * conv, g_ref[...]).astype(_BF16))
    _stage_close(stage_ref, halo_ref)


def _proj_call(x, w_in_bf, sconv_w, norm_conv_g, w_out, glu_w):
    tm = TM_MIX
    nt = SEQ // tm
    wspec = lambda j: pl.BlockSpec((D_MODEL, 1024), lambda i, j=j: (0, j), pipeline_mode=pl.Buffered(1))
    wo_rows, gw_rows = w_out.shape[0] // nt, glu_w.shape[0] // nt
    return pl.pallas_call(
        _proj_kernel,
        grid=(nt,),
        in_specs=[
            pl.BlockSpec((tm, D_MODEL), lambda i: (i, 0)),
            wspec(0), wspec(1), wspec(2), wspec(3),
            _resident((3, CONV_WIDTH)),
            _resident((1, CONV_WIDTH)),
            pl.BlockSpec((wo_rows, D_MODEL), lambda i: (i, 0)),
            pl.BlockSpec((gw_rows, SSM_WIDTH), lambda i: (i, 0)),
        ],
        out_specs=[
            pl.BlockSpec((N_SLABS, tm, SLAB_CH), lambda i: (0, i, 0)),
            pl.BlockSpec((tm // 2, CONV_WIDTH), lambda i: (i, 0)),
            pl.BlockSpec((wo_rows // 2, D_MODEL), lambda i: (i, 0)),
            pl.BlockSpec((gw_rows // 2, SSM_WIDTH), lambda i: (i, 0)),
        ],
        out_shape=[
            jax.ShapeDtypeStruct((N_SLABS, SEQ, SLAB_CH), _F32),
            jax.ShapeDtypeStruct((SEQ // 2, CONV_WIDTH), jnp.int32),
            jax.ShapeDtypeStruct((w_out.shape[0] // 2, D_MODEL), jnp.int32),
            jax.ShapeDtypeStruct((glu_w.shape[0] // 2, SSM_WIDTH), jnp.int32),
        ],
        scratch_shapes=[
            pltpu.VMEM((SUBLANES, CONV_WIDTH), _F32),
            pltpu.VMEM((SUBLANES + tm, CONV_WIDTH), _F32),
        ],
        compiler_params=pltpu.CompilerParams(dimension_semantics=("arbitrary",), vmem_limit_bytes=VMEM_LIMIT),
        name="proj_shortconv",
    )(x, w_in_bf, w_in_bf, w_in_bf, w_in_bf, sconv_w, norm_conv_g, w_out, glu_w)


def _cmul_add(ar, ai, br, bi, cr, ci):
    return ar * br - ai * bi + cr, ar * bi + ai * br + ci


_NT = (((1,), (1,)), ((), ()))
_TN = (((0,), (0,)), ((), ()))


def _s5_prepare(lre_ref, lim_ref, ldt_ref, bre_ref, bim_ref, cre_ref, cim_ref,
                bmat_ref, cmat_ref, lam_ref, bstage_ref, cstage_ref):
    lam_re, lam_im = lre_ref[...], lim_ref[...]
    dt = jnp.exp(ldt_ref[...])
    ang = lam_im * dt
    mag = jnp.exp(lam_re * dt)
    bar_re, bar_im = mag * jnp.cos(ang), mag * jnp.sin(ang)
    inv = 1.0 / (lam_re * lam_re + lam_im * lam_im)
    coef_re = ((bar_re - 1.0) * lam_re + bar_im * lam_im) * inv
    coef_im = (bar_im * lam_re - (bar_re - 1.0) * lam_im) * inv
    nmag = jnp.exp(SSM_STEPS * (lam_re * dt))
    pow_re, pow_im = nmag * jnp.cos(SSM_STEPS * ang), nmag * jnp.sin(SSM_STEPS * ang)
    eye = (lax.broadcasted_iota(jnp.int32, (SLAB_CH, SLAB_CH), 0)
           == lax.broadcasted_iota(jnp.int32, (SLAB_CH, SLAB_CH), 1)).astype(_BF16)
    for k in range(N_SLABS):
        bstage_ref[...] = jnp.zeros(bstage_ref.shape, _F32)
        cstage_ref[...] = jnp.zeros(cstage_ref.shape, _F32)
        for g in range(SLAB_GROUPS):
            grp = k * SLAB_GROUPS + g
            row = slice(grp, grp + 1)
            st = slice(g * STATE, (g + 1) * STATE)
            st_im = slice(SLAB_STATE + g * STATE, SLAB_STATE + (g + 1) * STATE)
            ch = slice(g * GROUP_CH, (g + 1) * GROUP_CH)
            for r, src in enumerate((bar_re, bar_im, pow_re, pow_im)):
                lam_ref[k, r:r + 1, st] = src[row, :]
            bstage_ref[st, ch] = bre_ref[grp]
            bstage_ref[st_im, ch] = bim_ref[grp]
            c_re, c_im = cre_ref[grp], cim_ref[grp]
            cstage_ref[ch, st] = c_re * coef_re[row, :] - c_im * coef_im[row, :]
            cstage_ref[ch, st_im] = -(c_re * coef_im[row, :] + c_im * coef_re[row, :])
        bmat_ref[k] = lax.dot_general(eye, bstage_ref[...].astype(_BF16), _NT,
                                      preferred_element_type=_F32).astype(_BF16)
        cmat_ref[k] = lax.dot_general(cstage_ref[...].astype(_BF16), eye, _TN,
                                      preferred_element_type=_F32).astype(_BF16)


def _ssm_kernel(u_ref, lre_ref, lim_ref, ldt_ref, bre_ref, bim_ref, cre_ref, cim_ref, d_ref, y_ref,
                bu_ref, xb_ref, state_ref, bmat_ref, cmat_ref, lam_ref, bstage_ref, cstage_ref):
    nc, ns = SSM_CHUNKS, SSM_STEPS

    @pl.when(pl.program_id(0) == 0)
    def _():
        state_ref[...] = jnp.zeros(state_ref.shape, _F32)
        _s5_prepare(lre_ref, lim_ref, ldt_ref, bre_ref, bim_ref, cre_ref, cim_ref,
                    bmat_ref, cmat_ref, lam_ref, bstage_ref, cstage_ref)

    def expand(k):
        bu_ref[k] = jnp.dot(u_ref[k].astype(_BF16), bmat_ref[k], preferred_element_type=_F32)

    expand(0)
    expand(1)
    re = slice(0, SLAB_STATE)
    im = slice(SLAB_STATE, SLAB_LANES)
    for k in range(N_SLABS):
        lr = jnp.broadcast_to(lam_ref[k, 0:1, :], (nc, SLAB_STATE))
        li = jnp.broadcast_to(lam_ref[k, 1:2, :], (nc, SLAB_STATE))

        xr = jnp.zeros((nc, SLAB_STATE), _F32)
        xi = jnp.zeros((nc, SLAB_STATE), _F32)
        for s in range(ns):
            rows = slice(s * nc, (s + 1) * nc)
            xr, xi = _cmul_add(lr, li, xr, xi, bu_ref[k, rows, re], bu_ref[k, rows, im])

        tr = lam_ref[k, 2:3, :]
        ti = lam_ref[k, 3:4, :]
        cr = state_ref[k, 0:1, :]
        ci = state_ref[k, 1:2, :]
        starts_r, starts_i = [], []
        for c in range(nc):
            starts_r.append(cr)
            starts_i.append(ci)
            cr, ci = _cmul_add(tr, ti, cr, ci, xr[c:c + 1, :], xi[c:c + 1, :])
        state_ref[k, 0:1, :] = cr
        state_ref[k, 1:2, :] = ci
        xr = jnp.concatenate(starts_r, axis=0)
        xi = jnp.concatenate(starts_i, axis=0)

        for s in range(0, ns, 2):
            parts_r, parts_i = [], []
            for q in (s, s + 1):
                rows = slice(q * nc, (q + 1) * nc)
                xr, xi = _cmul_add(lr, li, xr, xi, bu_ref[k, rows, re], bu_ref[k, rows, im])
                parts_r.append(xr)
                parts_i.append(xi)
            rows2 = slice(s * nc, (s + 2) * nc)
            xb_ref[k, rows2, re] = jnp.concatenate(parts_r, axis=0).astype(_BF16)
            xb_ref[k, rows2, im] = jnp.concatenate(parts_i, axis=0).astype(_BF16)
        if k + 2 < N_SLABS:
            expand(k + 2)
        y_ref[k] = (jnp.dot(xb_ref[k], cmat_ref[k], preferred_element_type=_F32)
                    + d_ref[k] * u_ref[k])


def _ssm_call(u, lam_re, lam_im, log_dt, b_re, b_im, c_re, c_im, d_skip):
    tm = TM_MIX
    slab_rows = pl.BlockSpec((N_SLABS, tm, SLAB_CH), lambda i: (0, i, 0))
    return pl.pallas_call(
        _ssm_kernel,
        grid=(SEQ // tm,),
        in_specs=[
            slab_rows,
            _resident((GROUPS, STATE)),
            _resident((GROUPS, STATE)),
            _resident((GROUPS, 1)),
            _resident((GROUPS, STATE, GROUP_CH)),
            _resident((GROUPS, STATE, GROUP_CH)),
            _resident((GROUPS, GROUP_CH, STATE)),
            _resident((GROUPS, GROUP_CH, STATE)),
            _resident((N_SLABS, 1, SLAB_CH)),
        ],
        out_specs=slab_rows,
        out_shape=jax.ShapeDtypeStruct((N_SLABS, SEQ, SLAB_CH), _F32),
        scratch_shapes=[
            pltpu.VMEM((N_SLABS, tm, SLAB_LANES), _F32),
            pltpu.VMEM((N_SLABS, tm, SLAB_LANES), _BF16),
            pltpu.VMEM((N_SLABS, 2, SLAB_STATE), _F32),
            pltpu.VMEM((N_SLABS, SLAB_CH, SLAB_LANES), _BF16),
            pltpu.VMEM((N_SLABS, SLAB_LANES, SLAB_CH), _BF16),
            pltpu.VMEM((N_SLABS, 4, SLAB_STATE), _F32),
            pltpu.VMEM((SLAB_LANES, SLAB_CH), _F32),
            pltpu.VMEM((SLAB_CH, SLAB_LANES), _F32),
        ],
        compiler_params=pltpu.CompilerParams(dimension_semantics=("arbitrary",), vmem_limit_bytes=VMEM_LIMIT),
        name="s5_core",
    )(u, lam_re, lam_im, log_dt, b_re, b_im, c_re, c_im, d_skip)


def _outproj_kernel(y_ref, yc_ref, gw_ref, gb_ref, ng_ref, wt_ref, wb_ref, x_ref, g_ref, b_ref, h_ref, hb_ref):
    tm = x_ref.shape[0]
    hm = tm // MIX_ROW_SPLIT
    chunks_per_part = SSM_CHUNKS // MIX_ROW_SPLIT
    for r in range(MIX_ROW_SPLIT):
        rows = slice(r * hm, (r + 1) * hm)
        y = jnp.concatenate(
            [jnp.concatenate([y_ref[k, _chunk_rows(c), :] for k in range(N_SLABS)], axis=1)
             for c in range(r * chunks_per_part, (r + 1) * chunks_per_part)], axis=0)
        gl = jax.nn.gelu(y)
        z = jnp.dot(gl.astype(_BF16), _unpack_rows(gw_ref[...]), preferred_element_type=_F32) + gb_ref[...]
        ys = _rmsnorm(gl * jax.nn.sigmoid(z), ng_ref[...]).astype(_BF16)
        mix = jnp.dot(ys, _unpack_rows(wt_ref[...]), preferred_element_type=_F32)
        yc = _unpack_rows(yc_ref[r * hm // 2:(r + 1) * hm // 2, :])
        mix = mix + jnp.dot(yc, _unpack_rows(wb_ref[...]), preferred_element_type=_F32)
        h = _layernorm(ALPHA * x_ref[rows, :] + mix, g_ref[...], b_ref[...])
        h_ref[rows, :] = h
        hb_ref[r * hm // 2:(r + 1) * hm // 2, :] = _pack_rows(h.astype(_BF16))


def _outproj_call(y, yc, glu_w_bf, glu_b, norm_g, w_out_bf, x, ln_g, ln_b):
    tm = TM_MIX
    half = lambda j: pl.BlockSpec((SSM_WIDTH // 2, D_MODEL), lambda i, j=j: (j, 0), pipeline_mode=pl.Buffered(1))
    return pl.pallas_call(
        _outproj_kernel,
        grid=(SEQ // tm,),
        in_specs=[
            pl.BlockSpec((N_SLABS, tm, SLAB_CH), lambda i: (0, i, 0)),
            pl.BlockSpec((tm // 2, CONV_WIDTH), lambda i: (i, 0)),
            _resident((SSM_WIDTH // 2, SSM_WIDTH)),
            _resident((1, SSM_WIDTH)),
            _resident((1, SSM_WIDTH)),
            half(0), half(1),
            pl.BlockSpec((tm, D_MODEL), lambda i: (i, 0)),
            _resident((1, D_MODEL)),
            _resident((1, D_MODEL)),
        ],
        out_specs=[
            pl.BlockSpec((tm, D_MODEL), lambda i: (i, 0)),
            pl.BlockSpec((tm // 2, D_MODEL), lambda i: (i, 0)),
        ],
        out_shape=[
            jax.ShapeDtypeStruct((SEQ, D_MODEL), _F32),
            jax.ShapeDtypeStruct((SEQ // 2, D_MODEL), jnp.int32),
        ],
        compiler_params=pltpu.CompilerParams(dimension_semantics=("arbitrary",), vmem_limit_bytes=VMEM_LIMIT),
        name="glu_outproj_ln1",
    )(y, yc, glu_w_bf, glu_b, norm_g, w_out_bf, w_out_bf, x, ln_g, ln_b)


def _ffn_up_kernel(hb_ref, wg_ref, wu_ref, cw_ref, cb_ref, wd_ref, act_ref, wdb_ref,
                   wgb_ref, wub_ref, halo_ref, stage_ref):
    i = pl.program_id(1)

    wdb_ref[...] = _pack_rows(wd_ref[...].astype(_BF16))

    @pl.when(i == 0)
    def _():
        wgb_ref[...] = wg_ref[...].astype(_BF16)
        wub_ref[...] = wu_ref[...].astype(_BF16)

    _stage_open(stage_ref, halo_ref, i == 0)
    row0 = 0
    for n in FFN_UP_ROW_PARTS:
        hb = _unpack_rows(hb_ref[row0 // 2:(row0 + n) // 2, :])
        gate = jnp.dot(hb, wgb_ref[...], preferred_element_type=_F32)
        up = jnp.dot(hb, wub_ref[...], preferred_element_type=_F32)
        a = _conv3(stage_ref, gate, row0, cw_ref) + cb_ref[...]
        act_ref[row0 // 2:(row0 + n) // 2, :] = _pack_rows((jax.nn.silu(a) * up).astype(_BF16))
        row0 += n
    _stage_close(stage_ref, halo_ref)


def _ffn_up_call(hb, w_gate, w_up, conv_w, conv_b, w_down):
    tm, tf = TM_FFN_UP, TF_FFN_UP
    nk, ni = D_FF // tf, SEQ // tm
    wd_rows = D_FF // (nk * ni)
    return pl.pallas_call(
        _ffn_up_kernel,
        grid=(nk, ni),
        in_specs=[
            pl.BlockSpec((tm // 2, D_MODEL), lambda k, i: (i, 0)),
            pl.BlockSpec((D_MODEL, tf), lambda k, i: (0, k)),
            pl.BlockSpec((D_MODEL, tf), lambda k, i: (0, k)),
            pl.BlockSpec((3, tf), lambda k, i: (0, k)),
            pl.BlockSpec((1, tf), lambda k, i: (0, k)),
            pl.BlockSpec((wd_rows, D_MODEL), lambda k, i: (k * ni + i, 0)),
        ],
        out_specs=[
            pl.BlockSpec((tm // 2, tf), lambda k, i: (i, k)),
            pl.BlockSpec((wd_rows // 2, D_MODEL), lambda k, i: (k * ni + i, 0)),
        ],
        out_shape=[
            jax.ShapeDtypeStruct((SEQ // 2, D_FF), jnp.int32),
            jax.ShapeDtypeStruct((D_FF // 2, D_MODEL), jnp.int32),
        ],
        scratch_shapes=[
            pltpu.VMEM((D_MODEL, tf), _BF16),
            pltpu.VMEM((D_MODEL, tf), _BF16),
            pltpu.VMEM((SUBLANES, tf), _F32),
            pltpu.VMEM((SUBLANES + tm, tf), _F32),
        ],
        compiler_params=pltpu.CompilerParams(dimension_semantics=("arbitrary", "arbitrary"),
                                             vmem_limit_bytes=VMEM_LIMIT),
        name="convffn_up",
    )(hb, w_gate, w_up, conv_w, conv_b, w_down)


def _ffn_down_kernel(act_ref, wd_ref, h_ref, g_ref, b_ref, o_ref):
    tm = o_ref.shape[0]
    hm = tm // FFN_ROW_SPLIT
    for r in range(FFN_ROW_SPLIT):
        rows = slice(r * hm, (r + 1) * hm)
        act = _unpack_rows(act_ref[r * hm // 2:(r + 1) * hm // 2, :])
        f = jnp.dot(act, _unpack_rows(wd_ref[...]), preferred_element_type=_F32)
        o_ref[rows, :] = _layernorm(ALPHA * h_ref[rows, :] + f, g_ref[...], b_ref[...])


def _ffn_down_call(act, wd_bf, h, ln_g, ln_b):
    tm = TM_FFN_DOWN
    return pl.pallas_call(
        _ffn_down_kernel,
        grid=(SEQ // tm,),
        in_specs=[
            pl.BlockSpec((tm // 2, D_FF), lambda i: (i, 0)),
            _resident((D_FF // 2, D_MODEL)),
            pl.BlockSpec((tm, D_MODEL), lambda i: (i, 0)),
            _resident((1, D_MODEL)),
            _resident((1, D_MODEL)),
        ],
        out_specs=pl.BlockSpec((tm, D_MODEL), lambda i: (i, 0)),
        out_shape=jax.ShapeDtypeStruct((SEQ, D_MODEL), _F32),
        compiler_params=pltpu.CompilerParams(dimension_semantics=("arbitrary",),
                                             vmem_limit_bytes=VMEM_LIMIT_DOWN),
        name="convffn_down_ln2",
    )(act, wd_bf, h, ln_g, ln_b)


def kernel(x, w_in, ssm_lambda_re, ssm_lambda_im, ssm_log_dt, ssm_b_re, ssm_b_im, ssm_c_re, ssm_c_im,
           ssm_d, ssm_glu_w, ssm_glu_b, sconv_w, norm_ssm_g, norm_conv_g, w_out, ln1_g, ln1_b,
           ffn_w_gate, ffn_w_up, ffn_conv_w, ffn_conv_b, ffn_w_down, ln2_g, ln2_b):
    assert x.shape == (1, SEQ, D_MODEL) and w_in.shape[0] == DEPTH
    h = x[0]
    for l in range(DEPTH):
        row = lambda p: p[l].reshape(1, -1)
        u, yc, w_out_bf, glu_w_bf = _proj_call(h, w_in[l].astype(_BF16), sconv_w[l], row(norm_conv_g),
                                               w_out[l], ssm_glu_w[l])
        y = _ssm_call(u, ssm_lambda_re[l], ssm_lambda_im[l], ssm_log_dt[l].reshape(GROUPS, 1),
                      ssm_b_re[l], ssm_b_im[l], ssm_c_re[l], ssm_c_im[l], ssm_d[l].reshape(N_SLABS, 1, SLAB_CH))
        h, hb = _outproj_call(y, yc, glu_w_bf, row(ssm_glu_b), row(norm_ssm_g),
                              w_out_bf, h, row(ln1_g), row(ln1_b))
        act, w_down_bf = _ffn_up_call(hb, ffn_w_gate[l], ffn_w_up[l], ffn_conv_w[l], row(ffn_conv_b),
                                      ffn_w_down[l])
        h = _ffn_down_call(act, w_down_bf, h, row(ln2_g), row(ln2_b))
    return h[None]
```

```python
import jax
import jax.numpy as jnp
from jax import lax
from jax.experimental import pallas as pl
from jax.experimental.pallas import tpu as pltpu

SEQ = 8192
D_MODEL = 2048
SSM_WIDTH = 1024
CONV_WIDTH = 1024
GROUP_CH = 16
GROUPS = SSM_WIDTH // GROUP_CH
STATE = 64
D_FF = 5632
DEPTH = 1
LN_EPS = 1e-5
RMS_EPS = 1e-6
ALPHA = (2.0 * DEPTH) ** 0.25

SUBLANES = 8
SLAB_GROUPS = 8
N_SLABS = GROUPS // SLAB_GROUPS
SLAB_CH = SLAB_GROUPS * GROUP_CH
SLAB_STATE = SLAB_GROUPS * STATE
SLAB_LANES = 2 * SLAB_STATE

TM_MIX = 512
SSM_CHUNKS = SUBLANES
SSM_STEPS = TM_MIX // SSM_CHUNKS
MIX_ROW_SPLIT = 2
TM_FFN_UP = 2048
TF_FFN_UP = 512
TM_FFN_DOWN = 512
FFN_UP_ROW_PARTS = (1024, 1024)
VMEM_LIMIT = 56 * 1024 * 1024
VMEM_LIMIT_DOWN = 60 * 1024 * 1024

_F32 = jnp.float32
_BF16 = jnp.bfloat16


def _resident(shape):
    return pl.BlockSpec(shape, lambda *_: (0,) * len(shape), pipeline_mode=pl.Buffered(1))


def _layernorm(r, g, b):
    mu = jnp.mean(r, axis=-1, keepdims=True)
    rc = r - mu
    var = jnp.mean(rc * rc, axis=-1, keepdims=True)
    return rc * lax.rsqrt(var + LN_EPS) * g + b


def _rmsnorm(y, g):
    return y * lax.rsqrt(jnp.mean(y * y, axis=-1, keepdims=True) + RMS_EPS) * g


def _pack_rows(v):
    return pltpu.bitcast(v, jnp.int32)


def _unpack_rows(v):
    return pltpu.bitcast(v, _BF16)


def _conv3(stage_ref, cur, row0, w_ref):
    n = cur.shape[0]
    stage_ref[SUBLANES + row0:SUBLANES + row0 + n, :] = cur
    x1 = stage_ref[SUBLANES - 1 + row0:SUBLANES - 1 + row0 + n, :]
    x2 = stage_ref[SUBLANES - 2 + row0:SUBLANES - 2 + row0 + n, :]
    return w_ref[0:1, :] * x2 + w_ref[1:2, :] * x1 + w_ref[2:3, :] * cur


def _stage_open(stage_ref, halo_ref, first_tile):
    @pl.when(first_tile)
    def _():
        stage_ref[0:SUBLANES, :] = jnp.zeros((SUBLANES, stage_ref.shape[1]), _F32)

    @pl.when(jnp.logical_not(first_tile))
    def _():
        stage_ref[0:SUBLANES, :] = halo_ref[...]


def _stage_close(stage_ref, halo_ref):
    tm = stage_ref.shape[0] - SUBLANES
    halo_ref[...] = stage_ref[tm:tm + SUBLANES, :]


def _chunk_rows(c):
    return pl.ds(c, SSM_STEPS, stride=SSM_CHUNKS)


def _proj_kernel(x_ref, wu_ref, wb_ref, wc_ref, wv_ref, cw_ref, g_ref, wo_ref, gw_ref,
                 u_ref, yc_ref, wob_ref, gwb_ref, halo_ref, stage_ref):
    wob_ref[...] = _pack_rows(wo_ref[...].astype(_BF16))
    gwb_ref[...] = _pack_rows(gw_ref[...].astype(_BF16))

    _stage_open(stage_ref, halo_ref, pl.program_id(0) == 0)
    xb = x_ref[...].astype(_BF16)
    gate_c = jnp.dot(xb, wc_ref[...], preferred_element_type=_F32)
    v = jnp.dot(xb, wv_ref[...], preferred_element_type=_F32)
    conv = _conv3(stage_ref, gate_c * v, 0, cw_ref)
    _stage_close(stage_ref, halo_ref)
    gate_b = jnp.dot(xb, wb_ref[...], preferred_element_type=_F32)
    yc_ref[...] = _pack_rows(_rmsnorm(gate_b * conv, g_ref[...]).astype(_BF16))
    u = jnp.dot(xb, wu_ref[...], preferred_element_type=_F32)
    for k in range(N_SLABS):
        for c in range(SSM_CHUNKS):
            u_ref[k, _chunk_rows(c), :] = u[c * SSM_STEPS:(c + 1) * SSM_STEPS, k * SLAB_CH:(k + 1) * SLAB_CH]


def _proj_call(x, w_in_bf, sconv_w, norm_conv_g, w_out, glu_w):
    tm = TM_MIX
    nt = SEQ // tm
    wspec = lambda j: pl.BlockSpec((D_MODEL, 1024), lambda i, j=j: (0, j), pipeline_mode=pl.Buffered(1))
    wo_rows, gw_rows = w_out.shape[0] // nt, glu_w.shape[0] // nt
    return pl.pallas_call(
        _proj_kernel,
        grid=(nt,),
        in_specs=[
            pl.BlockSpec((tm, D_MODEL), lambda i: (i, 0)),
            wspec(0), wspec(1), wspec(2), wspec(3),
            _resident((3, CONV_WIDTH)),
            _resident((1, CONV_WIDTH)),
            pl.BlockSpec((wo_rows, D_MODEL), lambda i: (i, 0)),
            pl.BlockSpec((gw_rows, SSM_WIDTH), lambda i: (i, 0)),
        ],
        out_specs=[
            pl.BlockSpec((N_SLABS, tm, SLAB_CH), lambda i: (0, i, 0)),
            pl.BlockSpec((tm // 2, CONV_WIDTH), lambda i: (i, 0)),
            pl.BlockSpec((wo_rows // 2, D_MODEL), lambda i: (i, 0)),
            pl.BlockSpec((gw_rows // 2, SSM_WIDTH), lambda i: (i, 0)),
        ],
        out_shape=[
            jax.ShapeDtypeStruct((N_SLABS, SEQ, SLAB_CH), _F32),
            jax.ShapeDtypeStruct((SEQ // 2, CONV_WIDTH), jnp.int32),
            jax.ShapeDtypeStruct((w_out.shape[0] // 2, D_MODEL), jnp.int32),
            jax.ShapeDtypeStruct((glu_w.shape[0] // 2, SSM_WIDTH), jnp.int32),
        ],
        scratch_shapes=[
            pltpu.VMEM((SUBLANES, CONV_WIDTH), _F32),
            pltpu.VMEM((SUBLANES + tm, CONV_WIDTH), _F32),
        ],
        compiler_params=pltpu.CompilerParams(dimension_semantics=("arbitrary",), vmem_limit_bytes=VMEM_LIMIT),
        name="proj_shortconv",
    )(x, w_in_bf, w_in_bf, w_in_bf, w_in_bf, sconv_w, norm_conv_g, w_out, glu_w)


def _cmul_add(ar, ai, br, bi, cr, ci):
    return ar * br - ai * bi + cr, ar * bi + ai * br + ci


_NT = (((1,), (1,)), ((), ()))
_TN = (((0,), (0,)), ((), ()))


def _s5_prepare(lre_ref, lim_ref, ldt_ref, bre_ref, bim_ref, cre_ref, cim_ref,
                bmat_ref, cmat_ref, lam_ref, bstage_ref, cstage_ref):
    lam_re, lam_im = lre_ref[...], lim_ref[...]
    dt = jnp.exp(ldt_ref[...])
    ang = lam_im * dt
    mag = jnp.exp(lam_re * dt)
    bar_re, bar_im = mag * jnp.cos(ang), mag * jnp.sin(ang)
    inv = 1.0 / (lam_re * lam_re + lam_im * lam_im)
    coef_re = ((bar_re - 1.0) * lam_re + bar_im * lam_im) * inv
    coef_im = (bar_im * lam_re - (bar_re - 1.0) * lam_im) * inv
    nmag = jnp.exp(SSM_STEPS * (lam_re * dt))
    pow_re, pow_im = nmag * jnp.cos(SSM_STEPS * ang), nmag * jnp.sin(SSM_STEPS * ang)
    eye = (lax.broadcasted_iota(jnp.int32, (SLAB_CH, SLAB_CH), 0)
           == lax.broadcasted_iota(jnp.int32, (SLAB_CH, SLAB_CH), 1)).astype(_BF16)
    for k in range(N_SLABS):
        bstage_ref[...] = jnp.zeros(bstage_ref.shape, _F32)
        cstage_ref[...] = jnp.zeros(cstage_ref.shape, _F32)
        for g in range(SLAB_GROUPS):
            grp = k * SLAB_GROUPS + g
            row = slice(grp, grp + 1)
            st = slice(g * STATE, (g + 1) * STATE)
            st_im = slice(SLAB_STATE + g * STATE, SLAB_STATE + (g + 1) * STATE)
            ch = slice(g * GROUP_CH, (g + 1) * GROUP_CH)
            for r, src in enumerate((bar_re, bar_im, pow_re, pow_im)):
                lam_ref[k, r:r + 1, st] = src[row, :]
            bstage_ref[st, ch] = bre_ref[grp]
            bstage_ref[st_im, ch] = bim_ref[grp]
            c_re, c_im = cre_ref[grp], cim_ref[grp]
            cstage_ref[ch, st] = c_re * coef_re[row, :] - c_im * coef_im[row, :]
            cstage_ref[ch, st_im] = -(c_re * coef_im[row, :] + c_im * coef_re[row, :])
        bmat_ref[k] = lax.dot_general(eye, bstage_ref[...].astype(_BF16), _NT,
                                      preferred_element_type=_F32).astype(_BF16)
        cmat_ref[k] = lax.dot_general(cstage_ref[...].astype(_BF16), eye, _TN,
                                      preferred_element_type=_F32).astype(_BF16)


def _ssm_kernel(u_ref, lre_ref, lim_ref, ldt_ref, bre_ref, bim_ref, cre_ref, cim_ref, d_ref, y_ref,
                bu_ref, xb_ref, state_ref, bmat_ref, cmat_ref, lam_ref, bstage_ref, cstage_ref):
    nc, ns = SSM_CHUNKS, SSM_STEPS

    @pl.when(pl.program_id(0) == 0)
    def _():
        state_ref[...] = jnp.zeros(state_ref.shape, _F32)
        _s5_prepare(lre_ref, lim_ref, ldt_ref, bre_ref, bim_ref, cre_ref, cim_ref,
                    bmat_ref, cmat_ref, lam_ref, bstage_ref, cstage_ref)

    def expand(k):
        bu_ref[k] = jnp.dot(u_ref[k].astype(_BF16), bmat_ref[k], preferred_element_type=_F32)

    expand(0)
    expand(1)
    re = slice(0, SLAB_STATE)
    im = slice(SLAB_STATE, SLAB_LANES)
    for k in range(N_SLABS):
        lr = jnp.broadcast_to(lam_ref[k, 0:1, :], (nc, SLAB_STATE))
        li = jnp.broadcast_to(lam_ref[k, 1:2, :], (nc, SLAB_STATE))

        xr = jnp.zeros((nc, SLAB_STATE), _F32)
        xi = jnp.zeros((nc, SLAB_STATE), _F32)
        for s in range(ns):
            rows = slice(s * nc, (s + 1) * nc)
            xr, xi = _cmul_add(lr, li, xr, xi, bu_ref[k, rows, re], bu_ref[k, rows, im])

        tr = lam_ref[k, 2:3, :]
        ti = lam_ref[k, 3:4, :]
        cr = state_ref[k, 0:1, :]
        ci = state_ref[k, 1:2, :]
        starts_r, starts_i = [], []
        for c in range(nc):
            starts_r.append(cr)
            starts_i.append(ci)
            cr, ci = _cmul_add(tr, ti, cr, ci, xr[c:c + 1, :], xi[c:c + 1, :])
        state_ref[k, 0:1, :] = cr
        state_ref[k, 1:2, :] = ci
        xr = jnp.concatenate(starts_r, axis=0)
        xi = jnp.concatenate(starts_i, axis=0)

        for s in range(0, ns, 2):
            parts_r, parts_i = [], []
            for q in (s, s + 1):
                rows = slice(q * nc, (q + 1) * nc)
                xr, xi = _cmul_add(lr, li, xr, xi, bu_ref[k, rows, re], bu_ref[k, rows, im])
                parts_r.append(xr)
                parts_i.append(xi)
            rows2 = slice(s * nc, (s + 2) * nc)
            xb_ref[k, rows2, re] = jnp.concatenate(parts_r, axis=0).astype(_BF16)
            xb_ref[k, rows2, im] = jnp.concatenate(parts_i, axis=0).astype(_BF16)
        if k + 2 < N_SLABS:
            expand(k + 2)
        y_ref[k] = (jnp.dot(xb_ref[k], cmat_ref[k], preferred_element_type=_F32)
                    + d_ref[k] * u_ref[k])


def _ssm_call(u, lam_re, lam_im, log_dt, b_re, b_im, c_re, c_im, d_skip):
    tm = TM_MIX
    slab_rows = pl.BlockSpec((N_SLABS, tm, SLAB_CH), lambda i: (0, i, 0))
    return pl.pallas_call(
        _ssm_kernel,
        grid=(SEQ // tm,),
        in_specs=[
            slab_rows,
            _resident((GROUPS, STATE)),
            _resident((GROUPS, STATE)),
            _resident((GROUPS, 1)),
            _resident((GROUPS, STATE, GROUP_CH)),
            _resident((GROUPS, STATE, GROUP_CH)),
            _resident((GROUPS, GROUP_CH, STATE)),
            _resident((GROUPS, GROUP_CH, STATE)),
            _resident((N_SLABS, 1, SLAB_CH)),
        ],
        out_specs=slab_rows,
        out_shape=jax.ShapeDtypeStruct((N_SLABS, SEQ, SLAB_CH), _F32),
        scratch_shapes=[
            pltpu.VMEM((N_SLABS, tm, SLAB_LANES), _F32),
            pltpu.VMEM((N_SLABS, tm, SLAB_LANES), _BF16),
            pltpu.VMEM((N_SLABS, 2, SLAB_STATE), _F32),
            pltpu.VMEM((N_SLABS, SLAB_CH, SLAB_LANES), _BF16),
            pltpu.VMEM((N_SLABS, SLAB_LANES, SLAB_CH), _BF16),
            pltpu.VMEM((N_SLABS, 4, SLAB_STATE), _F32),
            pltpu.VMEM((SLAB_LANES, SLAB_CH), _F32),
            pltpu.VMEM((SLAB_CH, SLAB_LANES), _F32),
        ],
        compiler_params=pltpu.CompilerParams(dimension_semantics=("arbitrary",), vmem_limit_bytes=VMEM_LIMIT),
        name="s5_core",
    )(u, lam_re, lam_im, log_dt, b_re, b_im, c_re, c_im, d_skip)


def _outproj_kernel(y_ref, yc_ref, gw_ref, gb_ref, ng_ref, wt_ref, wb_ref, x_ref, g_ref, b_ref, h_ref, hb_ref):
    tm = x_ref.shape[0]
    hm = tm // MIX_ROW_SPLIT
    chunks_per_part = SSM_CHUNKS // MIX_ROW_SPLIT
    for r in range(MIX_ROW_SPLIT):
        rows = slice(r * hm, (r + 1) * hm)
        y = jnp.concatenate(
            [jnp.concatenate([y_ref[k, _chunk_rows(c), :] for k in range(N_SLABS)], axis=1)
             for c in range(r * chunks_per_part, (r + 1) * chunks_per_part)], axis=0)
        gl = jax.nn.gelu(y)
        z = jnp.dot(gl.astype(_BF16), _unpack_rows(gw_ref[...]), preferred_element_type=_F32) + gb_ref[...]
        ys = _rmsnorm(gl * jax.nn.sigmoid(z), ng_ref[...]).astype(_BF16)
        mix = jnp.dot(ys, _unpack_rows(wt_ref[...]), preferred_element_type=_F32)
        yc = _unpack_rows(yc_ref[r * hm // 2:(r + 1) * hm // 2, :])
        mix = mix + jnp.dot(yc, _unpack_rows(wb_ref[...]), preferred_element_type=_F32)
        h = _layernorm(ALPHA * x_ref[rows, :] + mix, g_ref[...], b_ref[...])
        h_ref[rows, :] = h
        hb_ref[r * hm // 2:(r + 1) * hm // 2, :] = _pack_rows(h.astype(_BF16))


def _outproj_call(y, yc, glu_w_bf, glu_b, norm_g, w_out_bf, x, ln_g, ln_b):
    tm = TM_MIX
    half = lambda j: pl.BlockSpec((SSM_WIDTH // 2, D_MODEL), lambda i, j=j: (j, 0), pipeline_mode=pl.Buffered(1))
    return pl.pallas_call(
        _outproj_kernel,
        grid=(SEQ // tm,),
        in_specs=[
            pl.BlockSpec((N_SLABS, tm, SLAB_CH), lambda i: (0, i, 0)),
            pl.BlockSpec((tm // 2, CONV_WIDTH), lambda i: (i, 0)),
            _resident((SSM_WIDTH // 2, SSM_WIDTH)),
            _resident((1, SSM_WIDTH)),
            _resident((1, SSM_WIDTH)),
            half(0), half(1),
            pl.BlockSpec((tm, D_MODEL), lambda i: (i, 0)),
            _resident((1, D_MODEL)),
            _resident((1, D_MODEL)),
        ],
        out_specs=[
            pl.BlockSpec((tm, D_MODEL), lambda i: (i, 0)),
            pl.BlockSpec((tm // 2, D_MODEL), lambda i: (i, 0)),
        ],
        out_shape=[
            jax.ShapeDtypeStruct((SEQ, D_MODEL), _F32),
            jax.ShapeDtypeStruct((SEQ // 2, D_MODEL), jnp.int32),
        ],
        compiler_params=pltpu.CompilerParams(dimension_semantics=("arbitrary",), vmem_limit_bytes=VMEM_LIMIT),
        name="glu_outproj_ln1",
    )(y, yc, glu_w_bf, glu_b, norm_g, w_out_bf, w_out_bf, x, ln_g, ln_b)


def _ffn_up_kernel(hb_ref, wg_ref, wu_ref, cw_ref, cb_ref, wd_ref, act_ref, wdb_ref,
                   wgb_ref, wub_ref, halo_ref, stage_ref):
    i = pl.program_id(1)

    wdb_ref[...] = _pack_rows(wd_ref[...].astype(_BF16))

    @pl.when(i == 0)
    def _():
        wgb_ref[...] = wg_ref[...].astype(_BF16)
        wub_ref[...] = wu_ref[...].astype(_BF16)

    _stage_open(stage_ref, halo_ref, i == 0)
    row0 = 0
    for n in FFN_UP_ROW_PARTS:
        hb = _unpack_rows(hb_ref[row0 // 2:(row0 + n) // 2, :])
        gate = jnp.dot(hb, wgb_ref[...], preferred_element_type=_F32)
        up = jnp.dot(hb, wub_ref[...], preferred_element_type=_F32)
        a = _conv3(stage_ref, gate, row0, cw_ref) + cb_ref[...]
        act_ref[row0 // 2:(row0 + n) // 2, :] = _pack_rows((jax.nn.silu(a) * up).astype(_BF16))
        row0 += n
    _stage_close(stage_ref, halo_ref)


def _ffn_up_call(hb, w_gate, w_up, conv_w, conv_b, w_down):
    tm, tf = TM_FFN_UP, TF_FFN_UP
    nk, ni = D_FF // tf, SEQ // tm
    wd_rows = D_FF // (nk * ni)
    return pl.pallas_call(
        _ffn_up_kernel,
        grid=(nk, ni),
        in_specs=[
            pl.BlockSpec((tm // 2, D_MODEL), lambda k, i: (i, 0)),
            pl.BlockSpec((D_MODEL, tf), lambda k, i: (0, k)),
            pl.BlockSpec((D_MODEL, tf), lambda k, i: (0, k)),
            pl.BlockSpec((3, tf), lambda k, i: (0, k)),
            pl.BlockSpec((1, tf), lambda k, i: (0, k)),
            pl.BlockSpec((wd_rows, D_MODEL), lambda k, i: (k * ni + i, 0)),
        ],
        out_specs=[
            pl.BlockSpec((tm // 2, tf), lambda k, i: (i, k)),
            pl.BlockSpec((wd_rows // 2, D_MODEL), lambda k, i: (k * ni + i, 0)),
        ],
        out_shape=[
            jax.ShapeDtypeStruct((SEQ // 2, D_FF), jnp.int32),
            jax.ShapeDtypeStruct((D_FF // 2, D_MODEL), jnp.int32),
        ],
        scratch_shapes=[
            pltpu.VMEM((D_MODEL, tf), _BF16),
            pltpu.VMEM((D_MODEL, tf), _BF16),
            pltpu.VMEM((SUBLANES, tf), _F32),
            pltpu.VMEM((SUBLANES + tm, tf), _F32),
        ],
        compiler_params=pltpu.CompilerParams(dimension_semantics=("arbitrary", "arbitrary"),
                                             vmem_limit_bytes=VMEM_LIMIT),
        name="convffn_up",
    )(hb, w_gate, w_up, conv_w, conv_b, w_down)


def _ffn_down_kernel(act_ref, wd_ref, h_ref, g_ref, b_ref, o_ref):
    f = jnp.dot(_unpack_rows(act_ref[...]), _unpack_rows(wd_ref[...]), preferred_element_type=_F32)
    o_ref[...] = _layernorm(ALPHA * h_ref[...] + f, g_ref[...], b_ref[...])


def _ffn_down_call(act, wd_bf, h, ln_g, ln_b):
    tm = TM_FFN_DOWN
    return pl.pallas_call(
        _ffn_down_kernel,
        grid=(SEQ // tm,),
        in_specs=[
            pl.BlockSpec((tm // 2, D_FF), lambda i: (i, 0)),
            _resident((D_FF // 2, D_MODEL)),
            pl.BlockSpec((tm, D_MODEL), lambda i: (i, 0)),
            _resident((1, D_MODEL)),
            _resident((1, D_MODEL)),
        ],
        out_specs=pl.BlockSpec((tm, D_MODEL), lambda i: (i, 0)),
        out_shape=jax.ShapeDtypeStruct((SEQ, D_MODEL), _F32),
        compiler_params=pltpu.CompilerParams(dimension_semantics=("arbitrary",),
                                             vmem_limit_bytes=VMEM_LIMIT_DOWN),
        name="convffn_down_ln2",
    )(act, wd_bf, h, ln_g, ln_b)


def kernel(x, w_in, ssm_lambda_re, ssm_lambda_im, ssm_log_dt, ssm_b_re, ssm_b_im, ssm_c_re, ssm_c_im,
           ssm_d, ssm_glu_w, ssm_glu_b, sconv_w, norm_ssm_g, norm_conv_g, w_out, ln1_g, ln1_b,
           ffn_w_gate, ffn_w_up, ffn_conv_w, ffn_conv_b, ffn_w_down, ln2_g, ln2_b):
    assert x.shape == (1, SEQ, D_MODEL) and w_in.shape[0] == DEPTH
    h = x[0]
    for l in range(DEPTH):
        row = lambda p: p[l].reshape(1, -1)
        u, yc, w_out_bf, glu_w_bf = _proj_call(h, w_in[l].astype(_BF16), sconv_w[l], row(norm_conv_g),
                                               w_out[l], ssm_glu_w[l])
        y = _ssm_call(u, ssm_lambda_re[l], ssm_lambda_im[l], ssm_log_dt[l].reshape(GROUPS, 1),
                      ssm_b_re[l], ssm_b_im[l], ssm_c_re[l], ssm_c_im[l], ssm_d[l].reshape(N_SLABS, 1, SLAB_CH))
        h, hb = _outproj_call(y, yc, glu_w_bf, row(ssm_glu_b), row(norm_ssm_g),
                              w_out_bf, h, row(ln1_g), row(ln1_b))
        act, w_down_bf = _ffn_up_call(hb, ffn_w_gate[l], ffn_w_up[l], ffn_conv_w[l], row(ffn_conv_b),
                                      ffn_w_down[l])
        h = _ffn_down_call(act, w_down_bf, h, row(ln2_g), row(ln2_b))
    return h[None]
```

```python
import jax
import jax.numpy as jnp
from jax import lax
from jax.experimental import pallas as pl
from jax.experimental.pallas import tpu as pltpu

SEQ = 8192
D_MODEL = 2048
SSM_WIDTH = 1024
CONV_WIDTH = 1024
GROUP_CH = 16
GROUPS = SSM_WIDTH // GROUP_CH
STATE = 64
D_FF = 5632
DEPTH = 1
LN_EPS = 1e-5
RMS_EPS = 1e-6
ALPHA = (2.0 * DEPTH) ** 0.25

SUBLANES = 8
SLAB_GROUPS = 8
N_SLABS = GROUPS // SLAB_GROUPS
SLAB_CH = SLAB_GROUPS * GROUP_CH
SLAB_STATE = SLAB_GROUPS * STATE
SLAB_LANES = 2 * SLAB_STATE

TM_MIX = 512
SSM_CHUNKS = SUBLANES
SSM_STEPS = TM_MIX // SSM_CHUNKS
MIX_ROW_SPLIT = 2
TM_FFN_UP = 2048
TF_FFN_UP = 512
TM_FFN_DOWN = 512
FFN_UP_ROW_PARTS = (1024, 1024)
DOWN_ROW_SPLIT = 2
VMEM_LIMIT = 56 * 1024 * 1024
VMEM_LIMIT_DOWN = 60 * 1024 * 1024

_F32 = jnp.float32
_BF16 = jnp.bfloat16


def _resident(shape):
    return pl.BlockSpec(shape, lambda *_: (0,) * len(shape), pipeline_mode=pl.Buffered(1))


def _layernorm(r, g, b):
    mu = jnp.mean(r, axis=-1, keepdims=True)
    rc = r - mu
    var = jnp.mean(rc * rc, axis=-1, keepdims=True)
    return rc * lax.rsqrt(var + LN_EPS) * g + b


def _rmsnorm(y, g):
    return y * lax.rsqrt(jnp.mean(y * y, axis=-1, keepdims=True) + RMS_EPS) * g


def _pack_rows(v):
    return pltpu.bitcast(v, jnp.int32)


def _unpack_rows(v):
    return pltpu.bitcast(v, _BF16)


def _conv3(stage_ref, cur, row0, w_ref):
    n = cur.shape[0]
    stage_ref[SUBLANES + row0:SUBLANES + row0 + n, :] = cur
    x1 = stage_ref[SUBLANES - 1 + row0:SUBLANES - 1 + row0 + n, :]
    x2 = stage_ref[SUBLANES - 2 + row0:SUBLANES - 2 + row0 + n, :]
    return w_ref[0:1, :] * x2 + w_ref[1:2, :] * x1 + w_ref[2:3, :] * cur


def _stage_open(stage_ref, halo_ref, first_tile):
    @pl.when(first_tile)
    def _():
        stage_ref[0:SUBLANES, :] = jnp.zeros((SUBLANES, stage_ref.shape[1]), _F32)

    @pl.when(jnp.logical_not(first_tile))
    def _():
        stage_ref[0:SUBLANES, :] = halo_ref[...]


def _stage_close(stage_ref, halo_ref):
    tm = stage_ref.shape[0] - SUBLANES
    halo_ref[...] = stage_ref[tm:tm + SUBLANES, :]


def _chunk_rows(c):
    return pl.ds(c, SSM_STEPS, stride=SSM_CHUNKS)


def _proj_kernel(x_ref, wu_ref, wb_ref, wc_ref, wv_ref, cw_ref, g_ref, wo_ref, gw_ref,
                 u_ref, yc_ref, wob_ref, gwb_ref, halo_ref, stage_ref):
    wob_ref[...] = _pack_rows(wo_ref[...].astype(_BF16))
    gwb_ref[...] = _pack_rows(gw_ref[...].astype(_BF16))

    _stage_open(stage_ref, halo_ref, pl.program_id(0) == 0)
    xb = x_ref[...].astype(_BF16)
    gate_c = jnp.dot(xb, wc_ref[...], preferred_element_type=_F32)
    v = jnp.dot(xb, wv_ref[...], preferred_element_type=_F32)
    conv = _conv3(stage_ref, gate_c * v, 0, cw_ref)
    _stage_close(stage_ref, halo_ref)
    gate_b = jnp.dot(xb, wb_ref[...], preferred_element_type=_F32)
    yc_ref[...] = _pack_rows(_rmsnorm(gate_b * conv, g_ref[...]).astype(_BF16))
    u = jnp.dot(xb, wu_ref[...], preferred_element_type=_F32)
    for k in range(N_SLABS):
        for c in range(SSM_CHUNKS):
            u_ref[k, _chunk_rows(c), :] = u[c * SSM_STEPS:(c + 1) * SSM_STEPS, k * SLAB_CH:(k + 1) * SLAB_CH]


def _proj_call(x, w_in_bf, sconv_w, norm_conv_g, w_out, glu_w):
    tm = TM_MIX
    nt = SEQ // tm
    wspec = lambda j: pl.BlockSpec((D_MODEL, 1024), lambda i, j=j: (0, j), pipeline_mode=pl.Buffered(1))
    wo_rows, gw_rows = w_out.shape[0] // nt, glu_w.shape[0] // nt
    return pl.pallas_call(
        _proj_kernel,
        grid=(nt,),
        in_specs=[
            pl.BlockSpec((tm, D_MODEL), lambda i: (i, 0)),
            wspec(0), wspec(1), wspec(2), wspec(3),
            _resident((3, CONV_WIDTH)),
            _resident((1, CONV_WIDTH)),
            pl.BlockSpec((wo_rows, D_MODEL), lambda i: (i, 0)),
            pl.BlockSpec((gw_rows, SSM_WIDTH), lambda i: (i, 0)),
        ],
        out_specs=[
            pl.BlockSpec((N_SLABS, tm, SLAB_CH), lambda i: (0, i, 0)),
            pl.BlockSpec((tm // 2, CONV_WIDTH), lambda i: (i, 0)),
            pl.BlockSpec((wo_rows // 2, D_MODEL), lambda i: (i, 0)),
            pl.BlockSpec((gw_rows // 2, SSM_WIDTH), lambda i: (i, 0)),
        ],
        out_shape=[
            jax.ShapeDtypeStruct((N_SLABS, SEQ, SLAB_CH), _F32),
            jax.ShapeDtypeStruct((SEQ // 2, CONV_WIDTH), jnp.int32),
            jax.ShapeDtypeStruct((w_out.shape[0] // 2, D_MODEL), jnp.int32),
            jax.ShapeDtypeStruct((glu_w.shape[0] // 2, SSM_WIDTH), jnp.int32),
        ],
        scratch_shapes=[
            pltpu.VMEM((SUBLANES, CONV_WIDTH), _F32),
            pltpu.VMEM((SUBLANES + tm, CONV_WIDTH), _F32),
        ],
        compiler_params=pltpu.CompilerParams(dimension_semantics=("arbitrary",), vmem_limit_bytes=VMEM_LIMIT),
        name="proj_shortconv",
    )(x, w_in_bf, w_in_bf, w_in_bf, w_in_bf, sconv_w, norm_conv_g, w_out, glu_w)


def _cmul_add(ar, ai, br, bi, cr, ci):
    return ar * br - ai * bi + cr, ar * bi + ai * br + ci


_NT = (((1,), (1,)), ((), ()))
_TN = (((0,), (0,)), ((), ()))


def _s5_prepare(lre_ref, lim_ref, ldt_ref, bre_ref, bim_ref, cre_ref, cim_ref,
                bmat_ref, cmat_ref, lam_ref, bstage_ref, cstage_ref):
    lam_re, lam_im = lre_ref[...], lim_ref[...]
    dt = jnp.exp(ldt_ref[...])
    ang = lam_im * dt
    mag = jnp.exp(lam_re * dt)
    bar_re, bar_im = mag * jnp.cos(ang), mag * jnp.sin(ang)
    inv = 1.0 / (lam_re * lam_re + lam_im * lam_im)
    coef_re = ((bar_re - 1.0) * lam_re + bar_im * lam_im) * inv
    coef_im = (bar_im * lam_re - (bar_re - 1.0) * lam_im) * inv
    nmag = jnp.exp(SSM_STEPS * (lam_re * dt))
    pow_re, pow_im = nmag * jnp.cos(SSM_STEPS * ang), nmag * jnp.sin(SSM_STEPS * ang)
    eye = (lax.broadcasted_iota(jnp.int32, (SLAB_CH, SLAB_CH), 0)
           == lax.broadcasted_iota(jnp.int32, (SLAB_CH, SLAB_CH), 1)).astype(_BF16)
    for k in range(N_SLABS):
        bstage_ref[...] = jnp.zeros(bstage_ref.shape, _F32)
        cstage_ref[...] = jnp.zeros(cstage_ref.shape, _F32)
        for g in range(SLAB_GROUPS):
            grp = k * SLAB_GROUPS + g
            row = slice(grp, grp + 1)
            st = slice(g * STATE, (g + 1) * STATE)
            st_im = slice(SLAB_STATE + g * STATE, SLAB_STATE + (g + 1) * STATE)
            ch = slice(g * GROUP_CH, (g + 1) * GROUP_CH)
            for r, src in enumerate((bar_re, bar_im, pow_re, pow_im)):
                lam_ref[k, r:r + 1, st] = src[row, :]
            bstage_ref[st, ch] = bre_ref[grp]
            bstage_ref[st_im, ch] = bim_ref[grp]
            c_re, c_im = cre_ref[grp], cim_ref[grp]
            cstage_ref[ch, st] = c_re * coef_re[row, :] - c_im * coef_im[row, :]
            cstage_ref[ch, st_im] = -(c_re * coef_im[row, :] + c_im * coef_re[row, :])
        bmat_ref[k] = lax.dot_general(eye, bstage_ref[...].astype(_BF16), _NT,
                                      preferred_element_type=_F32).astype(_BF16)
        cmat_ref[k] = lax.dot_general(cstage_ref[...].astype(_BF16), eye, _TN,
                                      preferred_element_type=_F32).astype(_BF16)


def _ssm_kernel(u_ref, lre_ref, lim_ref, ldt_ref, bre_ref, bim_ref, cre_ref, cim_ref, d_ref, y_ref,
                bu_ref, xb_ref, state_ref, bmat_ref, cmat_ref, lam_ref, bstage_ref, cstage_ref):
    nc, ns = SSM_CHUNKS, SSM_STEPS

    @pl.when(pl.program_id(0) == 0)
    def _():
        state_ref[...] = jnp.zeros(state_ref.shape, _F32)
        _s5_prepare(lre_ref, lim_ref, ldt_ref, bre_ref, bim_ref, cre_ref, cim_ref,
                    bmat_ref, cmat_ref, lam_ref, bstage_ref, cstage_ref)

    def expand(k):
        bu_ref[k] = jnp.dot(u_ref[k].astype(_BF16), bmat_ref[k], preferred_element_type=_F32)

    expand(0)
    expand(1)
    re = slice(0, SLAB_STATE)
    im = slice(SLAB_STATE, SLAB_LANES)
    for k in range(N_SLABS):
        lr = jnp.broadcast_to(lam_ref[k, 0:1, :], (nc, SLAB_STATE))
        li = jnp.broadcast_to(lam_ref[k, 1:2, :], (nc, SLAB_STATE))

        xr = jnp.zeros((nc, SLAB_STATE), _F32)
        xi = jnp.zeros((nc, SLAB_STATE), _F32)
        for s in range(ns):
            rows = slice(s * nc, (s + 1) * nc)
            xr, xi = _cmul_add(lr, li, xr, xi, bu_ref[k, rows, re], bu_ref[k, rows, im])

        tr = lam_ref[k, 2:3, :]
        ti = lam_ref[k, 3:4, :]
        cr = state_ref[k, 0:1, :]
        ci = state_ref[k, 1:2, :]
        starts_r, starts_i = [], []
        for c in range(nc):
            starts_r.append(cr)
            starts_i.append(ci)
            cr, ci = _cmul_add(tr, ti, cr, ci, xr[c:c + 1, :], xi[c:c + 1, :])
        state_ref[k, 0:1, :] = cr
        state_ref[k, 1:2, :] = ci
        xr = jnp.concatenate(starts_r, axis=0)
        xi = jnp.concatenate(starts_i, axis=0)

        for s in range(0, ns, 2):
            parts_r, parts_i = [], []
            for q in (s, s + 1):
                rows = slice(q * nc, (q + 1) * nc)
                xr, xi = _cmul_add(lr, li, xr, xi, bu_ref[k, rows, re], bu_ref[k, rows, im])
                parts_r.append(xr)
                parts_i.append(xi)
            rows2 = slice(s * nc, (s + 2) * nc)
            xb_ref[k, rows2, re] = jnp.concatenate(parts_r, axis=0).astype(_BF16)
            xb_ref[k, rows2, im] = jnp.concatenate(parts_i, axis=0).astype(_BF16)
        if k + 2 < N_SLABS:
            expand(k + 2)
        y_ref[k] = (jnp.dot(xb_ref[k], cmat_ref[k], preferred_element_type=_F32)
                    + d_ref[k] * u_ref[k])


def _ssm_call(u, lam_re, lam_im, log_dt, b_re, b_im, c_re, c_im, d_skip):
    tm = TM_MIX
    slab_rows = pl.BlockSpec((N_SLABS, tm, SLAB_CH), lambda i: (0, i, 0))
    return pl.pallas_call(
        _ssm_kernel,
        grid=(SEQ // tm,),
        in_specs=[
            slab_rows,
            _resident((GROUPS, STATE)),
            _resident((GROUPS, STATE)),
            _resident((GROUPS, 1)),
            _resident((GROUPS, STATE, GROUP_CH)),
            _resident((GROUPS, STATE, GROUP_CH)),
            _resident((GROUPS, GROUP_CH, STATE)),
            _resident((GROUPS, GROUP_CH, STATE)),
            _resident((N_SLABS, 1, SLAB_CH)),
        ],
        out_specs=slab_rows,
        out_shape=jax.ShapeDtypeStruct((N_SLABS, SEQ, SLAB_CH), _F32),
        scratch_shapes=[
            pltpu.VMEM((N_SLABS, tm, SLAB_LANES), _F32),
            pltpu.VMEM((N_SLABS, tm, SLAB_LANES), _BF16),
            pltpu.VMEM((N_SLABS, 2, SLAB_STATE), _F32),
            pltpu.VMEM((N_SLABS, SLAB_CH, SLAB_LANES), _BF16),
            pltpu.VMEM((N_SLABS, SLAB_LANES, SLAB_CH), _BF16),
            pltpu.VMEM((N_SLABS, 4, SLAB_STATE), _F32),
            pltpu.VMEM((SLAB_LANES, SLAB_CH), _F32),
            pltpu.VMEM((SLAB_CH, SLAB_LANES), _F32),
        ],
        compiler_params=pltpu.CompilerParams(dimension_semantics=("arbitrary",), vmem_limit_bytes=VMEM_LIMIT),
        name="s5_core",
    )(u, lam_re, lam_im, log_dt, b_re, b_im, c_re, c_im, d_skip)


def _outproj_kernel(y_ref, yc_ref, gw_ref, gb_ref, ng_ref, wt_ref, wb_ref, x_ref, g_ref, b_ref, h_ref, hb_ref):
    tm = x_ref.shape[0]
    hm = tm // MIX_ROW_SPLIT
    chunks_per_part = SSM_CHUNKS // MIX_ROW_SPLIT
    for r in range(MIX_ROW_SPLIT):
        rows = slice(r * hm, (r + 1) * hm)
        y = jnp.concatenate(
            [jnp.concatenate([y_ref[k, _chunk_rows(c), :] for k in range(N_SLABS)], axis=1)
             for c in range(r * chunks_per_part, (r + 1) * chunks_per_part)], axis=0)
        gl = jax.nn.gelu(y)
        z = jnp.dot(gl.astype(_BF16), _unpack_rows(gw_ref[...]), preferred_element_type=_F32) + gb_ref[...]
        ys = _rmsnorm(gl * jax.nn.sigmoid(z), ng_ref[...]).astype(_BF16)
        mix = jnp.dot(ys, _unpack_rows(wt_ref[...]), preferred_element_type=_F32)
        yc = _unpack_rows(yc_ref[r * hm // 2:(r + 1) * hm // 2, :])
        mix = mix + jnp.dot(yc, _unpack_rows(wb_ref[...]), preferred_element_type=_F32)
        h = _layernorm(ALPHA * x_ref[rows, :] + mix, g_ref[...], b_ref[...])
        h_ref[rows, :] = h
        hb_ref[r * hm // 2:(r + 1) * hm // 2, :] = _pack_rows(h.astype(_BF16))


def _outproj_call(y, yc, glu_w_bf, glu_b, norm_g, w_out_bf, x, ln_g, ln_b):
    tm = TM_MIX
    half = lambda j: pl.BlockSpec((SSM_WIDTH // 2, D_MODEL), lambda i, j=j: (j, 0), pipeline_mode=pl.Buffered(1))
    return pl.pallas_call(
        _outproj_kernel,
        grid=(SEQ // tm,),
        in_specs=[
            pl.BlockSpec((N_SLABS, tm, SLAB_CH), lambda i: (0, i, 0)),
            pl.BlockSpec((tm // 2, CONV_WIDTH), lambda i: (i, 0)),
            _resident((SSM_WIDTH // 2, SSM_WIDTH)),
            _resident((1, SSM_WIDTH)),
            _resident((1, SSM_WIDTH)),
            half(0), half(1),
            pl.BlockSpec((tm, D_MODEL), lambda i: (i, 0)),
            _resident((1, D_MODEL)),
            _resident((1, D_MODEL)),
        ],
        out_specs=[
            pl.BlockSpec((tm, D_MODEL), lambda i: (i, 0)),
            pl.BlockSpec((tm // 2, D_MODEL), lambda i: (i, 0)),
        ],
        out_shape=[
            jax.ShapeDtypeStruct((SEQ, D_MODEL), _F32),
            jax.ShapeDtypeStruct((SEQ // 2, D_MODEL), jnp.int32),
        ],
        compiler_params=pltpu.CompilerParams(dimension_semantics=("arbitrary",), vmem_limit_bytes=VMEM_LIMIT),
        name="glu_outproj_ln1",
    )(y, yc, glu_w_bf, glu_b, norm_g, w_out_bf, w_out_bf, x, ln_g, ln_b)


def _ffn_up_kernel(hb_ref, wg_ref, wu_ref, cw_ref, cb_ref, wd_ref, act_ref, wdb_ref,
                   wgb_ref, wub_ref, halo_ref, stage_ref):
    i = pl.program_id(1)

    wdb_ref[...] = _pack_rows(wd_ref[...].astype(_BF16))

    @pl.when(i == 0)
    def _():
        wgb_ref[...] = wg_ref[...].astype(_BF16)
        wub_ref[...] = wu_ref[...].astype(_BF16)

    _stage_open(stage_ref, halo_ref, i == 0)
    row0 = 0
    for n in FFN_UP_ROW_PARTS:
        hb = _unpack_rows(hb_ref[row0 // 2:(row0 + n) // 2, :])
        gate = jnp.dot(hb, wgb_ref[...], preferred_element_type=_F32)
        up = jnp.dot(hb, wub_ref[...], preferred_element_type=_F32)
        a = _conv3(stage_ref, gate, row0, cw_ref) + cb_ref[...]
        act_ref[row0 // 2:(row0 + n) // 2, :] = _pack_rows((jax.nn.silu(a) * up).astype(_BF16))
        row0 += n
    _stage_close(stage_ref, halo_ref)


def _ffn_up_call(hb, w_gate, w_up, conv_w, conv_b, w_down):
    tm, tf = TM_FFN_UP, TF_FFN_UP
    nk, ni = D_FF // tf, SEQ // tm
    wd_rows = D_FF // (nk * ni)
    return pl.pallas_call(
        _ffn_up_kernel,
        grid=(nk, ni),
        in_specs=[
            pl.BlockSpec((tm // 2, D_MODEL), lambda k, i: (i, 0)),
            pl.BlockSpec((D_MODEL, tf), lambda k, i: (0, k)),
            pl.BlockSpec((D_MODEL, tf), lambda k, i: (0, k)),
            pl.BlockSpec((3, tf), lambda k, i: (0, k)),
            pl.BlockSpec((1, tf), lambda k, i: (0, k)),
            pl.BlockSpec((wd_rows, D_MODEL), lambda k, i: (k * ni + i, 0)),
        ],
        out_specs=[
            pl.BlockSpec((tm // 2, tf), lambda k, i: (i, k)),
            pl.BlockSpec((wd_rows // 2, D_MODEL), lambda k, i: (k * ni + i, 0)),
        ],
        out_shape=[
            jax.ShapeDtypeStruct((SEQ // 2, D_FF), jnp.int32),
            jax.ShapeDtypeStruct((D_FF // 2, D_MODEL), jnp.int32),
        ],
        scratch_shapes=[
            pltpu.VMEM((D_MODEL, tf), _BF16),
            pltpu.VMEM((D_MODEL, tf), _BF16),
            pltpu.VMEM((SUBLANES, tf), _F32),
            pltpu.VMEM((SUBLANES + tm, tf), _F32),
        ],
        compiler_params=pltpu.CompilerParams(dimension_semantics=("arbitrary", "arbitrary"),
                                             vmem_limit_bytes=VMEM_LIMIT),
        name="convffn_up",
    )(hb, w_gate, w_up, conv_w, conv_b, w_down)


def _ffn_down_kernel(act_ref, wd_ref, h_ref, g_ref, b_ref, o_ref):
    tm = o_ref.shape[0]
    hm = tm // DOWN_ROW_SPLIT
    for r in range(DOWN_ROW_SPLIT):
        rows = slice(r * hm, (r + 1) * hm)
        act = _unpack_rows(act_ref[r * hm // 2:(r + 1) * hm // 2, :])
        f = jnp.dot(act, _unpack_rows(wd_ref[...]), preferred_element_type=_F32)
        o_ref[rows, :] = _layernorm(ALPHA * h_ref[rows, :] + f, g_ref[...], b_ref[...])


def _ffn_down_call(act, wd_bf, h, ln_g, ln_b):
    tm = TM_FFN_DOWN
    return pl.pallas_call(
        _ffn_down_kernel,
        grid=(SEQ // tm,),
        in_specs=[
            pl.BlockSpec((tm // 2, D_FF), lambda i: (i, 0)),
            _resident((D_FF // 2, D_MODEL)),
            pl.BlockSpec((tm, D_MODEL), lambda i: (i, 0)),
            _resident((1, D_MODEL)),
            _resident((1, D_MODEL)),
        ],
        out_specs=pl.BlockSpec((tm, D_MODEL), lambda i: (i, 0)),
        out_shape=jax.ShapeDtypeStruct((SEQ, D_MODEL), _F32),
        compiler_params=pltpu.CompilerParams(dimension_semantics=("arbitrary",),
                                             vmem_limit_bytes=VMEM_LIMIT_DOWN),
        name="convffn_down_ln2",
    )(act, wd_bf, h, ln_g, ln_b)


def kernel(x, w_in, ssm_lambda_re, ssm_lambda_im, ssm_log_dt, ssm_b_re, ssm_b_im, ssm_c_re, ssm_c_im,
           ssm_d, ssm_glu_w, ssm_glu_b, sconv_w, norm_ssm_g, norm_conv_g, w_out, ln1_g, ln1_b,
           ffn_w_gate, ffn_w_up, ffn_conv_w, ffn_conv_b, ffn_w_down, ln2_g, ln2_b):
    assert x.shape == (1, SEQ, D_MODEL) and w_in.shape[0] == DEPTH
    h = x[0]
    for l in range(DEPTH):
        row = lambda p: p[l].reshape(1, -1)
        u, yc, w_out_bf, glu_w_bf = _proj_call(h, w_in[l].astype(_BF16), sconv_w[l], row(norm_conv_g),
                                               w_out[l], ssm_glu_w[l])
        y = _ssm_call(u, ssm_lambda_re[l], ssm_lambda_im[l], ssm_log_dt[l].reshape(GROUPS, 1),
                      ssm_b_re[l], ssm_b_im[l], ssm_c_re[l], ssm_c_im[l], ssm_d[l].reshape(N_SLABS, 1, SLAB_CH))
        h, hb = _outproj_call(y, yc, glu_w_bf, row(ssm_glu_b), row(norm_ssm_g),
                              w_out_bf, h, row(ln1_g), row(ln1_b))
        act, w_down_bf = _ffn_up_call(hb, ffn_w_gate[l], ffn_w_up[l], ffn_conv_w[l], row(ffn_conv_b),
                                      ffn_w_down[l])
        h = _ffn_down_call(act, w_down_bf, h, row(ln2_g), row(ln2_b))
    return h[None]
```

```python
import jax
import jax.numpy as jnp
from jax import lax
from jax.experimental import pallas as pl
from jax.experimental.pallas import tpu as pltpu

SEQ = 8192
D_MODEL = 2048
SSM_WIDTH = 1024
CONV_WIDTH = 1024
GROUP_CH = 16
GROUPS = SSM_WIDTH // GROUP_CH
STATE = 64
D_FF = 5632
DEPTH = 1
LN_EPS = 1e-5
RMS_EPS = 1e-6
ALPHA = (2.0 * DEPTH) ** 0.25

SUBLANES = 8
SLAB_GROUPS = 8
N_SLABS = GROUPS // SLAB_GROUPS
SLAB_CH = SLAB_GROUPS * GROUP_CH
SLAB_STATE = SLAB_GROUPS * STATE
SLAB_LANES = 2 * SLAB_STATE

TM_MIX = 512
SSM_CHUNKS = SUBLANES
SSM_STEPS = TM_MIX // SSM_CHUNKS
MIX_ROW_SPLIT = 2
TM_FFN_UP = 2048
TF_FFN_UP = 512
TM_FFN_DOWN = 512
FFN_UP_ROW_PARTS = (1024, 1024)
DOWN_ROW_SPLIT = 2
VMEM_LIMIT = 56 * 1024 * 1024
VMEM_LIMIT_DOWN = 60 * 1024 * 1024

_F32 = jnp.float32
_BF16 = jnp.bfloat16


def _resident(shape):
    return pl.BlockSpec(shape, lambda *_: (0,) * len(shape), pipeline_mode=pl.Buffered(1))


def _layernorm(r, g, b):
    mu = jnp.mean(r, axis=-1, keepdims=True)
    rc = r - mu
    var = jnp.mean(rc * rc, axis=-1, keepdims=True)
    return rc * lax.rsqrt(var + LN_EPS) * g + b


def _rmsnorm(y, g):
    return y * lax.rsqrt(jnp.mean(y * y, axis=-1, keepdims=True) + RMS_EPS) * g


def _pack_rows(v):
    return pltpu.bitcast(v, jnp.int32)


def _unpack_rows(v):
    return pltpu.bitcast(v, _BF16)


def _conv3(stage_ref, cur, row0, w_ref):
    n = cur.shape[0]
    stage_ref[SUBLANES + row0:SUBLANES + row0 + n, :] = cur
    x1 = stage_ref[SUBLANES - 1 + row0:SUBLANES - 1 + row0 + n, :]
    x2 = stage_ref[SUBLANES - 2 + row0:SUBLANES - 2 + row0 + n, :]
    return w_ref[0:1, :] * x2 + w_ref[1:2, :] * x1 + w_ref[2:3, :] * cur


def _stage_open(stage_ref, halo_ref, first_tile):
    @pl.when(first_tile)
    def _():
        stage_ref[0:SUBLANES, :] = jnp.zeros((SUBLANES, stage_ref.shape[1]), _F32)

    @pl.when(jnp.logical_not(first_tile))
    def _():
        stage_ref[0:SUBLANES, :] = halo_ref[...]


def _stage_close(stage_ref, halo_ref):
    tm = stage_ref.shape[0] - SUBLANES
    halo_ref[...] = stage_ref[tm:tm + SUBLANES, :]


def _chunk_rows(c):
    return pl.ds(c, SSM_STEPS, stride=SSM_CHUNKS)


def _proj_kernel(x_ref, wu_ref, wb_ref, wc_ref, wv_ref, cw_ref, g_ref, wo_ref, gw_ref,
                 u_ref, yc_ref, wob_ref, gwb_ref, halo_ref, stage_ref):
    wob_ref[...] = _pack_rows(wo_ref[...].astype(_BF16))
    gwb_ref[...] = _pack_rows(gw_ref[...].astype(_BF16))

    _stage_open(stage_ref, halo_ref, pl.program_id(0) == 0)
    xb = x_ref[...].astype(_BF16)
    gate_c = jnp.dot(xb, wc_ref[...], preferred_element_type=_F32)
    v = jnp.dot(xb, wv_ref[...], preferred_element_type=_F32)
    conv = _conv3(stage_ref, gate_c * v, 0, cw_ref)
    _stage_close(stage_ref, halo_ref)
    gate_b = jnp.dot(xb, wb_ref[...], preferred_element_type=_F32)
    yc_ref[...] = _pack_rows(_rmsnorm(gate_b * conv, g_ref[...]).astype(_BF16))
    u = jnp.dot(xb, wu_ref[...], preferred_element_type=_F32)
    for k in range(N_SLABS):
        for c in range(SSM_CHUNKS):
            u_ref[k, _chunk_rows(c), :] = u[c * SSM_STEPS:(c + 1) * SSM_STEPS, k * SLAB_CH:(k + 1) * SLAB_CH]


def _proj_call(x, w_in_bf, sconv_w, norm_conv_g, w_out, glu_w):
    tm = TM_MIX
    nt = SEQ // tm
    wspec = lambda j: pl.BlockSpec((D_MODEL, 1024), lambda i, j=j: (0, j), pipeline_mode=pl.Buffered(1))
    wo_rows, gw_rows = w_out.shape[0] // nt, glu_w.shape[0] // nt
    return pl.pallas_call(
        _proj_kernel,
        grid=(nt,),
        in_specs=[
            pl.BlockSpec((tm, D_MODEL), lambda i: (i, 0)),
            wspec(0), wspec(1), wspec(2), wspec(3),
            _resident((3, CONV_WIDTH)),
            _resident((1, CONV_WIDTH)),
            pl.BlockSpec((wo_rows, D_MODEL), lambda i: (i, 0)),
            pl.BlockSpec((gw_rows, SSM_WIDTH), lambda i: (i, 0)),
        ],
        out_specs=[
            pl.BlockSpec((N_SLABS, tm, SLAB_CH), lambda i: (0, i, 0)),
            pl.BlockSpec((tm // 2, CONV_WIDTH), lambda i: (i, 0)),
            pl.BlockSpec((wo_rows // 2, D_MODEL), lambda i: (i, 0)),
            pl.BlockSpec((gw_rows // 2, SSM_WIDTH), lambda i: (i, 0)),
        ],
        out_shape=[
            jax.ShapeDtypeStruct((N_SLABS, SEQ, SLAB_CH), _F32),
            jax.ShapeDtypeStruct((SEQ // 2, CONV_WIDTH), jnp.int32),
            jax.ShapeDtypeStruct((w_out.shape[0] // 2, D_MODEL), jnp.int32),
            jax.ShapeDtypeStruct((glu_w.shape[0] // 2, SSM_WIDTH), jnp.int32),
        ],
        scratch_shapes=[
            pltpu.VMEM((SUBLANES, CONV_WIDTH), _F32),
            pltpu.VMEM((SUBLANES + tm, CONV_WIDTH), _F32),
        ],
        compiler_params=pltpu.CompilerParams(dimension_semantics=("arbitrary",), vmem_limit_bytes=VMEM_LIMIT),
        name="proj_shortconv",
    )(x, w_in_bf, w_in_bf, w_in_bf, w_in_bf, sconv_w, norm_conv_g, w_out, glu_w)


def _cmul_add(ar, ai, br, bi, cr, ci):
    return ar * br - ai * bi + cr, ar * bi + ai * br + ci


_TN = (((0,), (0,)), ((), ()))


def _s5_prepare(lre_ref, lim_ref, ldt_ref, bre_ref, bim_ref, cre_ref, cim_ref,
                bmat_ref, cmat_ref, lam_ref, bstage_ref, cstage_ref):
    lam_re, lam_im = lre_ref[...], lim_ref[...]
    dt = jnp.exp(ldt_ref[...])
    ang = lam_im * dt
    mag = jnp.exp(lam_re * dt)
    bar_re, bar_im = mag * jnp.cos(ang), mag * jnp.sin(ang)
    inv = 1.0 / (lam_re * lam_re + lam_im * lam_im)
    coef_re = ((bar_re - 1.0) * lam_re + bar_im * lam_im) * inv
    coef_im = (bar_im * lam_re - (bar_re - 1.0) * lam_im) * inv
    nmag = jnp.exp(SSM_STEPS * (lam_re * dt))
    pow_re, pow_im = nmag * jnp.cos(SSM_STEPS * ang), nmag * jnp.sin(SSM_STEPS * ang)
    eye = (lax.broadcasted_iota(jnp.int32, (SLAB_CH, SLAB_CH), 0)
           == lax.broadcasted_iota(jnp.int32, (SLAB_CH, SLAB_CH), 1)).astype(_BF16)
    for k in range(N_SLABS):
        bstage_ref[...] = jnp.zeros(bstage_ref.shape, _F32)
        cstage_ref[...] = jnp.zeros(cstage_ref.shape, _F32)
        for g in range(SLAB_GROUPS):
            grp = k * SLAB_GROUPS + g
            row = slice(grp, grp + 1)
            st = slice(g * STATE, (g + 1) * STATE)
            st_im = slice(SLAB_STATE + g * STATE, SLAB_STATE + (g + 1) * STATE)
            ch = slice(g * GROUP_CH, (g + 1) * GROUP_CH)
            for r, src in enumerate((bar_re, bar_im, pow_re, pow_im)):
                lam_ref[k, r:r + 1, st] = src[row, :]
            bstage_ref[ch, st] = bre_ref[grp]
            bstage_ref[ch, st_im] = bim_ref[grp]
            c_re, c_im = cre_ref[grp], cim_ref[grp]
            cstage_ref[ch, st] = c_re * coef_re[row, :] - c_im * coef_im[row, :]
            cstage_ref[ch, st_im] = -(c_re * coef_im[row, :] + c_im * coef_re[row, :])
        bmat_ref[k] = bstage_ref[...].astype(_BF16)
        cmat_ref[k] = lax.dot_general(cstage_ref[...].astype(_BF16), eye, _TN,
                                      preferred_element_type=_F32).astype(_BF16)


def _ssm_kernel(u_ref, lre_ref, lim_ref, ldt_ref, bre_ref, bim_ref, cre_ref, cim_ref, d_ref, y_ref,
                bu_ref, xb_ref, state_ref, bmat_ref, cmat_ref, lam_ref, bstage_ref, cstage_ref):
    nc, ns = SSM_CHUNKS, SSM_STEPS

    @pl.when(pl.program_id(0) == 0)
    def _():
        state_ref[...] = jnp.zeros(state_ref.shape, _F32)
        _s5_prepare(lre_ref, lim_ref, ldt_ref, bre_ref, bim_ref, cre_ref, cim_ref,
                    bmat_ref, cmat_ref, lam_ref, bstage_ref, cstage_ref)

    def expand(k):
        bu_ref[k] = jnp.dot(u_ref[k].astype(_BF16), bmat_ref[k], preferred_element_type=_F32)

    expand(0)
    expand(1)
    re = slice(0, SLAB_STATE)
    im = slice(SLAB_STATE, SLAB_LANES)
    for k in range(N_SLABS):
        lr = jnp.broadcast_to(lam_ref[k, 0:1, :], (nc, SLAB_STATE))
        li = jnp.broadcast_to(lam_ref[k, 1:2, :], (nc, SLAB_STATE))

        xr = jnp.zeros((nc, SLAB_STATE), _F32)
        xi = jnp.zeros((nc, SLAB_STATE), _F32)
        for s in range(ns):
            rows = slice(s * nc, (s + 1) * nc)
            xr, xi = _cmul_add(lr, li, xr, xi, bu_ref[k, rows, re], bu_ref[k, rows, im])

        tr = lam_ref[k, 2:3, :]
        ti = lam_ref[k, 3:4, :]
        cr = state_ref[k, 0:1, :]
        ci = state_ref[k, 1:2, :]
        starts_r, starts_i = [], []
        for c in range(nc):
            starts_r.append(cr)
            starts_i.append(ci)
            cr, ci = _cmul_add(tr, ti, cr, ci, xr[c:c + 1, :], xi[c:c + 1, :])
        state_ref[k, 0:1, :] = cr
        state_ref[k, 1:2, :] = ci
        xr = jnp.concatenate(starts_r, axis=0)
        xi = jnp.concatenate(starts_i, axis=0)

        for s in range(0, ns, 2):
            parts_r, parts_i = [], []
            for q in (s, s + 1):
                rows = slice(q * nc, (q + 1) * nc)
                xr, xi = _cmul_add(lr, li, xr, xi, bu_ref[k, rows, re], bu_ref[k, rows, im])
                parts_r.append(xr)
                parts_i.append(xi)
            rows2 = slice(s * nc, (s + 2) * nc)
            xb_ref[k, rows2, re] = jnp.concatenate(parts_r, axis=0).astype(_BF16)
            xb_ref[k, rows2, im] = jnp.concatenate(parts_i, axis=0).astype(_BF16)
        if k + 2 < N_SLABS:
            expand(k + 2)
        y_ref[k] = (jnp.dot(xb_ref[k], cmat_ref[k], preferred_element_type=_F32)
                    + d_ref[k] * u_ref[k])


def _ssm_call(u, lam_re, lam_im, log_dt, b_re, b_im, c_re, c_im, d_skip):
    tm = TM_MIX
    slab_rows = pl.BlockSpec((N_SLABS, tm, SLAB_CH), lambda i: (0, i, 0))
    return pl.pallas_call(
        _ssm_kernel,
        grid=(SEQ // tm,),
        in_specs=[
            slab_rows,
            _resident((GROUPS, STATE)),
            _resident((GROUPS, STATE)),
            _resident((GROUPS, 1)),
            _resident((GROUPS, GROUP_CH, STATE)),
            _resident((GROUPS, GROUP_CH, STATE)),
            _resident((GROUPS, GROUP_CH, STATE)),
            _resident((GROUPS, GROUP_CH, STATE)),
            _resident((N_SLABS, 1, SLAB_CH)),
        ],
        out_specs=slab_rows,
        out_shape=jax.ShapeDtypeStruct((N_SLABS, SEQ, SLAB_CH), _F32),
        scratch_shapes=[
            pltpu.VMEM((N_SLABS, tm, SLAB_LANES), _F32),
            pltpu.VMEM((N_SLABS, tm, SLAB_LANES), _BF16),
            pltpu.VMEM((N_SLABS, 2, SLAB_STATE), _F32),
            pltpu.VMEM((N_SLABS, SLAB_CH, SLAB_LANES), _BF16),
            pltpu.VMEM((N_SLABS, SLAB_LANES, SLAB_CH), _BF16),
            pltpu.VMEM((N_SLABS, 4, SLAB_STATE), _F32),
            pltpu.VMEM((SLAB_CH, SLAB_LANES), _F32),
            pltpu.VMEM((SLAB_CH, SLAB_LANES), _F32),
        ],
        compiler_params=pltpu.CompilerParams(dimension_semantics=("arbitrary",), vmem_limit_bytes=VMEM_LIMIT),
        name="s5_core",
    )(u, lam_re, lam_im, log_dt, b_re, b_im, c_re, c_im, d_skip)


def _outproj_kernel(y_ref, yc_ref, gw_ref, gb_ref, ng_ref, wt_ref, wb_ref, x_ref, g_ref, b_ref, h_ref, hb_ref):
    tm = x_ref.shape[0]
    hm = tm // MIX_ROW_SPLIT
    chunks_per_part = SSM_CHUNKS // MIX_ROW_SPLIT
    for r in range(MIX_ROW_SPLIT):
        rows = slice(r * hm, (r + 1) * hm)
        y = jnp.concatenate(
            [jnp.concatenate([y_ref[k, _chunk_rows(c), :] for k in range(N_SLABS)], axis=1)
             for c in range(r * chunks_per_part, (r + 1) * chunks_per_part)], axis=0)
        gl = jax.nn.gelu(y)
        z = jnp.dot(gl.astype(_BF16), _unpack_rows(gw_ref[...]), preferred_element_type=_F32) + gb_ref[...]
        ys = _rmsnorm(gl * jax.nn.sigmoid(z), ng_ref[...]).astype(_BF16)
        mix = jnp.dot(ys, _unpack_rows(wt_ref[...]), preferred_element_type=_F32)
        yc = _unpack_rows(yc_ref[r * hm // 2:(r + 1) * hm // 2, :])
        mix = mix + jnp.dot(yc, _unpack_rows(wb_ref[...]), preferred_element_type=_F32)
        h = _layernorm(ALPHA * x_ref[rows, :] + mix, g_ref[...], b_ref[...])
        h_ref[rows, :] = h
        hb_ref[r * hm // 2:(r + 1) * hm // 2, :] = _pack_rows(h.astype(_BF16))


def _outproj_call(y, yc, glu_w_bf, glu_b, norm_g, w_out_bf, x, ln_g, ln_b):
    tm = TM_MIX
    half = lambda j: pl.BlockSpec((SSM_WIDTH // 2, D_MODEL), lambda i, j=j: (j, 0), pipeline_mode=pl.Buffered(1))
    return pl.pallas_call(
        _outproj_kernel,
        grid=(SEQ // tm,),
        in_specs=[
            pl.BlockSpec((N_SLABS, tm, SLAB_CH), lambda i: (0, i, 0)),
            pl.BlockSpec((tm // 2, CONV_WIDTH), lambda i: (i, 0)),
            _resident((SSM_WIDTH // 2, SSM_WIDTH)),
            _resident((1, SSM_WIDTH)),
            _resident((1, SSM_WIDTH)),
            half(0), half(1),
            pl.BlockSpec((tm, D_MODEL), lambda i: (i, 0)),
            _resident((1, D_MODEL)),
            _resident((1, D_MODEL)),
        ],
        out_specs=[
            pl.BlockSpec((tm, D_MODEL), lambda i: (i, 0)),
            pl.BlockSpec((tm // 2, D_MODEL), lambda i: (i, 0)),
        ],
        out_shape=[
            jax.ShapeDtypeStruct((SEQ, D_MODEL), _F32),
            jax.ShapeDtypeStruct((SEQ // 2, D_MODEL), jnp.int32),
        ],
        compiler_params=pltpu.CompilerParams(dimension_semantics=("arbitrary",), vmem_limit_bytes=VMEM_LIMIT),
        name="glu_outproj_ln1",
    )(y, yc, glu_w_bf, glu_b, norm_g, w_out_bf, w_out_bf, x, ln_g, ln_b)


def _ffn_up_kernel(hb_ref, wg_ref, wu_ref, cw_ref, cb_ref, wd_ref, act_ref, wdb_ref,
                   wgb_ref, wub_ref, halo_ref, stage_ref):
    i = pl.program_id(1)

    wdb_ref[...] = _pack_rows(wd_ref[...].astype(_BF16))

    @pl.when(i == 0)
    def _():
        wgb_ref[...] = wg_ref[...].astype(_BF16)
        wub_ref[...] = wu_ref[...].astype(_BF16)

    _stage_open(stage_ref, halo_ref, i == 0)
    row0 = 0
    for n in FFN_UP_ROW_PARTS:
        hb = _unpack_rows(hb_ref[row0 // 2:(row0 + n) // 2, :])
        gate = jnp.dot(hb, wgb_ref[...], preferred_element_type=_F32)
        up = jnp.dot(hb, wub_ref[...], preferred_element_type=_F32)
        a = _conv3(stage_ref, gate, row0, cw_ref) + cb_ref[...]
        act_ref[row0 // 2:(row0 + n) // 2, :] = _pack_rows((jax.nn.silu(a) * up).astype(_BF16))
        row0 += n
    _stage_close(stage_ref, halo_ref)


def _ffn_up_call(hb, w_gate, w_up, conv_w, conv_b, w_down):
    tm, tf = TM_FFN_UP, TF_FFN_UP
    nk, ni = D_FF // tf, SEQ // tm
    wd_rows = D_FF // (nk * ni)
    return pl.pallas_call(
        _ffn_up_kernel,
        grid=(nk, ni),
        in_specs=[
            pl.BlockSpec((tm // 2, D_MODEL), lambda k, i: (i, 0)),
            pl.BlockSpec((D_MODEL, tf), lambda k, i: (0, k)),
            pl.BlockSpec((D_MODEL, tf), lambda k, i: (0, k)),
            pl.BlockSpec((3, tf), lambda k, i: (0, k)),
            pl.BlockSpec((1, tf), lambda k, i: (0, k)),
            pl.BlockSpec((wd_rows, D_MODEL), lambda k, i: (k * ni + i, 0)),
        ],
        out_specs=[
            pl.BlockSpec((tm // 2, tf), lambda k, i: (i, k)),
            pl.BlockSpec((wd_rows // 2, D_MODEL), lambda k, i: (k * ni + i, 0)),
        ],
        out_shape=[
            jax.ShapeDtypeStruct((SEQ // 2, D_FF), jnp.int32),
            jax.ShapeDtypeStruct((D_FF // 2, D_MODEL), jnp.int32),
        ],
        scratch_shapes=[
            pltpu.VMEM((D_MODEL, tf), _BF16),
            pltpu.VMEM((D_MODEL, tf), _BF16),
            pltpu.VMEM((SUBLANES, tf), _F32),
            pltpu.VMEM((SUBLANES + tm, tf), _F32),
        ],
        compiler_params=pltpu.CompilerParams(dimension_semantics=("arbitrary", "arbitrary"),
                                             vmem_limit_bytes=VMEM_LIMIT),
        name="convffn_up",
    )(hb, w_gate, w_up, conv_w, conv_b, w_down)


def _ffn_down_kernel(act_ref, wd_ref, h_ref, g_ref, b_ref, o_ref):
    tm = o_ref.shape[0]
    hm = tm // DOWN_ROW_SPLIT
    for r in range(DOWN_ROW_SPLIT):
        rows = slice(r * hm, (r + 1) * hm)
        act = _unpack_rows(act_ref[r * hm // 2:(r + 1) * hm // 2, :])
        f = jnp.dot(act, _unpack_rows(wd_ref[...]), preferred_element_type=_F32)
        o_ref[rows, :] = _layernorm(ALPHA * h_ref[rows, :] + f, g_ref[...], b_ref[...])


def _ffn_down_call(act, wd_bf, h, ln_g, ln_b):
    tm = TM_FFN_DOWN
    return pl.pallas_call(
        _ffn_down_kernel,
        grid=(SEQ // tm,),
        in_specs=[
            pl.BlockSpec((tm // 2, D_FF), lambda i: (i, 0)),
            _resident((D_FF // 2, D_MODEL)),
            pl.BlockSpec((tm, D_MODEL), lambda i: (i, 0)),
            _resident((1, D_MODEL)),
            _resident((1, D_MODEL)),
        ],
        out_specs=pl.BlockSpec((tm, D_MODEL), lambda i: (i, 0)),
        out_shape=jax.ShapeDtypeStruct((SEQ, D_MODEL), _F32),
        compiler_params=pltpu.CompilerParams(dimension_semantics=("arbitrary",),
                                             vmem_limit_bytes=VMEM_LIMIT_DOWN),
        name="convffn_down_ln2",
    )(act, wd_bf, h, ln_g, ln_b)


def kernel(x, w_in, ssm_lambda_re, ssm_lambda_im, ssm_log_dt, ssm_b_re, ssm_b_im, ssm_c_re, ssm_c_im,
           ssm_d, ssm_glu_w, ssm_glu_b, sconv_w, norm_ssm_g, norm_conv_g, w_out, ln1_g, ln1_b,
           ffn_w_gate, ffn_w_up, ffn_conv_w, ffn_conv_b, ffn_w_down, ln2_g, ln2_b):
    assert x.shape == (1, SEQ, D_MODEL) and w_in.shape[0] == DEPTH
    h = x[0]
    for l in range(DEPTH):
        row = lambda p: p[l].reshape(1, -1)
        u, yc, w_out_bf, glu_w_bf = _proj_call(h, w_in[l].astype(_BF16), sconv_w[l], row(norm_conv_g),
                                               w_out[l], ssm_glu_w[l])
        y = _ssm_call(u, ssm_lambda_re[l], ssm_lambda_im[l], ssm_log_dt[l].reshape(GROUPS, 1),
                      jnp.swapaxes(ssm_b_re[l], 1, 2), jnp.swapaxes(ssm_b_im[l], 1, 2),
                      ssm_c_re[l], ssm_c_im[l], ssm_d[l].reshape(N_SLABS, 1, SLAB_CH))
        h, hb = _outproj_call(y, yc, glu_w_bf, row(ssm_glu_b), row(norm_ssm_g),
                              w_out_bf, h, row(ln1_g), row(ln1_b))
        act, w_down_bf = _ffn_up_call(hb, ffn_w_gate[l], ffn_w_up[l], ffn_conv_w[l], row(ffn_conv_b),
                                      ffn_w_down[l])
        h = _ffn_down_call(act, w_down_bf, h, row(ln2_g), row(ln2_b))
    return h[None]
```

```python
import jax
import jax.numpy as jnp
from jax import lax
from jax.experimental import pallas as pl
from jax.experimental.pallas import tpu as pltpu

SEQ = 8192
D_MODEL = 2048
SSM_WIDTH = 1024
CONV_WIDTH = 1024
GROUP_CH = 16
GROUPS = SSM_WIDTH // GROUP_CH
STATE = 64
D_FF = 5632
DEPTH = 1
LN_EPS = 1e-5
RMS_EPS = 1e-6
ALPHA = (2.0 * DEPTH) ** 0.25

SUBLANES = 8
SLAB_GROUPS = 8
N_SLABS = GROUPS // SLAB_GROUPS
SLAB_CH = SLAB_GROUPS * GROUP_CH
SLAB_STATE = SLAB_GROUPS * STATE
SLAB_LANES = 2 * SLAB_STATE

TM_MIX = 512
SSM_CHUNKS = SUBLANES
SSM_STEPS = TM_MIX // SSM_CHUNKS
MIX_ROW_SPLIT = 2
TM_FFN_UP = 2048
TF_FFN_UP = 512
TM_FFN_DOWN = 512
FFN_UP_ROW_PARTS = (1024, 512, 512)
DOWN_ROW_SPLIT = 2
VMEM_LIMIT = 56 * 1024 * 1024
VMEM_LIMIT_DOWN = 60 * 1024 * 1024

_F32 = jnp.float32
_BF16 = jnp.bfloat16


def _resident(shape):
    return pl.BlockSpec(shape, lambda *_: (0,) * len(shape), pipeline_mode=pl.Buffered(1))


def _layernorm(r, g, b):
    mu = jnp.mean(r, axis=-1, keepdims=True)
    rc = r - mu
    var = jnp.mean(rc * rc, axis=-1, keepdims=True)
    return rc * lax.rsqrt(var + LN_EPS) * g + b


def _rmsnorm(y, g):
    return y * lax.rsqrt(jnp.mean(y * y, axis=-1, keepdims=True) + RMS_EPS) * g


def _pack_rows(v):
    return pltpu.bitcast(v, jnp.int32)


def _unpack_rows(v):
    return pltpu.bitcast(v, _BF16)


def _conv3(stage_ref, cur, row0, w_ref):
    n = cur.shape[0]
    stage_ref[SUBLANES + row0:SUBLANES + row0 + n, :] = cur
    x1 = stage_ref[SUBLANES - 1 + row0:SUBLANES - 1 + row0 + n, :]
    x2 = stage_ref[SUBLANES - 2 + row0:SUBLANES - 2 + row0 + n, :]
    return w_ref[0:1, :] * x2 + w_ref[1:2, :] * x1 + w_ref[2:3, :] * cur


def _stage_open(stage_ref, halo_ref, first_tile):
    @pl.when(first_tile)
    def _():
        stage_ref[0:SUBLANES, :] = jnp.zeros((SUBLANES, stage_ref.shape[1]), _F32)

    @pl.when(jnp.logical_not(first_tile))
    def _():
        stage_ref[0:SUBLANES, :] = halo_ref[...]


def _stage_close(stage_ref, halo_ref):
    tm = stage_ref.shape[0] - SUBLANES
    halo_ref[...] = stage_ref[tm:tm + SUBLANES, :]


def _chunk_rows(c):
    return pl.ds(c, SSM_STEPS, stride=SSM_CHUNKS)


def _proj_kernel(x_ref, wu_ref, wb_ref, wc_ref, wv_ref, cw_ref, g_ref, wo_ref, gw_ref,
                 u_ref, yc_ref, wob_ref, gwb_ref, halo_ref, stage_ref):
    wob_ref[...] = _pack_rows(wo_ref[...].astype(_BF16))
    gwb_ref[...] = _pack_rows(gw_ref[...].astype(_BF16))

    _stage_open(stage_ref, halo_ref, pl.program_id(0) == 0)
    xb = x_ref[...].astype(_BF16)
    gate_c = jnp.dot(xb, wc_ref[...], preferred_element_type=_F32)
    v = jnp.dot(xb, wv_ref[...], preferred_element_type=_F32)
    conv = _conv3(stage_ref, gate_c * v, 0, cw_ref)
    _stage_close(stage_ref, halo_ref)
    gate_b = jnp.dot(xb, wb_ref[...], preferred_element_type=_F32)
    yc_ref[...] = _pack_rows(_rmsnorm(gate_b * conv, g_ref[...]).astype(_BF16))
    u = jnp.dot(xb, wu_ref[...], preferred_element_type=_F32)
    for k in range(N_SLABS):
        for c in range(SSM_CHUNKS):
            u_ref[k, _chunk_rows(c), :] = u[c * SSM_STEPS:(c + 1) * SSM_STEPS, k * SLAB_CH:(k + 1) * SLAB_CH]


def _proj_call(x, w_in_bf, sconv_w, norm_conv_g, w_out, glu_w):
    tm = TM_MIX
    nt = SEQ // tm
    wspec = lambda j: pl.BlockSpec((D_MODEL, 1024), lambda i, j=j: (0, j), pipeline_mode=pl.Buffered(1))
    wo_rows, gw_rows = w_out.shape[0] // nt, glu_w.shape[0] // nt
    return pl.pallas_call(
        _proj_kernel,
        grid=(nt,),
        in_specs=[
            pl.BlockSpec((tm, D_MODEL), lambda i: (i, 0)),
            wspec(0), wspec(1), wspec(2), wspec(3),
            _resident((3, CONV_WIDTH)),
            _resident((1, CONV_WIDTH)),
            pl.BlockSpec((wo_rows, D_MODEL), lambda i: (i, 0)),
            pl.BlockSpec((gw_rows, SSM_WIDTH), lambda i: (i, 0)),
        ],
        out_specs=[
            pl.BlockSpec((N_SLABS, tm, SLAB_CH), lambda i: (0, i, 0)),
            pl.BlockSpec((tm // 2, CONV_WIDTH), lambda i: (i, 0)),
            pl.BlockSpec((wo_rows // 2, D_MODEL), lambda i: (i, 0)),
            pl.BlockSpec((gw_rows // 2, SSM_WIDTH), lambda i: (i, 0)),
        ],
        out_shape=[
            jax.ShapeDtypeStruct((N_SLABS, SEQ, SLAB_CH), _F32),
            jax.ShapeDtypeStruct((SEQ // 2, CONV_WIDTH), jnp.int32),
            jax.ShapeDtypeStruct((w_out.shape[0] // 2, D_MODEL), jnp.int32),
            jax.ShapeDtypeStruct((glu_w.shape[0] // 2, SSM_WIDTH), jnp.int32),
        ],
        scratch_shapes=[
            pltpu.VMEM((SUBLANES, CONV_WIDTH), _F32),
            pltpu.VMEM((SUBLANES + tm, CONV_WIDTH), _F32),
        ],
        compiler_params=pltpu.CompilerParams(dimension_semantics=("arbitrary",), vmem_limit_bytes=VMEM_LIMIT),
        name="proj_shortconv",
    )(x, w_in_bf, w_in_bf, w_in_bf, w_in_bf, sconv_w, norm_conv_g, w_out, glu_w)


def _cmul_add(ar, ai, br, bi, cr, ci):
    return ar * br - ai * bi + cr, ar * bi + ai * br + ci


_TN = (((0,), (0,)), ((), ()))


def _s5_prepare(lre_ref, lim_ref, ldt_ref, bre_ref, bim_ref, cre_ref, cim_ref,
                bmat_ref, cmat_ref, lam_ref, bstage_ref, cstage_ref):
    lam_re, lam_im = lre_ref[...], lim_ref[...]
    dt = jnp.exp(ldt_ref[...])
    ang = lam_im * dt
    mag = jnp.exp(lam_re * dt)
    bar_re, bar_im = mag * jnp.cos(ang), mag * jnp.sin(ang)
    inv = 1.0 / (lam_re * lam_re + lam_im * lam_im)
    coef_re = ((bar_re - 1.0) * lam_re + bar_im * lam_im) * inv
    coef_im = (bar_im * lam_re - (bar_re - 1.0) * lam_im) * inv
    nmag = jnp.exp(SSM_STEPS * (lam_re * dt))
    pow_re, pow_im = nmag * jnp.cos(SSM_STEPS * ang), nmag * jnp.sin(SSM_STEPS * ang)
    eye = (lax.broadcasted_iota(jnp.int32, (SLAB_CH, SLAB_CH), 0)
           == lax.broadcasted_iota(jnp.int32, (SLAB_CH, SLAB_CH), 1)).astype(_BF16)
    for k in range(N_SLABS):
        bstage_ref[...] = jnp.zeros(bstage_ref.shape, _F32)
        cstage_ref[...] = jnp.zeros(cstage_ref.shape, _F32)
        for g in range(SLAB_GROUPS):
            grp = k * SLAB_GROUPS + g
            row = slice(grp, grp + 1)
            st = slice(g * STATE, (g + 1) * STATE)
            st_im = slice(SLAB_STATE + g * STATE, SLAB_STATE + (g + 1) * STATE)
            ch = slice(g * GROUP_CH, (g + 1) * GROUP_CH)
            for r, src in enumerate((bar_re, bar_im, pow_re, pow_im)):
                lam_ref[k, r:r + 1, st] = src[row, :]
            bstage_ref[ch, st] = bre_ref[grp]
            bstage_ref[ch, st_im] = bim_ref[grp]
            c_re, c_im = cre_ref[grp], cim_ref[grp]
            cstage_ref[ch, st] = c_re * coef_re[row, :] - c_im * coef_im[row, :]
            cstage_ref[ch, st_im] = -(c_re * coef_im[row, :] + c_im * coef_re[row, :])
        bmat_ref[k] = bstage_ref[...].astype(_BF16)
        cmat_ref[k] = lax.dot_general(cstage_ref[...].astype(_BF16), eye, _TN,
                                      preferred_element_type=_F32).astype(_BF16)


def _ssm_kernel(u_ref, lre_ref, lim_ref, ldt_ref, bre_ref, bim_ref, cre_ref, cim_ref, d_ref, y_ref,
                bu_ref, xb_ref, state_ref, bmat_ref, cmat_ref, lam_ref, bstage_ref, cstage_ref):
    nc, ns = SSM_CHUNKS, SSM_STEPS

    @pl.when(pl.program_id(0) == 0)
    def _():
        state_ref[...] = jnp.zeros(state_ref.shape, _F32)
        _s5_prepare(lre_ref, lim_ref, ldt_ref, bre_ref, bim_ref, cre_ref, cim_ref,
                    bmat_ref, cmat_ref, lam_ref, bstage_ref, cstage_ref)

    def expand(k):
        bu_ref[k] = jnp.dot(u_ref[k].astype(_BF16), bmat_ref[k], preferred_element_type=_F32)

    expand(0)
    expand(1)
    re = slice(0, SLAB_STATE)
    im = slice(SLAB_STATE, SLAB_LANES)
    for k in range(N_SLABS):
        lr = jnp.broadcast_to(lam_ref[k, 0:1, :], (nc, SLAB_STATE))
        li = jnp.broadcast_to(lam_ref[k, 1:2, :], (nc, SLAB_STATE))

        xr = jnp.zeros((nc, SLAB_STATE), _F32)
        xi = jnp.zeros((nc, SLAB_STATE), _F32)
        for s in range(ns):
            rows = slice(s * nc, (s + 1) * nc)
            xr, xi = _cmul_add(lr, li, xr, xi, bu_ref[k, rows, re], bu_ref[k, rows, im])

        tr = lam_ref[k, 2:3, :]
        ti = lam_ref[k, 3:4, :]
        cr = state_ref[k, 0:1, :]
        ci = state_ref[k, 1:2, :]
        starts_r, starts_i = [], []
        for c in range(nc):
            starts_r.append(cr)
            starts_i.append(ci)
            cr, ci = _cmul_add(tr, ti, cr, ci, xr[c:c + 1, :], xi[c:c + 1, :])
        state_ref[k, 0:1, :] = cr
        state_ref[k, 1:2, :] = ci
        xr = jnp.concatenate(starts_r, axis=0)
        xi = jnp.concatenate(starts_i, axis=0)

        for s in range(0, ns, 2):
            parts_r, parts_i = [], []
            for q in (s, s + 1):
                rows = slice(q * nc, (q + 1) * nc)
                xr, xi = _cmul_add(lr, li, xr, xi, bu_ref[k, rows, re], bu_ref[k, rows, im])
                parts_r.append(xr)
                parts_i.append(xi)
            rows2 = slice(s * nc, (s + 2) * nc)
            xb_ref[k, rows2, re] = jnp.concatenate(parts_r, axis=0).astype(_BF16)
            xb_ref[k, rows2, im] = jnp.concatenate(parts_i, axis=0).astype(_BF16)
        if k + 2 < N_SLABS:
            expand(k + 2)
        y_ref[k] = (jnp.dot(xb_ref[k], cmat_ref[k], preferred_element_type=_F32)
                    + d_ref[k] * u_ref[k])


def _ssm_call(u, lam_re, lam_im, log_dt, b_re, b_im, c_re, c_im, d_skip):
    tm = TM_MIX
    slab_rows = pl.BlockSpec((N_SLABS, tm, SLAB_CH), lambda i: (0, i, 0))
    return pl.pallas_call(
        _ssm_kernel,
        grid=(SEQ // tm,),
        in_specs=[
            slab_rows,
            _resident((GROUPS, STATE)),
            _resident((GROUPS, STATE)),
            _resident((GROUPS, 1)),
            _resident((GROUPS, GROUP_CH, STATE)),
            _resident((GROUPS, GROUP_CH, STATE)),
            _resident((GROUPS, GROUP_CH, STATE)),
            _resident((GROUPS, GROUP_CH, STATE)),
            _resident((N_SLABS, 1, SLAB_CH)),
        ],
        out_specs=slab_rows,
        out_shape=jax.ShapeDtypeStruct((N_SLABS, SEQ, SLAB_CH), _F32),
        scratch_shapes=[
            pltpu.VMEM((N_SLABS, tm, SLAB_LANES), _F32),
            pltpu.VMEM((N_SLABS, tm, SLAB_LANES), _BF16),
            pltpu.VMEM((N_SLABS, 2, SLAB_STATE), _F32),
            pltpu.VMEM((N_SLABS, SLAB_CH, SLAB_LANES), _BF16),
            pltpu.VMEM((N_SLABS, SLAB_LANES, SLAB_CH), _BF16),
            pltpu.VMEM((N_SLABS, 4, SLAB_STATE), _F32),
            pltpu.VMEM((SLAB_CH, SLAB_LANES), _F32),
            pltpu.VMEM((SLAB_CH, SLAB_LANES), _F32),
        ],
        compiler_params=pltpu.CompilerParams(dimension_semantics=("arbitrary",), vmem_limit_bytes=VMEM_LIMIT),
        name="s5_core",
    )(u, lam_re, lam_im, log_dt, b_re, b_im, c_re, c_im, d_skip)


def _outproj_kernel(y_ref, yc_ref, gw_ref, gb_ref, ng_ref, wt_ref, wb_ref, x_ref, g_ref, b_ref, h_ref, hb_ref):
    tm = x_ref.shape[0]
    hm = tm // MIX_ROW_SPLIT
    chunks_per_part = SSM_CHUNKS // MIX_ROW_SPLIT
    for r in range(MIX_ROW_SPLIT):
        rows = slice(r * hm, (r + 1) * hm)
        y = jnp.concatenate(
            [jnp.concatenate([y_ref[k, _chunk_rows(c), :] for k in range(N_SLABS)], axis=1)
             for c in range(r * chunks_per_part, (r + 1) * chunks_per_part)], axis=0)
        gl = jax.nn.gelu(y)
        z = jnp.dot(gl.astype(_BF16), _unpack_rows(gw_ref[...]), preferred_element_type=_F32) + gb_ref[...]
        ys = _rmsnorm(gl * jax.nn.sigmoid(z), ng_ref[...]).astype(_BF16)
        mix = jnp.dot(ys, _unpack_rows(wt_ref[...]), preferred_element_type=_F32)
        yc = _unpack_rows(yc_ref[r * hm // 2:(r + 1) * hm // 2, :])
        mix = mix + jnp.dot(yc, _unpack_rows(wb_ref[...]), preferred_element_type=_F32)
        h = _layernorm(ALPHA * x_ref[rows, :] + mix, g_ref[...], b_ref[...])
        h_ref[rows, :] = h
        hb_ref[r * hm // 2:(r + 1) * hm // 2, :] = _pack_rows(h.astype(_BF16))


def _outproj_call(y, yc, glu_w_bf, glu_b, norm_g, w_out_bf, x, ln_g, ln_b):
    tm = TM_MIX
    half = lambda j: pl.BlockSpec((SSM_WIDTH // 2, D_MODEL), lambda i, j=j: (j, 0), pipeline_mode=pl.Buffered(1))
    return pl.pallas_call(
        _outproj_kernel,
        grid=(SEQ // tm,),
        in_specs=[
            pl.BlockSpec((N_SLABS, tm, SLAB_CH), lambda i: (0, i, 0)),
            pl.BlockSpec((tm // 2, CONV_WIDTH), lambda i: (i, 0)),
            _resident((SSM_WIDTH // 2, SSM_WIDTH)),
            _resident((1, SSM_WIDTH)),
            _resident((1, SSM_WIDTH)),
            half(0), half(1),
            pl.BlockSpec((tm, D_MODEL), lambda i: (i, 0)),
            _resident((1, D_MODEL)),
            _resident((1, D_MODEL)),
        ],
        out_specs=[
            pl.BlockSpec((tm, D_MODEL), lambda i: (i, 0)),
            pl.BlockSpec((tm // 2, D_MODEL), lambda i: (i, 0)),
        ],
        out_shape=[
            jax.ShapeDtypeStruct((SEQ, D_MODEL), _F32),
            jax.ShapeDtypeStruct((SEQ // 2, D_MODEL), jnp.int32),
        ],
        compiler_params=pltpu.CompilerParams(dimension_semantics=("arbitrary",), vmem_limit_bytes=VMEM_LIMIT),
        name="glu_outproj_ln1",
    )(y, yc, glu_w_bf, glu_b, norm_g, w_out_bf, w_out_bf, x, ln_g, ln_b)


def _ffn_up_kernel(hb_ref, wg_ref, wu_ref, cw_ref, cb_ref, wd_ref, act_ref, wdb_ref,
                   wgb_ref, wub_ref, halo_ref, stage_ref):
    i = pl.program_id(1)

    wdb_ref[...] = _pack_rows(wd_ref[...].astype(_BF16))

    @pl.when(i == 0)
    def _():
        wgb_ref[...] = wg_ref[...].astype(_BF16)
        wub_ref[...] = wu_ref[...].astype(_BF16)

    _stage_open(stage_ref, halo_ref, i == 0)
    row0 = 0
    for n in FFN_UP_ROW_PARTS:
        hb = _unpack_rows(hb_ref[row0 // 2:(row0 + n) // 2, :])
        gate = jnp.dot(hb, wgb_ref[...], preferred_element_type=_F32)
        up = jnp.dot(hb, wub_ref[...], preferred_element_type=_F32)
        a = _conv3(stage_ref, gate, row0, cw_ref) + cb_ref[...]
        act_ref[row0 // 2:(row0 + n) // 2, :] = _pack_rows((jax.nn.silu(a) * up).astype(_BF16))
        row0 += n
    _stage_close(stage_ref, halo_ref)


def _ffn_up_call(hb, w_gate, w_up, conv_w, conv_b, w_down):
    tm, tf = TM_FFN_UP, TF_FFN_UP
    nk, ni = D_FF // tf, SEQ // tm
    wd_rows = D_FF // (nk * ni)
    return pl.pallas_call(
        _ffn_up_kernel,
        grid=(nk, ni),
        in_specs=[
            pl.BlockSpec((tm // 2, D_MODEL), lambda k, i: (i, 0)),
            pl.BlockSpec((D_MODEL, tf), lambda k, i: (0, k)),
            pl.BlockSpec((D_MODEL, tf), lambda k, i: (0, k)),
            pl.BlockSpec((3, tf), lambda k, i: (0, k)),
            pl.BlockSpec((1, tf), lambda k, i: (0, k)),
            pl.BlockSpec((wd_rows, D_MODEL), lambda k, i: (k * ni + i, 0)),
        ],
        out_specs=[
            pl.BlockSpec((tm // 2, tf), lambda k, i: (i, k)),
            pl.BlockSpec((wd_rows // 2, D_MODEL), lambda k, i: (k * ni + i, 0)),
        ],
        out_shape=[
            jax.ShapeDtypeStruct((SEQ // 2, D_FF), jnp.int32),
            jax.ShapeDtypeStruct((D_FF // 2, D_MODEL), jnp.int32),
        ],
        scratch_shapes=[
            pltpu.VMEM((D_MODEL, tf), _BF16),
            pltpu.VMEM((D_MODEL, tf), _BF16),
            pltpu.VMEM((SUBLANES, tf), _F32),
            pltpu.VMEM((SUBLANES + tm, tf), _F32),
        ],
        compiler_params=pltpu.CompilerParams(dimension_semantics=("arbitrary", "arbitrary"),
                                             vmem_limit_bytes=VMEM_LIMIT),
        name="convffn_up",
    )(hb, w_gate, w_up, conv_w, conv_b, w_down)


def _ffn_down_kernel(act_ref, wd_ref, h_ref, g_ref, b_ref, o_ref):
    tm = o_ref.shape[0]
    hm = tm // DOWN_ROW_SPLIT
    for r in range(DOWN_ROW_SPLIT):
        rows = slice(r * hm, (r + 1) * hm)
        act = _unpack_rows(act_ref[r * hm // 2:(r + 1) * hm // 2, :])
        f = jnp.dot(act, _unpack_rows(wd_ref[...]), preferred_element_type=_F32)
        o_ref[rows, :] = _layernorm(ALPHA * h_ref[rows, :] + f, g_ref[...], b_ref[...])


def _ffn_down_call(act, wd_bf, h, ln_g, ln_b):
    tm = TM_FFN_DOWN
    return pl.pallas_call(
        _ffn_down_kernel,
        grid=(SEQ // tm,),
        in_specs=[
            pl.BlockSpec((tm // 2, D_FF), lambda i: (i, 0)),
            _resident((D_FF // 2, D_MODEL)),
            pl.BlockSpec((tm, D_MODEL), lambda i: (i, 0)),
            _resident((1, D_MODEL)),
            _resident((1, D_MODEL)),
        ],
        out_specs=pl.BlockSpec((tm, D_MODEL), lambda i: (i, 0)),
        out_shape=jax.ShapeDtypeStruct((SEQ, D_MODEL), _F32),
        compiler_params=pltpu.CompilerParams(dimension_semantics=("arbitrary",),
                                             vmem_limit_bytes=VMEM_LIMIT_DOWN),
        name="convffn_down_ln2",
    )(act, wd_bf, h, ln_g, ln_b)


def kernel(x, w_in, ssm_lambda_re, ssm_lambda_im, ssm_log_dt, ssm_b_re, ssm_b_im, ssm_c_re, ssm_c_im,
           ssm_d, ssm_glu_w, ssm_glu_b, sconv_w, norm_ssm_g, norm_conv_g, w_out, ln1_g, ln1_b,
           ffn_w_gate, ffn_w_up, ffn_conv_w, ffn_conv_b, ffn_w_down, ln2_g, ln2_b):
    assert x.shape == (1, SEQ, D_MODEL) and w_in.shape[0] == DEPTH
    h = x[0]
    for l in range(DEPTH):
        row = lambda p: p[l].reshape(1, -1)
        u, yc, w_out_bf, glu_w_bf = _proj_call(h, w_in[l].astype(_BF16), sconv_w[l], row(norm_conv_g),
                                               w_out[l], ssm_glu_w[l])
        y = _ssm_call(u, ssm_lambda_re[l], ssm_lambda_im[l], ssm_log_dt[l].reshape(GROUPS, 1),
                      jnp.swapaxes(ssm_b_re[l], 1, 2), jnp.swapaxes(ssm_b_im[l], 1, 2),
                      ssm_c_re[l], ssm_c_im[l], ssm_d[l].reshape(N_SLABS, 1, SLAB_CH))
        h, hb = _outproj_call(y, yc, glu_w_bf, row(ssm_glu_b), row(norm_ssm_g),
                              w_out_bf, h, row(ln1_g), row(ln1_b))
        act, w_down_bf = _ffn_up_call(hb, ffn_w_gate[l], ffn_w_up[l], ffn_conv_w[l], row(ffn_conv_b),
                                      ffn_w_down[l])
        h = _ffn_down_call(act, w_down_bf, h, row(ln2_g), row(ln2_b))
    return h[None]
```

```python
import jax
import jax.numpy as jnp
from jax import lax
from jax.experimental import pallas as pl
from jax.experimental.pallas import tpu as pltpu

SEQ = 8192
D_MODEL = 2048
SSM_WIDTH = 1024
CONV_WIDTH = 1024
GROUP_CH = 16
GROUPS = SSM_WIDTH // GROUP_CH
STATE = 64
D_FF = 5632
DEPTH = 1
LN_EPS = 1e-5
RMS_EPS = 1e-6
ALPHA = (2.0 * DEPTH) ** 0.25

SUBLANES = 8
SLAB_GROUPS = 8
N_SLABS = GROUPS // SLAB_GROUPS
SLAB_CH = SLAB_GROUPS * GROUP_CH
SLAB_STATE = SLAB_GROUPS * STATE
SLAB_LANES = 2 * SLAB_STATE

TM_MIX = 512
SSM_CHUNKS = SUBLANES
SSM_STEPS = TM_MIX // SSM_CHUNKS
MIX_ROW_SPLIT = 2
W_IN_PIECE = 512
TM_FFN_UP = 2048
TF_FFN_UP = 512
TM_FFN_DOWN = 512
FFN_UP_ROW_PARTS = (1024, 1024)
DOWN_ROW_SPLIT = 2
VMEM_LIMIT = 56 * 1024 * 1024
VMEM_LIMIT_DOWN = 60 * 1024 * 1024

_F32 = jnp.float32
_BF16 = jnp.bfloat16


def _resident(shape):
    return pl.BlockSpec(shape, lambda *_: (0,) * len(shape), pipeline_mode=pl.Buffered(1))


def _layernorm(r, g, b):
    mu = jnp.mean(r, axis=-1, keepdims=True)
    rc = r - mu
    var = jnp.mean(rc * rc, axis=-1, keepdims=True)
    return rc * lax.rsqrt(var + LN_EPS) * g + b


def _rmsnorm(y, g):
    return y * lax.rsqrt(jnp.mean(y * y, axis=-1, keepdims=True) + RMS_EPS) * g


def _pack_rows(v):
    return pltpu.bitcast(v, jnp.int32)


def _unpack_rows(v):
    return pltpu.bitcast(v, _BF16)


def _conv3(stage_ref, cur, row0, w_ref):
    n = cur.shape[0]
    stage_ref[SUBLANES + row0:SUBLANES + row0 + n, :] = cur
    x1 = stage_ref[SUBLANES - 1 + row0:SUBLANES - 1 + row0 + n, :]
    x2 = stage_ref[SUBLANES - 2 + row0:SUBLANES - 2 + row0 + n, :]
    return w_ref[0:1, :] * x2 + w_ref[1:2, :] * x1 + w_ref[2:3, :] * cur


def _stage_open(stage_ref, halo_ref, first_tile):
    @pl.when(first_tile)
    def _():
        stage_ref[0:SUBLANES, :] = jnp.zeros((SUBLANES, stage_ref.shape[1]), _F32)

    @pl.when(jnp.logical_not(first_tile))
    def _():
        stage_ref[0:SUBLANES, :] = halo_ref[...]


def _stage_close(stage_ref, halo_ref):
    tm = stage_ref.shape[0] - SUBLANES
    halo_ref[...] = stage_ref[tm:tm + SUBLANES, :]


def _chunk_rows(c):
    return pl.ds(c, SSM_STEPS, stride=SSM_CHUNKS)


_W_IN_ORDER = (2, 3, 1, 0)
_W_IN_PIECES = tuple(j * SSM_WIDTH + h * W_IN_PIECE for j in _W_IN_ORDER for h in range(SSM_WIDTH // W_IN_PIECE))


def _w_in_piece_copy(w_hbm, wstage_ref, sem, p):
    slot = p % 2
    return pltpu.make_async_copy(w_hbm.at[:, pl.ds(_W_IN_PIECES[p], W_IN_PIECE)], wstage_ref.at[slot], sem.at[slot])


def _proj_kernel(x_ref, w_hbm, cw_ref, g_ref, wo_ref, gw_ref,
                 u_ref, yc_ref, wob_ref, gwb_ref, halo_ref, stage_ref, wbf_ref, wstage_ref, sem):
    wob_ref[...] = _pack_rows(wo_ref[...].astype(_BF16))
    gwb_ref[...] = _pack_rows(gw_ref[...].astype(_BF16))
    pieces_per_weight = SSM_WIDTH // W_IN_PIECE

    def body(first):
        def fetch_weight(n):
            if not first:
                return
            for p in range(n * pieces_per_weight, (n + 1) * pieces_per_weight):
                _w_in_piece_copy(w_hbm, wstage_ref, sem, p).wait()
                col = _W_IN_PIECES[p]
                wbf_ref[:, col:col + W_IN_PIECE] = wstage_ref[p % 2].astype(_BF16)
                if p + 2 < len(_W_IN_PIECES):
                    _w_in_piece_copy(w_hbm, wstage_ref, sem, p + 2).start()

        if first:
            _w_in_piece_copy(w_hbm, wstage_ref, sem, 0).start()
            _w_in_piece_copy(w_hbm, wstage_ref, sem, 1).start()
            stage_ref[0:SUBLANES, :] = jnp.zeros((SUBLANES, stage_ref.shape[1]), _F32)
        else:
            stage_ref[0:SUBLANES, :] = halo_ref[...]
        wu_ref, wb_ref, wc_ref, wv_ref = (wbf_ref.at[:, j * SSM_WIDTH:(j + 1) * SSM_WIDTH] for j in range(4))
        xb = x_ref[...].astype(_BF16)
        fetch_weight(0)
        gate_c = jnp.dot(xb, wc_ref[...], preferred_element_type=_F32)
        fetch_weight(1)
        v = jnp.dot(xb, wv_ref[...], preferred_element_type=_F32)
        conv = _conv3(stage_ref, gate_c * v, 0, cw_ref)
        _stage_close(stage_ref, halo_ref)
        fetch_weight(2)
        gate_b = jnp.dot(xb, wb_ref[...], preferred_element_type=_F32)
        yc_ref[...] = _pack_rows(_rmsnorm(gate_b * conv, g_ref[...]).astype(_BF16))
        fetch_weight(3)
        u = jnp.dot(xb, wu_ref[...], preferred_element_type=_F32)
        for k in range(N_SLABS):
            for c in range(SSM_CHUNKS):
                u_ref[k, _chunk_rows(c), :] = u[c * SSM_STEPS:(c + 1) * SSM_STEPS, k * SLAB_CH:(k + 1) * SLAB_CH]

    @pl.when(pl.program_id(0) == 0)
    def _():
        body(True)

    @pl.when(pl.program_id(0) > 0)
    def _():
        body(False)


def _proj_call(x, w_in, sconv_w, norm_conv_g, w_out, glu_w):
    tm = TM_MIX
    nt = SEQ // tm
    wo_rows, gw_rows = w_out.shape[0] // nt, glu_w.shape[0] // nt
    return pl.pallas_call(
        _proj_kernel,
        grid=(nt,),
        in_specs=[
            pl.BlockSpec((tm, D_MODEL), lambda i: (i, 0)),
            pl.BlockSpec(memory_space=pl.ANY),
            _resident((3, CONV_WIDTH)),
            _resident((1, CONV_WIDTH)),
            pl.BlockSpec((wo_rows, D_MODEL), lambda i: (i, 0)),
            pl.BlockSpec((gw_rows, SSM_WIDTH), lambda i: (i, 0)),
        ],
        out_specs=[
            pl.BlockSpec((N_SLABS, tm, SLAB_CH), lambda i: (0, i, 0)),
            pl.BlockSpec((tm // 2, CONV_WIDTH), lambda i: (i, 0)),
            pl.BlockSpec((wo_rows // 2, D_MODEL), lambda i: (i, 0)),
            pl.BlockSpec((gw_rows // 2, SSM_WIDTH), lambda i: (i, 0)),
        ],
        out_shape=[
            jax.ShapeDtypeStruct((N_SLABS, SEQ, SLAB_CH), _F32),
            jax.ShapeDtypeStruct((SEQ // 2, CONV_WIDTH), jnp.int32),
            jax.ShapeDtypeStruct((w_out.shape[0] // 2, D_MODEL), jnp.int32),
            jax.ShapeDtypeStruct((glu_w.shape[0] // 2, SSM_WIDTH), jnp.int32),
        ],
        scratch_shapes=[
            pltpu.VMEM((SUBLANES, CONV_WIDTH), _F32),
            pltpu.VMEM((SUBLANES + tm, CONV_WIDTH), _F32),
            pltpu.VMEM(w_in.shape, _BF16),
            pltpu.VMEM((2, w_in.shape[0], W_IN_PIECE), _F32),
            pltpu.SemaphoreType.DMA((2,)),
        ],
        compiler_params=pltpu.CompilerParams(dimension_semantics=("arbitrary",), vmem_limit_bytes=VMEM_LIMIT),
        name="proj_shortconv",
    )(x, w_in, sconv_w, norm_conv_g, w_out, glu_w)


def _cmul_add(ar, ai, br, bi, cr, ci):
    return ar * br - ai * bi + cr, ar * bi + ai * br + ci


_TN = (((0,), (0,)), ((), ()))


def _s5_prepare(lre_ref, lim_ref, ldt_ref, bre_ref, bim_ref, cre_ref, cim_ref,
                bmat_ref, cmat_ref, lam_ref, bstage_ref, cstage_ref):
    lam_re, lam_im = lre_ref[...], lim_ref[...]
    dt = jnp.exp(ldt_ref[...])
    ang = lam_im * dt
    mag = jnp.exp(lam_re * dt)
    bar_re, bar_im = mag * jnp.cos(ang), mag * jnp.sin(ang)
    inv = 1.0 / (lam_re * lam_re + lam_im * lam_im)
    coef_re = ((bar_re - 1.0) * lam_re + bar_im * lam_im) * inv
    coef_im = (bar_im * lam_re - (bar_re - 1.0) * lam_im) * inv
    nmag = jnp.exp(SSM_STEPS * (lam_re * dt))
    pow_re, pow_im = nmag * jnp.cos(SSM_STEPS * ang), nmag * jnp.sin(SSM_STEPS * ang)
    eye = (lax.broadcasted_iota(jnp.int32, (SLAB_CH, SLAB_CH), 0)
           == lax.broadcasted_iota(jnp.int32, (SLAB_CH, SLAB_CH), 1)).astype(_BF16)
    for k in range(N_SLABS):
        bstage_ref[...] = jnp.zeros(bstage_ref.shape, _F32)
        cstage_ref[...] = jnp.zeros(cstage_ref.shape, _F32)
        for g in range(SLAB_GROUPS):
            grp = k * SLAB_GROUPS + g
            row = slice(grp, grp + 1)
            st = slice(g * STATE, (g + 1) * STATE)
            st_im = slice(SLAB_STATE + g * STATE, SLAB_STATE + (g + 1) * STATE)
            ch = slice(g * GROUP_CH, (g + 1) * GROUP_CH)
            for r, src in enumerate((bar_re, bar_im, pow_re, pow_im)):
                lam_ref[k, r:r + 1, st] = src[row, :]
            bstage_ref[ch, st] = bre_ref[grp]
            bstage_ref[ch, st_im] = bim_ref[grp]
            c_re, c_im = cre_ref[grp], cim_ref[grp]
            cstage_ref[ch, st] = c_re * coef_re[row, :] - c_im * coef_im[row, :]
            cstage_ref[ch, st_im] = -(c_re * coef_im[row, :] + c_im * coef_re[row, :])
        bmat_ref[k] = bstage_ref[...].astype(_BF16)
        cmat_ref[k] = lax.dot_general(cstage_ref[...].astype(_BF16), eye, _TN,
                                      preferred_element_type=_F32).astype(_BF16)


def _ssm_kernel(u_ref, lre_ref, lim_ref, ldt_ref, bre_ref, bim_ref, cre_ref, cim_ref, d_ref, y_ref,
                bu_ref, xb_ref, state_ref, bmat_ref, cmat_ref, lam_ref, bstage_ref, cstage_ref):
    nc, ns = SSM_CHUNKS, SSM_STEPS

    @pl.when(pl.program_id(0) == 0)
    def _():
        state_ref[...] = jnp.zeros(state_ref.shape, _F32)
        _s5_prepare(lre_ref, lim_ref, ldt_ref, bre_ref, bim_ref, cre_ref, cim_ref,
                    bmat_ref, cmat_ref, lam_ref, bstage_ref, cstage_ref)

    def expand(k):
        bu_ref[k] = jnp.dot(u_ref[k].astype(_BF16), bmat_ref[k], preferred_element_type=_F32)

    expand(0)
    expand(1)
    re = slice(0, SLAB_STATE)
    im = slice(SLAB_STATE, SLAB_LANES)
    for k in range(N_SLABS):
        lr = jnp.broadcast_to(lam_ref[k, 0:1, :], (nc, SLAB_STATE))
        li = jnp.broadcast_to(lam_ref[k, 1:2, :], (nc, SLAB_STATE))

        xr = jnp.zeros((nc, SLAB_STATE), _F32)
        xi = jnp.zeros((nc, SLAB_STATE), _F32)
        for s in range(ns):
            rows = slice(s * nc, (s + 1) * nc)
            xr, xi = _cmul_add(lr, li, xr, xi, bu_ref[k, rows, re], bu_ref[k, rows, im])

        tr = lam_ref[k, 2:3, :]
        ti = lam_ref[k, 3:4, :]
        cr = state_ref[k, 0:1, :]
        ci = state_ref[k, 1:2, :]
        starts_r, starts_i = [], []
        for c in range(nc):
            starts_r.append(cr)
            starts_i.append(ci)
            cr, ci = _cmul_add(tr, ti, cr, ci, xr[c:c + 1, :], xi[c:c + 1, :])
        state_ref[k, 0:1, :] = cr
        state_ref[k, 1:2, :] = ci
        xr = jnp.concatenate(starts_r, axis=0)
        xi = jnp.concatenate(starts_i, axis=0)

        for s in range(0, ns, 2):
            parts_r, parts_i = [], []
            for q in (s, s + 1):
                rows = slice(q * nc, (q + 1) * nc)
                xr, xi = _cmul_add(lr, li, xr, xi, bu_ref[k, rows, re], bu_ref[k, rows, im])
                parts_r.append(xr)
                parts_i.append(xi)
            rows2 = slice(s * nc, (s + 2) * nc)
            xb_ref[k, rows2, re] = jnp.concatenate(parts_r, axis=0).astype(_BF16)
            xb_ref[k, rows2, im] = jnp.concatenate(parts_i, axis=0).astype(_BF16)
        if k + 2 < N_SLABS:
            expand(k + 2)
        y_ref[k] = (jnp.dot(xb_ref[k], cmat_ref[k], preferred_element_type=_F32)
                    + d_ref[k] * u_ref[k])


def _ssm_call(u, lam_re, lam_im, log_dt, b_re, b_im, c_re, c_im, d_skip):
    tm = TM_MIX
    slab_rows = pl.BlockSpec((N_SLABS, tm, SLAB_CH), lambda i: (0, i, 0))
    return pl.pallas_call(
        _ssm_kernel,
        grid=(SEQ // tm,),
        in_specs=[
            slab_rows,
            _resident((GROUPS, STATE)),
            _resident((GROUPS, STATE)),
            _resident((GROUPS, 1)),
            _resident((GROUPS, GROUP_CH, STATE)),
            _resident((GROUPS, GROUP_CH, STATE)),
            _resident((GROUPS, GROUP_CH, STATE)),
            _resident((GROUPS, GROUP_CH, STATE)),
            _resident((N_SLABS, 1, SLAB_CH)),
        ],
        out_specs=slab_rows,
        out_shape=jax.ShapeDtypeStruct((N_SLABS, SEQ, SLAB_CH), _F32),
        scratch_shapes=[
            pltpu.VMEM((N_SLABS, tm, SLAB_LANES), _F32),
            pltpu.VMEM((N_SLABS, tm, SLAB_LANES), _BF16),
            pltpu.VMEM((N_SLABS, 2, SLAB_STATE), _F32),
            pltpu.VMEM((N_SLABS, SLAB_CH, SLAB_LANES), _BF16),
            pltpu.VMEM((N_SLABS, SLAB_LANES, SLAB_CH), _BF16),
            pltpu.VMEM((N_SLABS, 4, SLAB_STATE), _F32),
            pltpu.VMEM((SLAB_CH, SLAB_LANES), _F32),
            pltpu.VMEM((SLAB_CH, SLAB_LANES), _F32),
        ],
        compiler_params=pltpu.CompilerParams(dimension_semantics=("arbitrary",), vmem_limit_bytes=VMEM_LIMIT),
        name="s5_core",
    )(u, lam_re, lam_im, log_dt, b_re, b_im, c_re, c_im, d_skip)


def _outproj_kernel(y_ref, yc_ref, gw_ref, gb_ref, ng_ref, wt_ref, wb_ref, x_ref, g_ref, b_ref, h_ref, hb_ref):
    tm = x_ref.shape[0]
    hm = tm // MIX_ROW_SPLIT
    chunks_per_part = SSM_CHUNKS // MIX_ROW_SPLIT
    for r in range(MIX_ROW_SPLIT):
        rows = slice(r * hm, (r + 1) * hm)
        y = jnp.concatenate(
            [jnp.concatenate([y_ref[k, _chunk_rows(c), :] for k in range(N_SLABS)], axis=1)
             for c in range(r * chunks_per_part, (r + 1) * chunks_per_part)], axis=0)
        gl = jax.nn.gelu(y)
        z = jnp.dot(gl.astype(_BF16), _unpack_rows(gw_ref[...]), preferred_element_type=_F32) + gb_ref[...]
        ys = _rmsnorm(gl * jax.nn.sigmoid(z), ng_ref[...]).astype(_BF16)
        mix = jnp.dot(ys, _unpack_rows(wt_ref[...]), preferred_element_type=_F32)
        yc = _unpack_rows(yc_ref[r * hm // 2:(r + 1) * hm // 2, :])
        mix = mix + jnp.dot(yc, _unpack_rows(wb_ref[...]), preferred_element_type=_F32)
        h = _layernorm(ALPHA * x_ref[rows, :] + mix, g_ref[...], b_ref[...])
        h_ref[rows, :] = h
        hb_ref[r * hm // 2:(r + 1) * hm // 2, :] = _pack_rows(h.astype(_BF16))


def _outproj_call(y, yc, glu_w_bf, glu_b, norm_g, w_out_bf, x, ln_g, ln_b):
    tm = TM_MIX
    half = lambda j: pl.BlockSpec((SSM_WIDTH // 2, D_MODEL), lambda i, j=j: (j, 0), pipeline_mode=pl.Buffered(1))
    return pl.pallas_call(
        _outproj_kernel,
        grid=(SEQ // tm,),
        in_specs=[
            pl.BlockSpec((N_SLABS, tm, SLAB_CH), lambda i: (0, i, 0)),
            pl.BlockSpec((tm // 2, CONV_WIDTH), lambda i: (i, 0)),
            _resident((SSM_WIDTH // 2, SSM_WIDTH)),
            _resident((1, SSM_WIDTH)),
            _resident((1, SSM_WIDTH)),
            half(0), half(1),
            pl.BlockSpec((tm, D_MODEL), lambda i: (i, 0)),
            _resident((1, D_MODEL)),
            _resident((1, D_MODEL)),
        ],
        out_specs=[
            pl.BlockSpec((tm, D_MODEL), lambda i: (i, 0)),
            pl.BlockSpec((tm // 2, D_MODEL), lambda i: (i, 0)),
        ],
        out_shape=[
            jax.ShapeDtypeStruct((SEQ, D_MODEL), _F32),
            jax.ShapeDtypeStruct((SEQ // 2, D_MODEL), jnp.int32),
        ],
        compiler_params=pltpu.CompilerParams(dimension_semantics=("arbitrary",), vmem_limit_bytes=VMEM_LIMIT),
        name="glu_outproj_ln1",
    )(y, yc, glu_w_bf, glu_b, norm_g, w_out_bf, w_out_bf, x, ln_g, ln_b)


def _ffn_up_kernel(hb_ref, wg_ref, wu_ref, cw_ref, cb_ref, wd_ref, act_ref, wdb_ref,
                   wgb_ref, wub_ref, halo_ref, stage_ref):
    i = pl.program_id(1)

    wdb_ref[...] = _pack_rows(wd_ref[...].astype(_BF16))

    @pl.when(i == 0)
    def _():
        wgb_ref[...] = wg_ref[...].astype(_BF16)
        wub_ref[...] = wu_ref[...].astype(_BF16)

    _stage_open(stage_ref, halo_ref, i == 0)
    row0 = 0
    for n in FFN_UP_ROW_PARTS:
        hb = _unpack_rows(hb_ref[row0 // 2:(row0 + n) // 2, :])
        gate = jnp.dot(hb, wgb_ref[...], preferred_element_type=_F32)
        up = jnp.dot(hb, wub_ref[...], preferred_element_type=_F32)
        a = _conv3(stage_ref, gate, row0, cw_ref) + cb_ref[...]
        act_ref[row0 // 2:(row0 + n) // 2, :] = _pack_rows((jax.nn.silu(a) * up).astype(_BF16))
        row0 += n
    _stage_close(stage_ref, halo_ref)


def _ffn_up_call(hb, w_gate, w_up, conv_w, conv_b, w_down):
    tm, tf = TM_FFN_UP, TF_FFN_UP
    nk, ni = D_FF // tf, SEQ // tm
    wd_rows = D_FF // (nk * ni)
    return pl.pallas_call(
        _ffn_up_kernel,
        grid=(nk, ni),
        in_specs=[
            pl.BlockSpec((tm // 2, D_MODEL), lambda k, i: (i, 0)),
            pl.BlockSpec((D_MODEL, tf), lambda k, i: (0, k)),
            pl.BlockSpec((D_MODEL, tf), lambda k, i: (0, k)),
            pl.BlockSpec((3, tf), lambda k, i: (0, k)),
            pl.BlockSpec((1, tf), lambda k, i: (0, k)),
            pl.BlockSpec((wd_rows, D_MODEL), lambda k, i: (k * ni + i, 0)),
        ],
        out_specs=[
            pl.BlockSpec((tm // 2, tf), lambda k, i: (i, k)),
            pl.BlockSpec((wd_rows // 2, D_MODEL), lambda k, i: (k * ni + i, 0)),
        ],
        out_shape=[
            jax.ShapeDtypeStruct((SEQ // 2, D_FF), jnp.int32),
            jax.ShapeDtypeStruct((D_FF // 2, D_MODEL), jnp.int32),
        ],
        scratch_shapes=[
            pltpu.VMEM((D_MODEL, tf), _BF16),
            pltpu.VMEM((D_MODEL, tf), _BF16),
            pltpu.VMEM((SUBLANES, tf), _F32),
            pltpu.VMEM((SUBLANES + tm, tf), _F32),
        ],
        compiler_params=pltpu.CompilerParams(dimension_semantics=("arbitrary", "arbitrary"),
                                             vmem_limit_bytes=VMEM_LIMIT),
        name="convffn_up",
    )(hb, w_gate, w_up, conv_w, conv_b, w_down)


def _ffn_down_kernel(act_ref, wd_ref, h_ref, g_ref, b_ref, o_ref):
    tm = o_ref.shape[0]
    hm = tm // DOWN_ROW_SPLIT
    for r in range(DOWN_ROW_SPLIT):
        rows = slice(r * hm, (r + 1) * hm)
        act = _unpack_rows(act_ref[r * hm // 2:(r + 1) * hm // 2, :])
        f = jnp.dot(act, _unpack_rows(wd_ref[...]), preferred_element_type=_F32)
        o_ref[rows, :] = _layernorm(ALPHA * h_ref[rows, :] + f, g_ref[...], b_ref[...])


def _ffn_down_call(act, wd_bf, h, ln_g, ln_b):
    tm = TM_FFN_DOWN
    return pl.pallas_call(
        _ffn_down_kernel,
        grid=(SEQ // tm,),
        in_specs=[
            pl.BlockSpec((tm // 2, D_FF), lambda i: (i, 0)),
            _resident((D_FF // 2, D_MODEL)),
            pl.BlockSpec((tm, D_MODEL), lambda i: (i, 0)),
            _resident((1, D_MODEL)),
            _resident((1, D_MODEL)),
        ],
        out_specs=pl.BlockSpec((tm, D_MODEL), lambda i: (i, 0)),
        out_shape=jax.ShapeDtypeStruct((SEQ, D_MODEL), _F32),
        compiler_params=pltpu.CompilerParams(dimension_semantics=("arbitrary",),
                                             vmem_limit_bytes=VMEM_LIMIT_DOWN),
        name="convffn_down_ln2",
    )(act, wd_bf, h, ln_g, ln_b)


def kernel(x, w_in, ssm_lambda_re, ssm_lambda_im, ssm_log_dt, ssm_b_re, ssm_b_im, ssm_c_re, ssm_c_im,
           ssm_d, ssm_glu_w, ssm_glu_b, sconv_w, norm_ssm_g, norm_conv_g, w_out, ln1_g, ln1_b,
           ffn_w_gate, ffn_w_up, ffn_conv_w, ffn_conv_b, ffn_w_down, ln2_g, ln2_b):
    assert x.shape == (1, SEQ, D_MODEL) and w_in.shape[0] == DEPTH
    h = x[0]
    for l in range(DEPTH):
        row = lambda p: p[l].reshape(1, -1)
        u, yc, w_out_bf, glu_w_bf = _proj_call(h, w_in[l], sconv_w[l], row(norm_conv_g),
                                               w_out[l], ssm_glu_w[l])
        y = _ssm_call(u, ssm_lambda_re[l], ssm_lambda_im[l], ssm_log_dt[l].reshape(GROUPS, 1),
                      jnp.swapaxes(ssm_b_re[l], 1, 2), jnp.swapaxes(ssm_b_im[l], 1, 2),
                      ssm_c_re[l], ssm_c_im[l], ssm_d[l].reshape(N_SLABS, 1, SLAB_CH))
        h, hb = _outproj_call(y, yc, glu_w_bf, row(ssm_glu_b), row(norm_ssm_g),
                              w_out_bf, h, row(ln1_g), row(ln1_b))
        act, w_down_bf = _ffn_up_call(hb, ffn_w_gate[l], ffn_w_up[l], ffn_conv_w[l], row(ffn_conv_b),
                                      ffn_w_down[l])
        h = _ffn_down_call(act, w_down_bf, h, row(ln2_g), row(ln2_b))
    return h[None]
```
